```python
import jax, jax.numpy as jnp
from jax import lax
import numpy as np

D_MODEL = 1024
BATCH = 2
SEQ = 8192
DEPTH = 1

N_META = 16
N_HEADS = 8
N_KV_HEADS = 2
HEAD_DIM = 128
Q_PER_KV = N_HEADS // N_KV_HEADS
ATTN_WIDTH = N_HEADS * HEAD_DIM
KV_WIDTH = N_KV_HEADS * HEAD_DIM
WINDOW = 128
BLOCK = 128
LRU_WIDTH = D_MODEL
LRU_BLOCKS = 8
LRU_BLOCK_DIM = LRU_WIDTH // LRU_BLOCKS
CONV_WIDTH = 4
LRU_C = 8.0
N_GROUPS = 4
EXPERTS_PER_GROUP = 4
N_EXPERTS = N_GROUPS * EXPERTS_PER_GROUP
TOP_K = 2
EXPERT_FF = 512
N_BRANCHES = 2
IN_WIDTH = ATTN_WIDTH + 2 * KV_WIDTH + 2 * LRU_WIDTH + N_BRANCHES * D_MODEL
IN_SPLITS = (ATTN_WIDTH, ATTN_WIDTH + KV_WIDTH, ATTN_WIDTH + 2 * KV_WIDTH,
             ATTN_WIDTH + 2 * KV_WIDTH + LRU_WIDTH, ATTN_WIDTH + 2 * KV_WIDTH + 2 * LRU_WIDTH)
EPS = 1e-6
NEG_INF = -1e30

kernel_name = 'hybrid_rglru_swa_hmoe_encoder_block'


def rms_norm(x, g):
    xf = x.astype(jnp.float32)
    y = xf * lax.rsqrt(jnp.mean(xf * xf, axis=-1, keepdims=True) + EPS)
    return (y * g.astype(jnp.float32)).astype(x.dtype)


def alibi_slopes():
    return jnp.exp2(-8.0 * (jnp.arange(N_HEADS, dtype=jnp.float32) + 1.0) / N_HEADS)


def banded_window_attention(q, k, v, sink):
    batch, t_len = q.shape[0], q.shape[1]
    lead = BLOCK - N_META
    p_len = t_len + lead
    nb = p_len // BLOCK
    pad = ((0, 0), (lead, 0), (0, 0), (0, 0))
    qb = jnp.pad(q, pad).reshape(batch, nb, BLOCK, N_KV_HEADS, Q_PER_KV, HEAD_DIM)

    def band(z):
        zb = jnp.pad(z, pad).reshape(batch, nb, BLOCK, N_KV_HEADS, HEAD_DIM)
        zb = jnp.pad(zb, ((0, 0), (1, 1), (0, 0), (0, 0), (0, 0)))
        return jnp.concatenate([zb[:, :-2], zb[:, 1:-1], zb[:, 2:]], axis=2)

    kb, vb = band(k), band(v)
    blk = jnp.arange(nb)[:, None]
    q_pos = blk * BLOCK + jnp.arange(BLOCK)[None, :] - lead
    k_idx = (blk - 1) * BLOCK + jnp.arange(3 * BLOCK)[None, :]
    k_pos = k_idx - lead
    k_ok = (k_pos >= N_META) & (k_idx < p_len)
    dist = jnp.abs(q_pos[:, :, None] - k_pos[:, None, :])
    band_ok = (dist <= WINDOW) & k_ok[:, None, :]
    scale = HEAD_DIM ** -0.5
    slopes = alibi_slopes().reshape(N_KV_HEADS, Q_PER_KV)
    s_band = jnp.einsum('bnqkgd,bnskd->bnkgqs', qb, kb, preferred_element_type=jnp.float32) * scale
    s_band = s_band - slopes[None, None, :, :, None, None] * dist.astype(jnp.float32)[None, :, None, None]
    s_band = jnp.where(band_ok[None, :, None, None], s_band, NEG_INF)
    k_meta, v_meta = k[:, :N_META], v[:, :N_META]
    s_meta = jnp.einsum('bnqkgd,bmkd->bnkgqm', qb, k_meta, preferred_element_type=jnp.float32) * scale
    s_sink = jnp.broadcast_to(sink.astype(jnp.float32).reshape(1, 1, N_KV_HEADS, Q_PER_KV, 1, 1),
                              s_band.shape[:-1] + (1,))
    probs = jax.nn.softmax(jnp.concatenate([s_band, s_meta, s_sink], axis=-1), axis=-1).astype(v.dtype)
    out = (jnp.einsum('bnkgqs,bnskd->bnqkgd', probs[..., :3 * BLOCK], vb)
           + jnp.einsum('bnkgqm,bmkd->bnqkgd', probs[..., 3 * BLOCK:3 * BLOCK + N_META], v_meta))
    return out.reshape(batch, p_len, ATTN_WIDTH)[:, lead:]


def centred_depthwise_conv(x, w, b):
    y = lax.conv_general_dilated(x, w[:, None, :].astype(x.dtype), window_strides=(1,),
                                 padding=[(CONV_WIDTH // 2, CONV_WIDTH - 1 - CONV_WIDTH // 2)],
                                 dimension_numbers=('NWC', 'WIO', 'NWC'),
                                 feature_group_count=x.shape[-1])
    return y + b


def linear_combine(left, right):
    a1, b1 = left
    a2, b2 = right
    return a1 * a2, a2 * b1 + b2


def rg_lru(xc, w_a, b_a, w_x, b_x, lam, reverse):
    batch, t_len = xc.shape[0], xc.shape[1]
    xb = xc.reshape(batch, t_len, LRU_BLOCKS, LRU_BLOCK_DIM)
    r = jax.nn.sigmoid((jnp.einsum('btnd,nde->btne', xb, w_a).reshape(batch, t_len, LRU_WIDTH) + b_a).astype(jnp.float32))
    i = jax.nn.sigmoid((jnp.einsum('btnd,nde->btne', xb, w_x).reshape(batch, t_len, LRU_WIDTH) + b_x).astype(jnp.float32))
    log_a = -LRU_C * r * jax.nn.softplus(-lam.astype(jnp.float32))
    a = jnp.exp(log_a)
    u = jnp.sqrt(-jnp.expm1(2.0 * log_a)) * (i * xc.astype(jnp.float32))
    _, h = lax.associative_scan(linear_combine, (a, u), axis=1, reverse=reverse)
    return h


def hierarchical_moe(x2, w_group, b_group, w_router, b_router, w_gate, w_up, w_down):
    n = x2.shape[0]
    g_prob = jax.nn.softmax((x2 @ w_group + b_group).astype(jnp.float32), axis=-1)
    g_top_p, g_idx = lax.top_k(g_prob, 1)
    e_logits = (x2 @ w_router + b_router).astype(jnp.float32).reshape(n, N_GROUPS, EXPERTS_PER_GROUP)
    e_in_group = jnp.take_along_axis(e_logits, g_idx[:, :, None], axis=1)[:, 0]
    e_top, e_idx = lax.top_k(e_in_group, TOP_K)
    weights = jax.nn.softmax(e_top, axis=-1) * g_top_p
    expert_ids = g_idx * EXPERTS_PER_GROUP + e_idx
    gates = jnp.einsum('nk,nke->ne', weights,
                       jax.nn.one_hot(expert_ids, N_EXPERTS, dtype=jnp.float32)).astype(x2.dtype)
    out = jnp.zeros_like(x2)
    for e in range(N_EXPERTS):
        hdn = jax.nn.silu(x2 @ w_gate[e]) * (x2 @ w_up[e])
        out = out + gates[:, e:e + 1] * (hdn @ w_down[e])
    return out


def setup_inputs(seed: int = 0) -> dict:
    key = jax.random.key(seed)
    ks = jax.random.split(key, 24)
    f32 = jnp.float32

    def nrm(k, shape, scale):
        return jax.random.normal(k, shape, f32) * scale

    a0 = jax.random.uniform(ks[12], (DEPTH, 2, LRU_WIDTH), f32, 0.9, 0.999)
    return {
        'x': nrm(ks[0], (BATCH, SEQ, D_MODEL), 1.0),
        'meta_tokens': nrm(ks[1], (N_META, D_MODEL), 1.0),
        'norm_mix_g': 1.0 + nrm(ks[2], (DEPTH, D_MODEL), 0.02),
        'w_in': nrm(ks[3], (DEPTH, D_MODEL, IN_WIDTH), D_MODEL ** -0.5),
        'conv_w': nrm(ks[4], (DEPTH, CONV_WIDTH, LRU_WIDTH), CONV_WIDTH ** -0.5),
        'conv_b': nrm(ks[5], (DEPTH, LRU_WIDTH), 0.01),
        'lru_w_a': nrm(ks[6], (DEPTH, 2, LRU_BLOCKS, LRU_BLOCK_DIM, LRU_BLOCK_DIM), LRU_BLOCK_DIM ** -0.5),
        'lru_b_a': nrm(ks[7], (DEPTH, 2, LRU_WIDTH), 0.01),
        'lru_w_x': nrm(ks[8], (DEPTH, 2, LRU_BLOCKS, LRU_BLOCK_DIM, LRU_BLOCK_DIM), LRU_BLOCK_DIM ** -0.5),
        'lru_b_x': nrm(ks[9], (DEPTH, 2, LRU_WIDTH), 0.01),
        'lru_lambda': jnp.log(a0) - jnp.log1p(-a0),
        'attn_sink': nrm(ks[10], (DEPTH, N_HEADS), 0.5),
        'w_attn_branch': nrm(ks[11], (DEPTH, ATTN_WIDTH, D_MODEL), ATTN_WIDTH ** -0.5),
        'w_rec_branch': nrm(ks[13], (DEPTH, LRU_WIDTH, D_MODEL), LRU_WIDTH ** -0.5),
        'w_out': nrm(ks[14], (DEPTH, D_MODEL, D_MODEL), D_MODEL ** -0.5),
        'norm_ffn_g': 1.0 + nrm(ks[15], (DEPTH, D_MODEL), 0.02),
        'w_group': nrm(ks[16], (DEPTH, D_MODEL, N_GROUPS), D_MODEL ** -0.5),
        'b_group': nrm(ks[17], (DEPTH, N_GROUPS), 0.01),
        'w_router': nrm(ks[18], (DEPTH, D_MODEL, N_EXPERTS), D_MODEL ** -0.5),
        'b_router': nrm(ks[19], (DEPTH, N_EXPERTS), 0.01),
        'moe_w_gate': nrm(ks[20], (DEPTH, N_EXPERTS, D_MODEL, EXPERT_FF), D_MODEL ** -0.5),
        'moe_w_up': nrm(ks[21], (DEPTH, N_EXPERTS, D_MODEL, EXPERT_FF), D_MODEL ** -0.5),
        'moe_w_down': nrm(ks[22], (DEPTH, N_EXPERTS, EXPERT_FF, D_MODEL), EXPERT_FF ** -0.5),
        'final_norm_g': 1.0 + nrm(ks[23], (D_MODEL,), 0.02),
    }


def reference(x, meta_tokens, norm_mix_g, w_in, conv_w, conv_b, lru_w_a, lru_b_a, lru_w_x, lru_b_x,
              lru_lambda, attn_sink, w_attn_branch, w_rec_branch, w_out, norm_ffn_g, w_group, b_group,
              w_router, b_router, moe_w_gate, moe_w_up, moe_w_down, final_norm_g):
    batch = x.shape[0]
    meta = jnp.broadcast_to(meta_tokens[None].astype(x.dtype), (batch, N_META, D_MODEL))
    h = jnp.concatenate([meta, x], axis=1)
    t_len = h.shape[1]
    for l in range(DEPTH):
        n = rms_norm(h, norm_mix_g[l])
        z = n @ w_in[l]
        q, k, v, xr, yr, gate_logits = jnp.split(z, IN_SPLITS, axis=-1)
        q = q.reshape(batch, t_len, N_HEADS, HEAD_DIM)
        k = k.reshape(batch, t_len, N_KV_HEADS, HEAD_DIM)
        v = v.reshape(batch, t_len, N_KV_HEADS, HEAD_DIM)
        attn = banded_window_attention(q, k, v, attn_sink[l]) @ w_attn_branch[l]
        xc = centred_depthwise_conv(xr, conv_w[l], conv_b[l])
        rec = (rg_lru(xc, lru_w_a[l, 0], lru_b_a[l, 0], lru_w_x[l, 0], lru_b_x[l, 0], lru_lambda[l, 0], False)
               + rg_lru(xc, lru_w_a[l, 1], lru_b_a[l, 1], lru_w_x[l, 1], lru_b_x[l, 1], lru_lambda[l, 1], True))
        rec = (rec.astype(x.dtype) * jax.nn.gelu(yr)) @ w_rec_branch[l]
        g_attn, g_rec = jnp.split(jax.nn.sigmoid(gate_logits), N_BRANCHES, axis=-1)
        h = h + (g_attn * attn + g_rec * rec) @ w_out[l]
        n2 = rms_norm(h, norm_ffn_g[l]).reshape(-1, D_MODEL)
        h = h + hierarchical_moe(n2, w_group[l], b_group[l], w_router[l], b_router[l],
                                 moe_w_gate[l], moe_w_up[l], moe_w_down[l]).reshape(h.shape)
    return rms_norm(h, final_norm_g)[:, N_META:]
```

```python
import functools
import math

import jax
import jax.numpy as jnp
from jax import lax
from jax.experimental import pallas as pl
from jax.experimental.pallas import tpu as pltpu

D_MODEL = 1024
N_META = 16
N_HEADS = 8
N_KV_HEADS = 2
HEAD_DIM = 128
Q_PER_KV = N_HEADS // N_KV_HEADS
ATTN_WIDTH = N_HEADS * HEAD_DIM
KV_WIDTH = N_KV_HEADS * HEAD_DIM
WINDOW = 128
BLOCK = 128
LRU_WIDTH = D_MODEL
LRU_BLOCKS = 8
LRU_BLOCK_DIM = LRU_WIDTH // LRU_BLOCKS
CONV_WIDTH = 4
LRU_C = 8.0
N_GROUPS = 4
EXPERTS_PER_GROUP = 4
N_EXPERTS = N_GROUPS * EXPERTS_PER_GROUP
EXPERT_FF = 512
IN_WIDTH = ATTN_WIDTH + 2 * KV_WIDTH + 2 * LRU_WIDTH + 2 * D_MODEL
EPS = 1e-6
NEG_INF = -1e30

LANES = 128
SUBLANES = 8
VMEM_LIMIT = 56 * 1024 * 1024

BF16 = jnp.bfloat16
F32 = jnp.float32


def _params(n_grid_dims):
    return pltpu.CompilerParams(
        dimension_semantics=("arbitrary",) * n_grid_dims,
        vmem_limit_bytes=VMEM_LIMIT,
    )


def _sigmoid(x):
    return 0.5 * jnp.tanh(0.5 * x) + 0.5


def _gelu_tanh(x):
    c = math.sqrt(2.0 / math.pi)
    return 0.5 * x * (1.0 + jnp.tanh(c * (x + 0.044715 * (x * x * x))))


def _rms_norm(xf, g):
    ms = jnp.mean(xf * xf, axis=-1, keepdims=True)
    return xf * lax.rsqrt(ms + EPS) * g


_IN_CHUNK = 512


def _in_proj_kernel(x_ref, g_ref, w_ref, q_ref, k_ref, v_ref, xr_ref, gy_ref, ga_ref, gr_ref):
    n = _rms_norm(x_ref[...], g_ref[...]).astype(BF16)

    def proj(c0, width):
        return jnp.dot(n, w_ref[:, c0:c0 + width], preferred_element_type=F32)

    c = 0
    for j in range(ATTN_WIDTH // _IN_CHUNK):
        q_ref[:, j * _IN_CHUNK:(j + 1) * _IN_CHUNK] = proj(c, _IN_CHUNK).astype(BF16)
        c += _IN_CHUNK
    kv = proj(c, 2 * KV_WIDTH)
    k_ref[...] = kv[:, :KV_WIDTH].astype(BF16)
    v_ref[...] = kv[:, KV_WIDTH:].astype(BF16)
    c += 2 * KV_WIDTH
    for j in range(LRU_WIDTH // _IN_CHUNK):
        xr_ref[:, j * _IN_CHUNK:(j + 1) * _IN_CHUNK] = proj(c, _IN_CHUNK)
        c += _IN_CHUNK
    for j in range(LRU_WIDTH // _IN_CHUNK):
        gy_ref[:, j * _IN_CHUNK:(j + 1) * _IN_CHUNK] = _gelu_tanh(proj(c, _IN_CHUNK)).astype(BF16)
        c += _IN_CHUNK
    for ref in (ga_ref, gr_ref):
        for j in range(D_MODEL // _IN_CHUNK):
            ref[:, j * _IN_CHUNK:(j + 1) * _IN_CHUNK] = _sigmoid(proj(c, _IN_CHUNK)).astype(BF16)
            c += _IN_CHUNK


def _in_proj(x2, g, w_bf, tm):
    n_rows = x2.shape[0]
    row = lambda w: pl.BlockSpec((tm, w), lambda i: (i, 0))
    full = lambda a: pl.BlockSpec(a.shape, lambda i: (0,) * a.ndim)
    out_shapes = (
        jax.ShapeDtypeStruct((n_rows, ATTN_WIDTH), BF16),
        jax.ShapeDtypeStruct((n_rows, KV_WIDTH), BF16),
        jax.ShapeDtypeStruct((n_rows, KV_WIDTH), BF16),
        jax.ShapeDtypeStruct((n_rows, LRU_WIDTH), F32),
        jax.ShapeDtypeStruct((n_rows, LRU_WIDTH), BF16),
        jax.ShapeDtypeStruct((n_rows, D_MODEL), BF16),
        jax.ShapeDtypeStruct((n_rows, D_MODEL), BF16),
    )
    return pl.pallas_call(
        _in_proj_kernel,
        grid=(n_rows // tm,),
        in_specs=[row(D_MODEL), full(g), full(w_bf)],
        out_specs=tuple(row(s.shape[1]) for s in out_shapes),
        out_shape=out_shapes,
        compiler_params=_params(1),
        name="in_proj",
    )(x2, g, w_bf)


_TQ = 512
_SUB = _TQ // BLOCK
_GROUP_ROWS = Q_PER_KV * BLOCK


def _attn_kernel(q_ref, kp_ref, kc_ref, kn_ref, vp_ref, vc_ref, vn_ref, km_ref, vm_ref,
                 sink_ref, ga_ref, w_ref, o_ref, bias_ref, attn_ref):
    i = pl.program_id(1)
    n_i = pl.num_programs(1)

    @pl.when((pl.program_id(0) == 0) & (i == 0))
    def _init_bias():
        r = lax.broadcasted_iota(jnp.int32, (BLOCK, BLOCK), 0)
        c = lax.broadcasted_iota(jnp.int32, (BLOCK, BLOCK), 1)
        d_prev = (r + BLOCK - c).astype(F32)
        d_cur = jnp.abs(r - c).astype(F32)
        d_next = (c + BLOCK - r).astype(F32)
        for h in range(N_HEADS):
            slope = 2.0 ** (-8.0 * (h + 1.0) / N_HEADS)
            rows = slice(h * BLOCK, (h + 1) * BLOCK)
            bias_ref[0, rows, :] = jnp.where(c >= r, -slope * d_prev, NEG_INF)
            bias_ref[1, rows, :] = -slope * d_cur
            bias_ref[2, rows, :] = jnp.where(c <= r, -slope * d_next, NEG_INF)

    scale = HEAD_DIM ** -0.5
    nt = (((1,), (1,)), ((), ()))
    for j in range(_SUB):
        rows = slice(j * BLOCK, (j + 1) * BLOCK)
        q = q_ref[0, rows, :]
        if j == 0:
            k_prev, v_prev, prev_ok = kp_ref[0], vp_ref[0], i > 0
        else:
            prows = slice((j - 1) * BLOCK, j * BLOCK)
            k_prev, v_prev, prev_ok = kc_ref[0, prows, :], vc_ref[0, prows, :], None
        if j == _SUB - 1:
            k_next, v_next, next_ok = kn_ref[0], vn_ref[0], i < n_i - 1
        else:
            nrows = slice((j + 1) * BLOCK, (j + 2) * BLOCK)
            k_next, v_next, next_ok = kc_ref[0, nrows, :], vc_ref[0, nrows, :], None
        k_cur, v_cur = kc_ref[0, rows, :], vc_ref[0, rows, :]
        for g in range(N_KV_HEADS):
            cols = slice(g * HEAD_DIM, (g + 1) * HEAD_DIM)
            grows = slice(g * _GROUP_ROWS, (g + 1) * _GROUP_ROWS)
            qg = jnp.concatenate(
                [q[:, (g * Q_PER_KV + h) * HEAD_DIM:(g * Q_PER_KV + h + 1) * HEAD_DIM] for h in range(Q_PER_KV)],
                axis=0)

            def scores(kk):
                return lax.dot_general(qg, kk, nt, preferred_element_type=F32) * scale

            s_p = scores(k_prev[:, cols]) + bias_ref[0, grows, :]
            if prev_ok is not None:
                s_p = jnp.where(prev_ok, s_p, NEG_INF)
            s_c = scores(k_cur[:, cols]) + bias_ref[1, grows, :]
            s_n = scores(k_next[:, cols]) + bias_ref[2, grows, :]
            if next_ok is not None:
                s_n = jnp.where(next_ok, s_n, NEG_INF)
            s_m = scores(km_ref[:, cols])
            sink = sink_ref[grows, :]
            m = jnp.maximum(
                jnp.maximum(jnp.max(s_p, axis=-1, keepdims=True), jnp.max(s_c, axis=-1, keepdims=True)),
                jnp.maximum(jnp.max(s_n, axis=-1, keepdims=True), jnp.max(s_m, axis=-1, keepdims=True)))
            m = jnp.maximum(m, sink)
            p_p = jnp.exp(s_p - m)
            p_c = jnp.exp(s_c - m)
            p_n = jnp.exp(s_n - m)
            p_m = jnp.exp(s_m - m)
            denom = (jnp.sum(p_p, axis=-1, keepdims=True) + jnp.sum(p_c, axis=-1, keepdims=True)
                     + jnp.sum(p_n, axis=-1, keepdims=True) + jnp.sum(p_m, axis=-1, keepdims=True)
                     + jnp.exp(sink - m))
            o = (jnp.dot(p_p.astype(BF16), v_prev[:, cols], preferred_element_type=F32)
                 + jnp.dot(p_c.astype(BF16), v_cur[:, cols], preferred_element_type=F32)
                 + jnp.dot(p_n.astype(BF16), v_next[:, cols], preferred_element_type=F32)
                 + jnp.dot(p_m.astype(BF16), vm_ref[:, cols], preferred_element_type=F32))
            o = (o / denom).astype(BF16)
            for h in range(Q_PER_KV):
                head = g * Q_PER_KV + h
                attn_ref[rows, head * HEAD_DIM:(head + 1) * HEAD_DIM] = o[h * BLOCK:(h + 1) * BLOCK, :]

    proj = jnp.dot(attn_ref[...], w_ref[...], preferred_element_type=F32)
    o_ref[0] = (ga_ref[0].astype(F32) * proj).astype(BF16)


def _attention(q, k, v, k_meta, v_meta, sink_rows, g_attn, w_bf):
    batch, seq, _ = q.shape
    n_blk = seq // BLOCK
    main = lambda w: pl.BlockSpec((1, _TQ, w), lambda b, i: (b, i, 0))
    prev = pl.BlockSpec((1, BLOCK, KV_WIDTH), lambda b, i: (b, jnp.maximum(i * _SUB - 1, 0), 0))
    nxt = pl.BlockSpec((1, BLOCK, KV_WIDTH), lambda b, i: (b, jnp.minimum((i + 1) * _SUB, n_blk - 1), 0))
    full = lambda a: pl.BlockSpec(a.shape, lambda b, i: (0,) * a.ndim)
    return pl.pallas_call(
        _attn_kernel,
        grid=(batch, seq // _TQ),
        in_specs=[main(ATTN_WIDTH), prev, main(KV_WIDTH), nxt, prev, main(KV_WIDTH), nxt,
                  full(k_meta), full(v_meta), full(sink_rows), main(D_MODEL), full(w_bf)],
        out_specs=main(D_MODEL),
        out_shape=jax.ShapeDtypeStruct((batch, seq, D_MODEL), BF16),
        scratch_shapes=[pltpu.VMEM((3, N_HEADS * BLOCK, BLOCK), F32),
                        pltpu.VMEM((_TQ, ATTN_WIDTH), BF16)],
        compiler_params=_params(2),
        name="attention",
    )(q, k, k, k, v, v, v, k_meta, v_meta, sink_rows, g_attn, w_bf)


_TC = 512
_HALO = SUBLANES


def _lru_gates(ext, n_rows, cw_ref, cb_ref, wg_ref, bg_ref, lam_ref, a_ref, u_ref):
    xc = cb_ref[...] + sum(
        cw_ref[t:t + 1, :] * ext[_HALO - CONV_WIDTH // 2 + t:_HALO - CONV_WIDTH // 2 + t + n_rows, :]
        for t in range(CONV_WIDTH))
    lam = lam_ref[...]
    neg_c_softplus = -LRU_C * (jnp.maximum(-lam, 0.0) + jnp.log(1.0 + jnp.exp(-jnp.abs(lam))))
    xc_bf = xc.astype(BF16)
    for n in range(LRU_BLOCKS):
        cols = slice(n * LRU_BLOCK_DIM, (n + 1) * LRU_BLOCK_DIM)
        pre = jnp.dot(xc_bf[:, cols], wg_ref[n], preferred_element_type=F32)
        r = _sigmoid(pre[:, :LRU_BLOCK_DIM] + bg_ref[0:1, cols])
        gate_i = _sigmoid(pre[:, LRU_BLOCK_DIM:] + bg_ref[1:2, cols])
        a = jnp.exp(r * neg_c_softplus[:, cols])
        a_ref[0:n_rows, cols] = a
        u_ref[0:n_rows, cols] = jnp.sqrt(1.0 - a * a) * (gate_i * xc[:, cols])


def _lru_scan(n_rows, reverse, carry, a_ref, u_ref, write):
    n_groups = n_rows // SUBLANES
    sub = lax.broadcasted_iota(jnp.int32, (SUBLANES, LRU_WIDTH), 0)

    def body(gi, carry):
        g = (n_groups - 1 - gi) if reverse else gi
        r0 = pl.multiple_of(g * SUBLANES, SUBLANES)
        a = a_ref[pl.ds(r0, SUBLANES), :]
        u = u_ref[pl.ds(r0, SUBLANES), :]
        for d in (1, 2, 4):
            if reverse:
                a_s = pltpu.roll(a, SUBLANES - d, axis=0)
                u_s = pltpu.roll(u, SUBLANES - d, axis=0)
                ok = sub < SUBLANES - d
            else:
                a_s = pltpu.roll(a, d, axis=0)
                u_s = pltpu.roll(u, d, axis=0)
                ok = sub >= d
            u = u + a * jnp.where(ok, u_s, 0.0)
            a = a * jnp.where(ok, a_s, 1.0)
        h = a * carry + u
        write(r0, h)
        edge = 0 if reverse else SUBLANES - 1
        return jnp.broadcast_to(h[edge:edge + 1, :], (SUBLANES, LRU_WIDTH))

    return lax.fori_loop(0, n_groups, body, carry)


def _lru_kernel(reverse, xr_ref, xp_ref, xn_ref, xm_ref, cw_ref, cb_ref, wg_ref, bg_ref, lam_ref,
                h_ref, carry_ref, a_ref, u_ref):
    step = pl.program_id(1)
    n_steps = pl.num_programs(1)
    t = (n_steps - 1 - step) if reverse else step
    args = (cw_ref, cb_ref, wg_ref, bg_ref, lam_ref, a_ref, u_ref)

    if reverse:
        @pl.when(step == 0)
        def _zero_state():
            carry_ref[...] = jnp.zeros_like(carry_ref)
    else:
        @pl.when(step == 0)
        def _meta_state():
            ext = jnp.concatenate(
                [jnp.zeros((_HALO, LRU_WIDTH), F32), xm_ref[...], xr_ref[0, 0:_HALO, :]], axis=0)
            _lru_gates(ext, N_META, *args)
            carry_ref[...] = _lru_scan(N_META, False, jnp.zeros((SUBLANES, LRU_WIDTH), F32), a_ref, u_ref,
                                       lambda r0, h: None)

    before = jnp.where(t == 0, xm_ref[N_META - _HALO:N_META, :], xp_ref[0])
    after = jnp.where(t == n_steps - 1, 0.0, xn_ref[0])
    ext = jnp.concatenate([before, xr_ref[0], after], axis=0)
    _lru_gates(ext, _TC, *args)

    def write(r0, h):
        h_ref[0, pl.ds(r0, SUBLANES), :] = h

    carry_ref[...] = _lru_scan(_TC, reverse, carry_ref[...], a_ref, u_ref, write)


def _lru(xr, xr_meta, conv_w, conv_b, wg_bf, bg, lam, reverse):
    batch, seq, _ = xr.shape
    n_steps = seq // _TC
    n_halo = seq // _HALO
    per_tile = _TC // _HALO
    tile = (lambda s: n_steps - 1 - s) if reverse else (lambda s: s)
    main = pl.BlockSpec((1, _TC, LRU_WIDTH), lambda b, s: (b, tile(s), 0))
    before = pl.BlockSpec((1, _HALO, LRU_WIDTH), lambda b, s: (b, jnp.maximum(tile(s) * per_tile - 1, 0), 0))
    after = pl.BlockSpec((1, _HALO, LRU_WIDTH),
                         lambda b, s: (b, jnp.minimum((tile(s) + 1) * per_tile, n_halo - 1), 0))
    full = lambda a: pl.BlockSpec(a.shape, lambda b, s: (0,) * a.ndim)
    return pl.pallas_call(
        functools.partial(_lru_kernel, reverse),
        grid=(batch, n_steps),
        in_specs=[main, before, after, full(xr_meta), full(conv_w), full(conv_b), full(wg_bf), full(bg),
                  full(lam)],
        out_specs=main,
        out_shape=jax.ShapeDtypeStruct((batch, seq, LRU_WIDTH), F32),
        scratch_shapes=[pltpu.VMEM((SUBLANES, LRU_WIDTH), F32),
                        pltpu.VMEM((_TC, LRU_WIDTH), F32),
                        pltpu.VMEM((_TC, LRU_WIDTH), F32)],
        compiler_params=_params(2),
        name="lru_bwd" if reverse else "lru_fwd",
    )(xr, xr, xr, xr_meta, conv_w, conv_b, wg_bf, bg, lam)


_ROUTER_LANES = LANES


def _split_dot(a, b_hi, b_lo):
    a_hi = a.astype(BF16)
    a_lo = (a - a_hi.astype(F32)).astype(BF16)
    return (jnp.dot(a_hi, b_hi, preferred_element_type=F32)
            + (jnp.dot(a_lo, b_hi, preferred_element_type=F32) + jnp.dot(a_hi, b_lo, preferred_element_type=F32)))


def _merge_kernel(x_ref, hf_ref, hb_ref, gy_ref, ga_ref, gr_ref, wrec_ref, wout_ref, g_ref,
                  wr_hi_ref, wr_lo_ref, br_ref, h1_ref, n2_ref, gates_ref):
    rec_in = ((hf_ref[...] + hb_ref[...]) * gy_ref[...].astype(F32)).astype(BF16)
    rec = jnp.dot(rec_in, wrec_ref[...], preferred_element_type=F32)
    mix = (ga_ref[...].astype(F32) + gr_ref[...].astype(F32) * rec).astype(BF16)
    h1 = x_ref[...] + jnp.dot(mix, wout_ref[...], preferred_element_type=F32)
    h1_ref[...] = h1
    n2 = _rms_norm(h1, g_ref[...])
    n2_ref[...] = n2.astype(BF16)

    logits = _split_dot(n2, wr_hi_ref[...], wr_lo_ref[...]) + br_ref[...]
    lane_i = lax.broadcasted_iota(jnp.int32, logits.shape, 1)
    lane = lane_i.astype(F32)
    first = lambda mask: jnp.min(jnp.where(mask, lane, float(_ROUTER_LANES)), axis=-1, keepdims=True)
    lg = jnp.where(lane < N_GROUPS, logits, -jnp.inf)
    g_max = jnp.max(lg, axis=-1, keepdims=True)
    g_top_p = 1.0 / jnp.sum(jnp.exp(lg - g_max), axis=-1, keepdims=True)
    g_idx = first(lg == g_max)
    e = lane_i - N_GROUPS
    e_group = jnp.right_shift(e, int(math.log2(EXPERTS_PER_GROUP))).astype(F32)
    in_group = (e >= 0) & (e < N_EXPERTS) & (e_group == g_idx)
    le = jnp.where(in_group, logits, -jnp.inf)
    m1 = jnp.max(le, axis=-1, keepdims=True)
    i1 = first(le == m1)
    le2 = jnp.where(lane == i1, -jnp.inf, le)
    m2 = jnp.max(le2, axis=-1, keepdims=True)
    i2 = first(le2 == m2)
    e2 = jnp.exp(m2 - m1)
    w1 = g_top_p / (1.0 + e2)
    w2 = g_top_p * e2 / (1.0 + e2)
    gates_ref[...] = jnp.where(lane == i1, w1, 0.0) + jnp.where(lane == i2, w2, 0.0)


def _merge(x2, hf, hb, gy, ga, gr, wrec_bf, wout_bf, g, wr_hi, wr_lo, br, tm):
    n_rows = x2.shape[0]
    row = lambda w: pl.BlockSpec((tm, w), lambda i: (i, 0))
    full = lambda a: pl.BlockSpec(a.shape, lambda i: (0,) * a.ndim)
    return pl.pallas_call(
        _merge_kernel,
        grid=(n_rows // tm,),
        in_specs=[row(D_MODEL)] * 6 + [full(wrec_bf), full(wout_bf), full(g), full(wr_hi), full(wr_lo), full(br)],
        out_specs=(row(D_MODEL), row(D_MODEL), row(_ROUTER_LANES)),
        out_shape=(jax.ShapeDtypeStruct((n_rows, D_MODEL), F32),
                   jax.ShapeDtypeStruct((n_rows, D_MODEL), BF16),
                   jax.ShapeDtypeStruct((n_rows, _ROUTER_LANES), F32)),
        compiler_params=_params(1),
        name="merge",
    )(x2, hf, hb, gy, ga, gr, wrec_bf, wout_bf, g, wr_hi, wr_lo, br)


def _moe_kernel(n2_ref, gates_ref, h1_ref, wg_ref, wu_ref, wd_ref, g_ref, o_ref, acc_ref):
    e = pl.program_id(1)

    @pl.when(e == 0)
    def _zero():
        acc_ref[...] = jnp.zeros_like(acc_ref)

    n2 = n2_ref[...]
    gate = jnp.dot(n2, wg_ref[0], preferred_element_type=F32)
    up = jnp.dot(n2, wu_ref[0], preferred_element_type=F32)
    hidden = (gate * _sigmoid(gate) * up).astype(BF16)
    lane = lax.broadcasted_iota(jnp.int32, gates_ref.shape, 1)
    weight = jnp.sum(jnp.where(lane == e + N_GROUPS, gates_ref[...], 0.0), axis=-1, keepdims=True)
    acc_ref[...] += weight * jnp.dot(hidden, wd_ref[0], preferred_element_type=F32)

    @pl.when(e == N_EXPERTS - 1)
    def _finish():
        o_ref[...] = _rms_norm(h1_ref[...] + acc_ref[...], g_ref[...])


def _moe(n2, gates, h1, wg_bf, wu_bf, wd_bf, g, tm):
    n_rows = n2.shape[0]
    row = lambda w: pl.BlockSpec((tm, w), lambda i, e: (i, 0))
    expert = lambda a: pl.BlockSpec((1,) + a.shape[1:], lambda i, e: (e, 0, 0))
    return pl.pallas_call(
        _moe_kernel,
        grid=(n_rows // tm, N_EXPERTS),
        in_specs=[row(D_MODEL), row(_ROUTER_LANES), row(D_MODEL), expert(wg_bf), expert(wu_bf), expert(wd_bf),
                  pl.BlockSpec(g.shape, lambda i, e: (0, 0))],
        out_specs=row(D_MODEL),
        out_shape=jax.ShapeDtypeStruct((n_rows, D_MODEL), F32),
        scratch_shapes=[pltpu.VMEM((tm, D_MODEL), F32)],
        compiler_params=_params(2),
        name="moe",
    )(n2, gates, h1, wg_bf, wu_bf, wd_bf, g)


def kernel(x, meta_tokens, norm_mix_g, w_in, conv_w, conv_b, lru_w_a, lru_b_a, lru_w_x, lru_b_x, lru_lambda, attn_sink, w_attn_branch, w_rec_branch, w_out, norm_ffn_g, w_group, b_group, w_router, b_router, moe_w_gate, moe_w_up, moe_w_down, final_norm_g):
    batch, seq, _ = x.shape
    assert norm_mix_g.shape[0] == 1, "single-layer block"
    assert seq % _TQ == 0 and seq % _TC == 0
    n_rows = batch * seq
    x2 = x.reshape(n_rows, D_MODEL)
    row = lambda a: a.reshape(1, -1).astype(F32)

    w_in_bf = w_in[0].astype(BF16)
    g_mix = row(norm_mix_g[0])
    q, k, v, xr, gy, ga, gr = _in_proj(x2, g_mix, w_in_bf, 512)
    _, k_meta, v_meta, xr_meta, _, _, _ = _in_proj(meta_tokens.astype(F32), g_mix, w_in_bf, N_META)

    sink_rows = jnp.repeat(attn_sink[0].astype(F32), BLOCK).reshape(N_HEADS * BLOCK, 1)
    shape3 = lambda a: a.reshape(batch, seq, a.shape[-1])
    attn = _attention(shape3(q), shape3(k), shape3(v), k_meta, v_meta, sink_rows, shape3(ga),
                      w_attn_branch[0].astype(BF16))

    h_dirs = []
    for d, reverse in enumerate((False, True)):
        wg = jnp.concatenate([lru_w_a[0, d], lru_w_x[0, d]], axis=-1).astype(BF16)
        bg = jnp.stack([lru_b_a[0, d], lru_b_x[0, d]]).astype(F32)
        h_dirs.append(_lru(shape3(xr), xr_meta, conv_w[0].astype(F32), row(conv_b[0]), wg, bg,
                           row(lru_lambda[0, d]), reverse))

    w_route = jnp.concatenate([w_group[0], w_router[0]], axis=1).astype(F32)
    w_route = jnp.pad(w_route, ((0, 0), (0, _ROUTER_LANES - w_route.shape[1])))
    wr_hi = w_route.astype(BF16)
    wr_lo = (w_route - wr_hi.astype(F32)).astype(BF16)
    b_route = jnp.pad(jnp.concatenate([b_group[0], b_router[0]]).astype(F32),
                      (0, _ROUTER_LANES - N_GROUPS - N_EXPERTS)).reshape(1, _ROUTER_LANES)
    h1, n2, gates = _merge(x2, h_dirs[0].reshape(n_rows, LRU_WIDTH), h_dirs[1].reshape(n_rows, LRU_WIDTH),
                           gy, attn.reshape(n_rows, D_MODEL), gr,
                           w_rec_branch[0].astype(BF16), w_out[0].astype(BF16), row(norm_ffn_g[0]),
                           wr_hi, wr_lo, b_route, 512)

    out = _moe(n2, gates, h1, moe_w_gate[0].astype(BF16), moe_w_up[0].astype(BF16), moe_w_down[0].astype(BF16),
               row(final_norm_g), 1024)
    return out.reshape(batch, seq, D_MODEL)
```

```python
import functools
import math

import jax
import jax.numpy as jnp
from jax import lax
from jax.experimental import pallas as pl
from jax.experimental.pallas import tpu as pltpu

D_MODEL = 1024
N_META = 16
N_HEADS = 8
N_KV_HEADS = 2
HEAD_DIM = 128
Q_PER_KV = N_HEADS // N_KV_HEADS
ATTN_WIDTH = N_HEADS * HEAD_DIM
KV_WIDTH = N_KV_HEADS * HEAD_DIM
WINDOW = 128
BLOCK = 128
LRU_WIDTH = D_MODEL
LRU_BLOCKS = 8
LRU_BLOCK_DIM = LRU_WIDTH // LRU_BLOCKS
CONV_WIDTH = 4
LRU_C = 8.0
N_GROUPS = 4
EXPERTS_PER_GROUP = 4
N_EXPERTS = N_GROUPS * EXPERTS_PER_GROUP
EXPERT_FF = 512
IN_WIDTH = ATTN_WIDTH + 2 * KV_WIDTH + 2 * LRU_WIDTH + 2 * D_MODEL
EPS = 1e-6
NEG_INF = -1e30

LANES = 128
SUBLANES = 8
VMEM_LIMIT = 56 * 1024 * 1024

BF16 = jnp.bfloat16
F32 = jnp.float32


def _params(n_grid_dims):
    return pltpu.CompilerParams(
        dimension_semantics=("arbitrary",) * n_grid_dims,
        vmem_limit_bytes=VMEM_LIMIT,
    )


def _sigmoid(x):
    return 0.5 * jnp.tanh(0.5 * x) + 0.5


def _gelu_tanh(x):
    c = math.sqrt(2.0 / math.pi)
    return 0.5 * x * (1.0 + jnp.tanh(c * (x + 0.044715 * (x * x * x))))


def _rms_norm(xf, g):
    ms = jnp.mean(xf * xf, axis=-1, keepdims=True)
    return xf * lax.rsqrt(ms + EPS) * g


_IN_CHUNK = 512


def _in_proj_kernel(x_ref, g_ref, w_ref, q_ref, k_ref, v_ref, xr_ref, gy_ref, ga_ref, gr_ref):
    n = _rms_norm(x_ref[...], g_ref[...]).astype(BF16)

    def proj(c0, width):
        return jnp.dot(n, w_ref[:, c0:c0 + width], preferred_element_type=F32)

    c = 0
    for j in range(ATTN_WIDTH // _IN_CHUNK):
        q_ref[:, j * _IN_CHUNK:(j + 1) * _IN_CHUNK] = proj(c, _IN_CHUNK).astype(BF16)
        c += _IN_CHUNK
    kv = proj(c, 2 * KV_WIDTH)
    k_ref[...] = kv[:, :KV_WIDTH].astype(BF16)
    v_ref[...] = kv[:, KV_WIDTH:].astype(BF16)
    c += 2 * KV_WIDTH
    for j in range(LRU_WIDTH // _IN_CHUNK):
        xr_ref[:, j * _IN_CHUNK:(j + 1) * _IN_CHUNK] = proj(c, _IN_CHUNK)
        c += _IN_CHUNK
    for j in range(LRU_WIDTH // _IN_CHUNK):
        gy_ref[:, j * _IN_CHUNK:(j + 1) * _IN_CHUNK] = _gelu_tanh(proj(c, _IN_CHUNK)).astype(BF16)
        c += _IN_CHUNK
    for ref in (ga_ref, gr_ref):
        for j in range(D_MODEL // _IN_CHUNK):
            ref[:, j * _IN_CHUNK:(j + 1) * _IN_CHUNK] = _sigmoid(proj(c, _IN_CHUNK)).astype(BF16)
            c += _IN_CHUNK


def _in_proj(x2, g, w_bf, tm):
    n_rows = x2.shape[0]
    row = lambda w: pl.BlockSpec((tm, w), lambda i: (i, 0))
    full = lambda a: pl.BlockSpec(a.shape, lambda i: (0,) * a.ndim)
    out_shapes = (
        jax.ShapeDtypeStruct((n_rows, ATTN_WIDTH), BF16),
        jax.ShapeDtypeStruct((n_rows, KV_WIDTH), BF16),
        jax.ShapeDtypeStruct((n_rows, KV_WIDTH), BF16),
        jax.ShapeDtypeStruct((n_rows, LRU_WIDTH), F32),
        jax.ShapeDtypeStruct((n_rows, LRU_WIDTH), BF16),
        jax.ShapeDtypeStruct((n_rows, D_MODEL), BF16),
        jax.ShapeDtypeStruct((n_rows, D_MODEL), BF16),
    )
    return pl.pallas_call(
        _in_proj_kernel,
        grid=(n_rows // tm,),
        in_specs=[row(D_MODEL), full(g), full(w_bf)],
        out_specs=tuple(row(s.shape[1]) for s in out_shapes),
        out_shape=out_shapes,
        compiler_params=_params(1),
        name="in_proj",
    )(x2, g, w_bf)


_TQ = 512
_SUB = _TQ // BLOCK
_GROUP_ROWS = Q_PER_KV * BLOCK


def _attn_kernel(q_ref, kp_ref, kc_ref, kn_ref, vp_ref, vc_ref, vn_ref, km_ref, vm_ref,
                 sink_ref, ga_ref, w_ref, o_ref, bias_ref, attn_ref):
    i = pl.program_id(1)
    n_i = pl.num_programs(1)

    @pl.when((pl.program_id(0) == 0) & (i == 0))
    def _init_bias():
        r = lax.broadcasted_iota(jnp.int32, (BLOCK, BLOCK), 0)
        c = lax.broadcasted_iota(jnp.int32, (BLOCK, BLOCK), 1)
        d_prev = (r + BLOCK - c).astype(F32)
        d_cur = jnp.abs(r - c).astype(F32)
        d_next = (c + BLOCK - r).astype(F32)
        for h in range(N_HEADS):
            slope = 2.0 ** (-8.0 * (h + 1.0) / N_HEADS)
            rows = slice(h * BLOCK, (h + 1) * BLOCK)
            bias_ref[0, rows, :] = jnp.where(c >= r, -slope * d_prev, NEG_INF)
            bias_ref[1, rows, :] = -slope * d_cur
            bias_ref[2, rows, :] = jnp.where(c <= r, -slope * d_next, NEG_INF)

    scale = HEAD_DIM ** -0.5
    nt = (((1,), (1,)), ((), ()))
    for j in range(_SUB):
        rows = slice(j * BLOCK, (j + 1) * BLOCK)
        q = q_ref[0, rows, :]
        if j == 0:
            k_prev, v_prev, prev_ok = kp_ref[0], vp_ref[0], i > 0
        else:
            prows = slice((j - 1) * BLOCK, j * BLOCK)
            k_prev, v_prev, prev_ok = kc_ref[0, prows, :], vc_ref[0, prows, :], None
        if j == _SUB - 1:
            k_next, v_next, next_ok = kn_ref[0], vn_ref[0], i < n_i - 1
        else:
            nrows = slice((j + 1) * BLOCK, (j + 2) * BLOCK)
            k_next, v_next, next_ok = kc_ref[0, nrows, :], vc_ref[0, nrows, :], None
        k_cur, v_cur = kc_ref[0, rows, :], vc_ref[0, rows, :]
        for g in range(N_KV_HEADS):
            cols = slice(g * HEAD_DIM, (g + 1) * HEAD_DIM)
            grows = slice(g * _GROUP_ROWS, (g + 1) * _GROUP_ROWS)
            qg = jnp.concatenate(
                [q[:, (g * Q_PER_KV + h) * HEAD_DIM:(g * Q_PER_KV + h + 1) * HEAD_DIM] for h in range(Q_PER_KV)],
                axis=0)

            def scores(kk):
                return lax.dot_general(qg, kk, nt, preferred_element_type=F32) * scale

            s_p = scores(k_prev[:, cols]) + bias_ref[0, grows, :]
            if prev_ok is not None:
                s_p = jnp.where(prev_ok, s_p, NEG_INF)
            s_c = scores(k_cur[:, cols]) + bias_ref[1, grows, :]
            s_n = scores(k_next[:, cols]) + bias_ref[2, grows, :]
            if next_ok is not None:
                s_n = jnp.where(next_ok, s_n, NEG_INF)
            s_m = scores(km_ref[:, cols])
            sink = sink_ref[grows, :]
            m = jnp.maximum(
                jnp.maximum(jnp.max(s_p, axis=-1, keepdims=True), jnp.max(s_c, axis=-1, keepdims=True)),
                jnp.maximum(jnp.max(s_n, axis=-1, keepdims=True), jnp.max(s_m, axis=-1, keepdims=True)))
            m = jnp.maximum(m, sink)
            p_p = jnp.exp(s_p - m)
            p_c = jnp.exp(s_c - m)
            p_n = jnp.exp(s_n - m)
            p_m = jnp.exp(s_m - m)
            denom = (jnp.sum(p_p, axis=-1, keepdims=True) + jnp.sum(p_c, axis=-1, keepdims=True)
                     + jnp.sum(p_n, axis=-1, keepdims=True) + jnp.sum(p_m, axis=-1, keepdims=True)
                     + jnp.exp(sink - m))
            o = (jnp.dot(p_p.astype(BF16), v_prev[:, cols], preferred_element_type=F32)
                 + jnp.dot(p_c.astype(BF16), v_cur[:, cols], preferred_element_type=F32)
                 + jnp.dot(p_n.astype(BF16), v_next[:, cols], preferred_element_type=F32)
                 + jnp.dot(p_m.astype(BF16), vm_ref[:, cols], preferred_element_type=F32))
            o = (o / denom).astype(BF16)
            for h in range(Q_PER_KV):
                head = g * Q_PER_KV + h
                attn_ref[rows, head * HEAD_DIM:(head + 1) * HEAD_DIM] = o[h * BLOCK:(h + 1) * BLOCK, :]

    proj = jnp.dot(attn_ref[...], w_ref[...], preferred_element_type=F32)
    o_ref[0] = (ga_ref[0].astype(F32) * proj).astype(BF16)


def _attention(q, k, v, k_meta, v_meta, sink_rows, g_attn, w_bf):
    batch, seq, _ = q.shape
    n_blk = seq // BLOCK
    main = lambda w: pl.BlockSpec((1, _TQ, w), lambda b, i: (b, i, 0))
    prev = pl.BlockSpec((1, BLOCK, KV_WIDTH), lambda b, i: (b, jnp.maximum(i * _SUB - 1, 0), 0))
    nxt = pl.BlockSpec((1, BLOCK, KV_WIDTH), lambda b, i: (b, jnp.minimum((i + 1) * _SUB, n_blk - 1), 0))
    full = lambda a: pl.BlockSpec(a.shape, lambda b, i: (0,) * a.ndim)
    return pl.pallas_call(
        _attn_kernel,
        grid=(batch, seq // _TQ),
        in_specs=[main(ATTN_WIDTH), prev, main(KV_WIDTH), nxt, prev, main(KV_WIDTH), nxt,
                  full(k_meta), full(v_meta), full(sink_rows), main(D_MODEL), full(w_bf)],
        out_specs=main(D_MODEL),
        out_shape=jax.ShapeDtypeStruct((batch, seq, D_MODEL), BF16),
        scratch_shapes=[pltpu.VMEM((3, N_HEADS * BLOCK, BLOCK), F32),
                        pltpu.VMEM((_TQ, ATTN_WIDTH), BF16)],
        compiler_params=_params(2),
        name="attention",
    )(q, k, k, k, v, v, v, k_meta, v_meta, sink_rows, g_attn, w_bf)


_TC = 512
_HALO = SUBLANES


def _lru_gates(ext, n_rows, cw_ref, cb_ref, wg_ref, bg_ref, lam_ref, a_ref, u_ref):
    xc = cb_ref[...] + sum(
        cw_ref[t:t + 1, :] * ext[_HALO - CONV_WIDTH // 2 + t:_HALO - CONV_WIDTH // 2 + t + n_rows, :]
        for t in range(CONV_WIDTH))
    lam = lam_ref[...]
    neg_c_softplus = -LRU_C * (jnp.maximum(-lam, 0.0) + jnp.log(1.0 + jnp.exp(-jnp.abs(lam))))
    xc_bf = xc.astype(BF16)
    for n in range(LRU_BLOCKS):
        cols = slice(n * LRU_BLOCK_DIM, (n + 1) * LRU_BLOCK_DIM)
        pre = jnp.dot(xc_bf[:, cols], wg_ref[n], preferred_element_type=F32)
        r = _sigmoid(pre[:, :LRU_BLOCK_DIM] + bg_ref[0:1, cols])
        gate_i = _sigmoid(pre[:, LRU_BLOCK_DIM:] + bg_ref[1:2, cols])
        a = jnp.exp(r * neg_c_softplus[:, cols])
        a_ref[0:n_rows, cols] = a
        u_ref[0:n_rows, cols] = jnp.sqrt(1.0 - a * a) * (gate_i * xc[:, cols])


def _lru_scan(n_rows, reverse, carry, a_ref, u_ref, write):
    n_groups = n_rows // SUBLANES
    sub = lax.broadcasted_iota(jnp.int32, (SUBLANES, LRU_WIDTH), 0)

    def body(gi, carry):
        g = (n_groups - 1 - gi) if reverse else gi
        r0 = pl.multiple_of(g * SUBLANES, SUBLANES)
        a = a_ref[pl.ds(r0, SUBLANES), :]
        u = u_ref[pl.ds(r0, SUBLANES), :]
        for d in (1, 2, 4):
            if reverse:
                a_s = pltpu.roll(a, SUBLANES - d, axis=0)
                u_s = pltpu.roll(u, SUBLANES - d, axis=0)
                ok = sub < SUBLANES - d
            else:
                a_s = pltpu.roll(a, d, axis=0)
                u_s = pltpu.roll(u, d, axis=0)
                ok = sub >= d
            u = u + a * jnp.where(ok, u_s, 0.0)
            a = a * jnp.where(ok, a_s, 1.0)
        h = a * carry + u
        write(r0, h)
        edge = 0 if reverse else SUBLANES - 1
        return jnp.broadcast_to(h[edge:edge + 1, :], (SUBLANES, LRU_WIDTH))

    return lax.fori_loop(0, n_groups, body, carry)


def _lru_kernel(reverse, xr_ref, xp_ref, xn_ref, xm_ref, cw_ref, cb_ref, wg_ref, bg_ref, lam_ref,
                h_ref, carry_ref, a_ref, u_ref):
    step = pl.program_id(1)
    n_steps = pl.num_programs(1)
    t = (n_steps - 1 - step) if reverse else step
    args = (cw_ref, cb_ref, wg_ref, bg_ref, lam_ref, a_ref, u_ref)

    if reverse:
        @pl.when(step == 0)
        def _zero_state():
            carry_ref[...] = jnp.zeros_like(carry_ref)
    else:
        @pl.when(step == 0)
        def _meta_state():
            ext = jnp.concatenate(
                [jnp.zeros((_HALO, LRU_WIDTH), F32), xm_ref[...], xr_ref[0, 0:_HALO, :]], axis=0)
            _lru_gates(ext, N_META, *args)
            carry_ref[...] = _lru_scan(N_META, False, jnp.zeros((SUBLANES, LRU_WIDTH), F32), a_ref, u_ref,
                                       lambda r0, h: None)

    before = jnp.where(t == 0, xm_ref[N_META - _HALO:N_META, :], xp_ref[0])
    after = jnp.where(t == n_steps - 1, 0.0, xn_ref[0])
    ext = jnp.concatenate([before, xr_ref[0], after], axis=0)
    _lru_gates(ext, _TC, *args)

    def write(r0, h):
        h_ref[0, pl.ds(r0, SUBLANES), :] = h

    carry_ref[...] = _lru_scan(_TC, reverse, carry_ref[...], a_ref, u_ref, write)


def _lru(xr, xr_meta, conv_w, conv_b, wg_bf, bg, lam, reverse):
    batch, seq, _ = xr.shape
    n_steps = seq // _TC
    n_halo = seq // _HALO
    per_tile = _TC // _HALO
    tile = (lambda s: n_steps - 1 - s) if reverse else (lambda s: s)
    main = pl.BlockSpec((1, _TC, LRU_WIDTH), lambda b, s: (b, tile(s), 0))
    before = pl.BlockSpec((1, _HALO, LRU_WIDTH), lambda b, s: (b, jnp.maximum(tile(s) * per_tile - 1, 0), 0))
    after = pl.BlockSpec((1, _HALO, LRU_WIDTH),
                         lambda b, s: (b, jnp.minimum((tile(s) + 1) * per_tile, n_halo - 1), 0))
    full = lambda a: pl.BlockSpec(a.shape, lambda b, s: (0,) * a.ndim)
    return pl.pallas_call(
        functools.partial(_lru_kernel, reverse),
        grid=(batch, n_steps),
        in_specs=[main, before, after, full(xr_meta), full(conv_w), full(conv_b), full(wg_bf), full(bg),
                  full(lam)],
        out_specs=main,
        out_shape=jax.ShapeDtypeStruct((batch, seq, LRU_WIDTH), F32),
        scratch_shapes=[pltpu.VMEM((SUBLANES, LRU_WIDTH), F32),
                        pltpu.VMEM((_TC, LRU_WIDTH), F32),
                        pltpu.VMEM((_TC, LRU_WIDTH), F32)],
        compiler_params=_params(2),
        name="lru_bwd" if reverse else "lru_fwd",
    )(xr, xr, xr, xr_meta, conv_w, conv_b, wg_bf, bg, lam)


_ROUTER_LANES = LANES


def _split_dot(a, b_hi, b_lo):
    a_hi = a.astype(BF16)
    a_lo = (a - a_hi.astype(F32)).astype(BF16)
    return (jnp.dot(a_hi, b_hi, preferred_element_type=F32)
            + (jnp.dot(a_lo, b_hi, preferred_element_type=F32) + jnp.dot(a_hi, b_lo, preferred_element_type=F32)))


def _merge_kernel(x_ref, hf_ref, hb_ref, gy_ref, ga_ref, gr_ref, wrec_ref, wout_ref, g_ref,
                  wr_hi_ref, wr_lo_ref, br_ref, h1_ref, n2_ref, gates_ref):
    rec_in = ((hf_ref[...] + hb_ref[...]) * gy_ref[...].astype(F32)).astype(BF16)
    rec = jnp.dot(rec_in, wrec_ref[...], preferred_element_type=F32)
    mix = (ga_ref[...].astype(F32) + gr_ref[...].astype(F32) * rec).astype(BF16)
    h1 = x_ref[...] + jnp.dot(mix, wout_ref[...], preferred_element_type=F32)
    h1_ref[...] = h1
    n2 = _rms_norm(h1, g_ref[...])
    n2_ref[...] = n2.astype(BF16)

    logits = _split_dot(n2, wr_hi_ref[...], wr_lo_ref[...]) + br_ref[...]
    lane_i = lax.broadcasted_iota(jnp.int32, logits.shape, 1)
    lane = lane_i.astype(F32)
    first = lambda mask: jnp.min(jnp.where(mask, lane, float(_ROUTER_LANES)), axis=-1, keepdims=True)
    lg = jnp.where(lane < N_GROUPS, logits, -jnp.inf)
    g_max = jnp.max(lg, axis=-1, keepdims=True)
    g_top_p = 1.0 / jnp.sum(jnp.exp(lg - g_max), axis=-1, keepdims=True)
    g_idx = first(lg == g_max)
    e = lane_i - N_GROUPS
    e_group = jnp.right_shift(e, int(math.log2(EXPERTS_PER_GROUP))).astype(F32)
    in_group = (e >= 0) & (e < N_EXPERTS) & (e_group == g_idx)
    le = jnp.where(in_group, logits, -jnp.inf)
    m1 = jnp.max(le, axis=-1, keepdims=True)
    i1 = first(le == m1)
    le2 = jnp.where(lane == i1, -jnp.inf, le)
    m2 = jnp.max(le2, axis=-1, keepdims=True)
    i2 = first(le2 == m2)
    e2 = jnp.exp(m2 - m1)
    w1 = g_top_p / (1.0 + e2)
    w2 = g_top_p * e2 / (1.0 + e2)
    gates_ref[...] = jnp.where(lane == i1, w1, 0.0) + jnp.where(lane == i2, w2, 0.0)


def _merge(x2, hf, hb, gy, ga, gr, wrec_bf, wout_bf, g, wr_hi, wr_lo, br, tm):
    n_rows = x2.shape[0]
    row = lambda w: pl.BlockSpec((tm, w), lambda i: (i, 0))
    full = lambda a: pl.BlockSpec(a.shape, lambda i: (0,) * a.ndim)
    return pl.pallas_call(
        _merge_kernel,
        grid=(n_rows // tm,),
        in_specs=[row(D_MODEL)] * 6 + [full(wrec_bf), full(wout_bf), full(g), full(wr_hi), full(wr_lo), full(br)],
        out_specs=(row(D_MODEL), row(D_MODEL), row(_ROUTER_LANES)),
        out_shape=(jax.ShapeDtypeStruct((n_rows, D_MODEL), F32),
                   jax.ShapeDtypeStruct((n_rows, D_MODEL), BF16),
                   jax.ShapeDtypeStruct((n_rows, _ROUTER_LANES), F32)),
        compiler_params=_params(1),
        name="merge",
    )(x2, hf, hb, gy, ga, gr, wrec_bf, wout_bf, g, wr_hi, wr_lo, br)


_TT = 512
_CHUNK = 16
_TM = 512
_SLOTS = 1280
_BIG = 1.0e6


def _moe_plan(gates, n_tiles):
    i32 = jnp.int32
    sel = gates[:, N_GROUPS:N_GROUPS + N_EXPERTS] > 0.0
    cnt = jnp.sum(sel.reshape(n_tiles, _TT, N_EXPERTS), axis=1, dtype=i32)
    padc = (cnt + _CHUNK - 1) // _CHUNK * _CHUNK
    lstart = jnp.cumsum(padc, axis=1) - padc
    tot = jnp.sum(padc, axis=0)
    ntile = (tot + _TM - 1) // _TM
    tile_end = jnp.cumsum(ntile)
    base = (tile_end - ntile) * _TM
    roff = base[None, :] + jnp.cumsum(padc, axis=0) - padc
    n_active = tile_end[-1]
    max_tiles = (2 * n_tiles * _TT + n_tiles * N_EXPERTS * (_CHUNK - 1)) // _TM + N_EXPERTS
    g = jnp.minimum(jnp.arange(max_tiles, dtype=i32), n_active - 1)
    tile_expert = jnp.searchsorted(tile_end, g, side="right").astype(i32)
    lstart_vec = jnp.zeros((n_tiles, 1, _ROUTER_LANES), F32).at[:, 0, N_GROUPS:N_GROUPS + N_EXPERTS].set(
        lstart.astype(F32))
    plan = dict(
        nchunk=(padc // _CHUNK).reshape(-1), lstart=lstart.reshape(-1), roff=roff.reshape(-1),
        tile_chunks=jnp.sum(padc // _CHUNK, axis=1, dtype=i32),
        tail_start=base + tot, tail_chunks=(ntile * _TM - tot) // _CHUNK,
        tile_expert=tile_expert, tile_block=g, n_active=n_active.reshape(1), lstart_vec=lstart_vec)
    return plan, max_tiles


def _slot_positions(gates, lstart_vec):
    sel = gates > 0.0
    r = lax.broadcasted_iota(jnp.int32, (_TT, _TT), 0)
    c = lax.broadcasted_iota(jnp.int32, (_TT, _TT), 1)
    before = (c < r).astype(BF16)
    rank = jnp.dot(before, sel.astype(BF16), preferred_element_type=F32)
    return sel, rank + lstart_vec


def _dispatch_kernel(nchunk_ref, lstart_ref, roff_ref, tchunks_ref, tail_start_ref, tail_chunks_ref, n_active_ref,
                     n2_ref, gates_ref, lvec_ref, xs_ref, xloc_ref, zero_ref, sem, zsem):
    i = pl.program_id(0)
    sel, pos = _slot_positions(gates_ref[...], lvec_ref[0])
    lo = jnp.min(jnp.where(sel, pos, _BIG).T, axis=0, keepdims=True)
    hi = jnp.max(jnp.where(sel, pos, -1.0).T, axis=0, keepdims=True)
    slot = lax.broadcasted_iota(jnp.int32, (_SLOTS, _TT), 0).astype(F32)
    onehot = ((slot == lo) | (slot == hi)).astype(BF16)
    xloc_ref[...] = jnp.dot(onehot, n2_ref[...], preferred_element_type=F32).astype(BF16)

    def run_copy(src0, dst0, c):
        src = pl.multiple_of(src0 + c * _CHUNK, _CHUNK)
        dst = pl.multiple_of(dst0 + c * _CHUNK, _CHUNK)
        return pltpu.make_async_copy(xloc_ref.at[pl.ds(src, _CHUNK), :], xs_ref.at[pl.ds(dst, _CHUNK), :], sem)

    for e in range(N_EXPERTS):
        idx = i * N_EXPERTS + e
        src0, dst0 = lstart_ref[idx], roff_ref[idx]

        def start(c, carry, src0=src0, dst0=dst0):
            run_copy(src0, dst0, c).start()
            return carry

        lax.fori_loop(0, nchunk_ref[idx], start, 0)

    def wait(c, carry):
        run_copy(0, 0, 0).wait()
        return carry

    lax.fori_loop(0, tchunks_ref[i], wait, 0)

    @pl.when(i == pl.num_programs(0) - 1)
    def _zero_tails():
        zero_ref[...] = jnp.zeros_like(zero_ref)

        def tail_copy(dst0, c):
            dst = pl.multiple_of(dst0 + c * _CHUNK, _CHUNK)
            return pltpu.make_async_copy(zero_ref.at[pl.ds(0, _CHUNK), :], xs_ref.at[pl.ds(dst, _CHUNK), :], zsem)

        def tile_copy(t):
            dst = pl.multiple_of(t * _TM, _TM)
            return pltpu.make_async_copy(zero_ref, xs_ref.at[pl.ds(dst, _TM), :], zsem)

        n_tiles_total = xs_ref.shape[0] // _TM

        def tstart(t, carry):
            tile_copy(t).start()
            return carry

        def twait(t, carry):
            tile_copy(0).wait()
            return carry

        lax.fori_loop(n_active_ref[0], n_tiles_total, tstart, 0)
        lax.fori_loop(n_active_ref[0], n_tiles_total, twait, 0)

        for e in range(N_EXPERTS):
            dst0 = tail_start_ref[e]

            def zstart(c, carry, dst0=dst0):
                tail_copy(dst0, c).start()
                return carry

            def zwait(c, carry):
                tail_copy(0, 0).wait()
                return carry

            lax.fori_loop(0, tail_chunks_ref[e], zstart, 0)
            lax.fori_loop(0, tail_chunks_ref[e], zwait, 0)


def _dispatch(plan, n2, gates, n_tiles, n_sorted):
    row = lambda w: pl.BlockSpec((_TT, w), lambda i, *_: (i, 0))
    return pl.pallas_call(
        _dispatch_kernel,
        grid_spec=pltpu.PrefetchScalarGridSpec(
            num_scalar_prefetch=7,
            grid=(n_tiles,),
            in_specs=[row(D_MODEL), row(_ROUTER_LANES),
                      pl.BlockSpec((1, 1, _ROUTER_LANES), lambda i, *_: (i, 0, 0))],
            out_specs=pl.BlockSpec(memory_space=pl.ANY),
            scratch_shapes=[pltpu.VMEM((_SLOTS, D_MODEL), BF16), pltpu.VMEM((_TM, D_MODEL), BF16),
                            pltpu.SemaphoreType.DMA, pltpu.SemaphoreType.DMA],
        ),
        out_shape=jax.ShapeDtypeStruct((n_sorted, D_MODEL), BF16),
        compiler_params=_params(1),
        name="moe_dispatch",
    )(plan["nchunk"], plan["lstart"], plan["roff"], plan["tile_chunks"], plan["tail_start"], plan["tail_chunks"],
      plan["n_active"], n2, gates, plan["lstart_vec"])


def _experts_kernel(tile_expert_ref, tile_block_ref, n_active_ref, xs_ref, wg_ref, wu_ref, wd_ref, ys_ref):
    active = pl.program_id(0) < n_active_ref[0]

    @pl.when(active)
    def _ffn():
        xs = xs_ref[...]
        gate = jnp.dot(xs, wg_ref[0], preferred_element_type=F32)
        up = jnp.dot(xs, wu_ref[0], preferred_element_type=F32)
        hidden = (gate * _sigmoid(gate) * up).astype(BF16)
        ys_ref[...] = jnp.dot(hidden, wd_ref[0], preferred_element_type=F32).astype(BF16)

    @pl.when(jnp.logical_not(active))
    def _unused_tile():
        ys_ref[...] = jnp.zeros_like(ys_ref)


def _experts(plan, xs, wg_bf, wu_bf, wd_bf, max_tiles):
    rows_in = pl.BlockSpec((_TM, D_MODEL), lambda g, te, tb, na: (tb[g], 0))
    rows_out = pl.BlockSpec((_TM, D_MODEL), lambda g, te, tb, na: (g, 0))
    expert = lambda a: pl.BlockSpec((1,) + a.shape[1:], lambda g, te, tb, na: (te[g], 0, 0))
    return pl.pallas_call(
        _experts_kernel,
        grid_spec=pltpu.PrefetchScalarGridSpec(
            num_scalar_prefetch=3,
            grid=(max_tiles,),
            in_specs=[rows_in, expert(wg_bf), expert(wu_bf), expert(wd_bf)],
            out_specs=rows_out,
        ),
        out_shape=jax.ShapeDtypeStruct(xs.shape, BF16),
        compiler_params=_params(1),
        name="moe_experts",
    )(plan["tile_expert"], plan["tile_block"], plan["n_active"], xs, wg_bf, wu_bf, wd_bf)


def _combine_kernel(nchunk_ref, lstart_ref, roff_ref, tchunks_ref,
                    gates_ref, lvec_ref, h1_ref, g_ref, ys_ref, o_ref, yloc_ref, sem):
    i = pl.program_id(0)

    @pl.when(i == 0)
    def _zero_slots():
        yloc_ref[...] = jnp.zeros_like(yloc_ref)

    def run_copy(src0, dst0, c):
        src = pl.multiple_of(src0 + c * _CHUNK, _CHUNK)
        dst = pl.multiple_of(dst0 + c * _CHUNK, _CHUNK)
        return pltpu.make_async_copy(ys_ref.at[pl.ds(src, _CHUNK), :], yloc_ref.at[pl.ds(dst, _CHUNK), :], sem)

    for e in range(N_EXPERTS):
        idx = i * N_EXPERTS + e
        src0, dst0 = roff_ref[idx], lstart_ref[idx]

        def start(c, carry, src0=src0, dst0=dst0):
            run_copy(src0, dst0, c).start()
            return carry

        lax.fori_loop(0, nchunk_ref[idx], start, 0)

    gates = gates_ref[...]
    sel, pos = _slot_positions(gates, lvec_ref[0])
    pos_lo = jnp.where(sel, pos, _BIG)
    pos_hi = jnp.where(sel, pos, -1.0)
    lo = jnp.min(pos_lo, axis=-1, keepdims=True)
    hi = jnp.max(pos_hi, axis=-1, keepdims=True)
    w_lo = jnp.sum(jnp.where(pos_lo == lo, gates, 0.0), axis=-1, keepdims=True)
    w_hi = jnp.where(hi == lo, 0.0, jnp.sum(jnp.where(pos_hi == hi, gates, 0.0), axis=-1, keepdims=True))
    slot = lax.broadcasted_iota(jnp.int32, (_TT, _SLOTS), 1).astype(F32)
    weights = (jnp.where(slot == lo, w_lo, 0.0) + jnp.where(slot == hi, w_hi, 0.0)).astype(BF16)

    def wait(c, carry):
        run_copy(0, 0, 0).wait()
        return carry

    lax.fori_loop(0, tchunks_ref[i], wait, 0)
    moe = jnp.dot(weights, yloc_ref[...], preferred_element_type=F32)
    o_ref[...] = _rms_norm(h1_ref[...] + moe, g_ref[...])


def _combine(plan, gates, h1, g, ys, n_tiles):
    row = lambda w: pl.BlockSpec((_TT, w), lambda i, *_: (i, 0))
    return pl.pallas_call(
        _combine_kernel,
        grid_spec=pltpu.PrefetchScalarGridSpec(
            num_scalar_prefetch=4,
            grid=(n_tiles,),
            in_specs=[row(_ROUTER_LANES), pl.BlockSpec((1, 1, _ROUTER_LANES), lambda i, *_: (i, 0, 0)),
                      row(D_MODEL), pl.BlockSpec(g.shape, lambda i, *_: (0, 0)),
                      pl.BlockSpec(memory_space=pl.ANY)],
            out_specs=row(D_MODEL),
            scratch_shapes=[pltpu.VMEM((_SLOTS, D_MODEL), BF16), pltpu.SemaphoreType.DMA],
        ),
        out_shape=jax.ShapeDtypeStruct(h1.shape, F32),
        compiler_params=_params(1),
        name="moe_combine",
    )(plan["nchunk"], plan["lstart"], plan["roff"], plan["tile_chunks"], gates, plan["lstart_vec"], h1, g, ys)


def _moe(n2, gates, h1, wg_bf, wu_bf, wd_bf, g):
    n_rows = n2.shape[0]
    assert n_rows % _TT == 0 and _SLOTS >= 2 * _TT + N_EXPERTS * (_CHUNK - 1)
    n_tiles = n_rows // _TT
    plan, max_tiles = _moe_plan(gates, n_tiles)
    xs = _dispatch(plan, n2, gates, n_tiles, max_tiles * _TM)
    ys = _experts(plan, xs, wg_bf, wu_bf, wd_bf, max_tiles)
    return _combine(plan, gates, h1, g, ys, n_tiles)


def kernel(x, meta_tokens, norm_mix_g, w_in, conv_w, conv_b, lru_w_a, lru_b_a, lru_w_x, lru_b_x, lru_lambda, attn_sink, w_attn_branch, w_rec_branch, w_out, norm_ffn_g, w_group, b_group, w_router, b_router, moe_w_gate, moe_w_up, moe_w_down, final_norm_g):
    batch, seq, _ = x.shape
    assert norm_mix_g.shape[0] == 1, "single-layer block"
    assert seq % _TQ == 0 and seq % _TC == 0
    n_rows = batch * seq
    x2 = x.reshape(n_rows, D_MODEL)
    row = lambda a: a.reshape(1, -1).astype(F32)

    w_in_bf = w_in[0].astype(BF16)
    g_mix = row(norm_mix_g[0])
    q, k, v, xr, gy, ga, gr = _in_proj(x2, g_mix, w_in_bf, 512)
    _, k_meta, v_meta, xr_meta, _, _, _ = _in_proj(meta_tokens.astype(F32), g_mix, w_in_bf, N_META)

    sink_rows = jnp.repeat(attn_sink[0].astype(F32), BLOCK).reshape(N_HEADS * BLOCK, 1)
    shape3 = lambda a: a.reshape(batch, seq, a.shape[-1])
    attn = _attention(shape3(q), shape3(k), shape3(v), k_meta, v_meta, sink_rows, shape3(ga),
                      w_attn_branch[0].astype(BF16))

    h_dirs = []
    for d, reverse in enumerate((False, True)):
        wg = jnp.concatenate([lru_w_a[0, d], lru_w_x[0, d]], axis=-1).astype(BF16)
        bg = jnp.stack([lru_b_a[0, d], lru_b_x[0, d]]).astype(F32)
        h_dirs.append(_lru(shape3(xr), xr_meta, conv_w[0].astype(F32), row(conv_b[0]), wg, bg,
                           row(lru_lambda[0, d]), reverse))

    w_route = jnp.concatenate([w_group[0], w_router[0]], axis=1).astype(F32)
    w_route = jnp.pad(w_route, ((0, 0), (0, _ROUTER_LANES - w_route.shape[1])))
    wr_hi = w_route.astype(BF16)
    wr_lo = (w_route - wr_hi.astype(F32)).astype(BF16)
    b_route = jnp.pad(jnp.concatenate([b_group[0], b_router[0]]).astype(F32),
                      (0, _ROUTER_LANES - N_GROUPS - N_EXPERTS)).reshape(1, _ROUTER_LANES)
    h1, n2, gates = _merge(x2, h_dirs[0].reshape(n_rows, LRU_WIDTH), h_dirs[1].reshape(n_rows, LRU_WIDTH),
                           gy, attn.reshape(n_rows, D_MODEL), gr,
                           w_rec_branch[0].astype(BF16), w_out[0].astype(BF16), row(norm_ffn_g[0]),
                           wr_hi, wr_lo, b_route, 512)

    out = _moe(n2, gates, h1, moe_w_gate[0].astype(BF16), moe_w_up[0].astype(BF16), moe_w_down[0].astype(BF16),
               row(final_norm_g))
    return out.reshape(batch, seq, D_MODEL)
```

```python
import functools
import math

import jax
import jax.numpy as jnp
from jax import lax
from jax.experimental import pallas as pl
from jax.experimental.pallas import tpu as pltpu

D_MODEL = 1024
N_META = 16
N_HEADS = 8
N_KV_HEADS = 2
HEAD_DIM = 128
Q_PER_KV = N_HEADS // N_KV_HEADS
ATTN_WIDTH = N_HEADS * HEAD_DIM
KV_WIDTH = N_KV_HEADS * HEAD_DIM
WINDOW = 128
BLOCK = 128
LRU_WIDTH = D_MODEL
LRU_BLOCKS = 8
LRU_BLOCK_DIM = LRU_WIDTH // LRU_BLOCKS
CONV_WIDTH = 4
LRU_C = 8.0
N_GROUPS = 4
EXPERTS_PER_GROUP = 4
N_EXPERTS = N_GROUPS * EXPERTS_PER_GROUP
EXPERT_FF = 512
IN_WIDTH = ATTN_WIDTH + 2 * KV_WIDTH + 2 * LRU_WIDTH + 2 * D_MODEL
EPS = 1e-6
NEG_INF = -1e30

LANES = 128
SUBLANES = 8
VMEM_LIMIT = 56 * 1024 * 1024

BF16 = jnp.bfloat16
F32 = jnp.float32


def _params(n_grid_dims):
    return pltpu.CompilerParams(
        dimension_semantics=("arbitrary",) * n_grid_dims,
        vmem_limit_bytes=VMEM_LIMIT,
    )


def _sigmoid(x):
    return 0.5 * jnp.tanh(0.5 * x) + 0.5


def _gelu_tanh(x):
    c = math.sqrt(2.0 / math.pi)
    return 0.5 * x * (1.0 + jnp.tanh(c * (x + 0.044715 * (x * x * x))))


def _rms_norm(xf, g):
    ms = jnp.mean(xf * xf, axis=-1, keepdims=True)
    return xf * lax.rsqrt(ms + EPS) * g


_IN_CHUNK = 512


def _in_proj_kernel(x_ref, g_ref, w_ref, q_ref, k_ref, v_ref, xr_ref, gy_ref, ga_ref, gr_ref):
    n = _rms_norm(x_ref[...], g_ref[...]).astype(BF16)

    def proj(c0, width):
        return jnp.dot(n, w_ref[:, c0:c0 + width], preferred_element_type=F32)

    c = 0
    for j in range(ATTN_WIDTH // _IN_CHUNK):
        q_ref[:, j * _IN_CHUNK:(j + 1) * _IN_CHUNK] = proj(c, _IN_CHUNK).astype(BF16)
        c += _IN_CHUNK
    kv = proj(c, 2 * KV_WIDTH)
    k_ref[...] = kv[:, :KV_WIDTH].astype(BF16)
    v_ref[...] = kv[:, KV_WIDTH:].astype(BF16)
    c += 2 * KV_WIDTH
    for j in range(LRU_WIDTH // _IN_CHUNK):
        xr_ref[:, j * _IN_CHUNK:(j + 1) * _IN_CHUNK] = proj(c, _IN_CHUNK)
        c += _IN_CHUNK
    for j in range(LRU_WIDTH // _IN_CHUNK):
        gy_ref[:, j * _IN_CHUNK:(j + 1) * _IN_CHUNK] = _gelu_tanh(proj(c, _IN_CHUNK)).astype(BF16)
        c += _IN_CHUNK
    for ref in (ga_ref, gr_ref):
        for j in range(D_MODEL // _IN_CHUNK):
            ref[:, j * _IN_CHUNK:(j + 1) * _IN_CHUNK] = _sigmoid(proj(c, _IN_CHUNK)).astype(BF16)
            c += _IN_CHUNK


def _in_proj(x2, g, w_bf, tm):
    n_rows = x2.shape[0]
    row = lambda w: pl.BlockSpec((tm, w), lambda i: (i, 0))
    full = lambda a: pl.BlockSpec(a.shape, lambda i: (0,) * a.ndim)
    out_shapes = (
        jax.ShapeDtypeStruct((n_rows, ATTN_WIDTH), BF16),
        jax.ShapeDtypeStruct((n_rows, KV_WIDTH), BF16),
        jax.ShapeDtypeStruct((n_rows, KV_WIDTH), BF16),
        jax.ShapeDtypeStruct((n_rows, LRU_WIDTH), F32),
        jax.ShapeDtypeStruct((n_rows, LRU_WIDTH), BF16),
        jax.ShapeDtypeStruct((n_rows, D_MODEL), BF16),
        jax.ShapeDtypeStruct((n_rows, D_MODEL), BF16),
    )
    return pl.pallas_call(
        _in_proj_kernel,
        grid=(n_rows // tm,),
        in_specs=[row(D_MODEL), full(g), full(w_bf)],
        out_specs=tuple(row(s.shape[1]) for s in out_shapes),
        out_shape=out_shapes,
        compiler_params=_params(1),
        name="in_proj",
    )(x2, g, w_bf)


_TQ = 512
_SUB = _TQ // BLOCK
_GROUP_ROWS = Q_PER_KV * BLOCK


def _attn_kernel(q_ref, kp_ref, kc_ref, kn_ref, vp_ref, vc_ref, vn_ref, km_ref, vm_ref,
                 sink_ref, ga_ref, w_ref, o_ref, bias_ref, attn_ref):
    i = pl.program_id(1)
    n_i = pl.num_programs(1)

    @pl.when((pl.program_id(0) == 0) & (i == 0))
    def _init_bias():
        r = lax.broadcasted_iota(jnp.int32, (BLOCK, BLOCK), 0)
        c = lax.broadcasted_iota(jnp.int32, (BLOCK, BLOCK), 1)
        d_prev = (r + BLOCK - c).astype(F32)
        d_cur = jnp.abs(r - c).astype(F32)
        d_next = (c + BLOCK - r).astype(F32)
        for h in range(N_HEADS):
            slope = 2.0 ** (-8.0 * (h + 1.0) / N_HEADS)
            rows = slice(h * BLOCK, (h + 1) * BLOCK)
            bias_ref[0, rows, :] = jnp.where(c >= r, -slope * d_prev, NEG_INF)
            bias_ref[1, rows, :] = -slope * d_cur
            bias_ref[2, rows, :] = jnp.where(c <= r, -slope * d_next, NEG_INF)

    scale = HEAD_DIM ** -0.5
    nt = (((1,), (1,)), ((), ()))
    for j in range(_SUB):
        rows = slice(j * BLOCK, (j + 1) * BLOCK)
        q = q_ref[0, rows, :]
        if j == 0:
            k_prev, v_prev, prev_ok = kp_ref[0], vp_ref[0], i > 0
        else:
            prows = slice((j - 1) * BLOCK, j * BLOCK)
            k_prev, v_prev, prev_ok = kc_ref[0, prows, :], vc_ref[0, prows, :], None
        if j == _SUB - 1:
            k_next, v_next, next_ok = kn_ref[0], vn_ref[0], i < n_i - 1
        else:
            nrows = slice((j + 1) * BLOCK, (j + 2) * BLOCK)
            k_next, v_next, next_ok = kc_ref[0, nrows, :], vc_ref[0, nrows, :], None
        k_cur, v_cur = kc_ref[0, rows, :], vc_ref[0, rows, :]
        for g in range(N_KV_HEADS):
            cols = slice(g * HEAD_DIM, (g + 1) * HEAD_DIM)
            grows = slice(g * _GROUP_ROWS, (g + 1) * _GROUP_ROWS)
            qg = jnp.concatenate(
                [q[:, (g * Q_PER_KV + h) * HEAD_DIM:(g * Q_PER_KV + h + 1) * HEAD_DIM] for h in range(Q_PER_KV)],
                axis=0)

            def scores(kk):
                return lax.dot_general(qg, kk, nt, preferred_element_type=F32) * scale

            s_p = scores(k_prev[:, cols]) + bias_ref[0, grows, :]
            if prev_ok is not None:
                s_p = jnp.where(prev_ok, s_p, NEG_INF)
            s_c = scores(k_cur[:, cols]) + bias_ref[1, grows, :]
            s_n = scores(k_next[:, cols]) + bias_ref[2, grows, :]
            if next_ok is not None:
                s_n = jnp.where(next_ok, s_n, NEG_INF)
            s_m = scores(km_ref[:, cols])
            sink = sink_ref[grows, :]
            m = jnp.maximum(
                jnp.maximum(jnp.max(s_p, axis=-1, keepdims=True), jnp.max(s_c, axis=-1, keepdims=True)),
                jnp.maximum(jnp.max(s_n, axis=-1, keepdims=True), jnp.max(s_m, axis=-1, keepdims=True)))
            m = jnp.maximum(m, sink)
            p_p = jnp.exp(s_p - m)
            p_c = jnp.exp(s_c - m)
            p_n = jnp.exp(s_n - m)
            p_m = jnp.exp(s_m - m)
            denom = (jnp.sum(p_p, axis=-1, keepdims=True) + jnp.sum(p_c, axis=-1, keepdims=True)
                     + jnp.sum(p_n, axis=-1, keepdims=True) + jnp.sum(p_m, axis=-1, keepdims=True)
                     + jnp.exp(sink - m))
            o = (jnp.dot(p_p.astype(BF16), v_prev[:, cols], preferred_element_type=F32)
                 + jnp.dot(p_c.astype(BF16), v_cur[:, cols], preferred_element_type=F32)
                 + jnp.dot(p_n.astype(BF16), v_next[:, cols], preferred_element_type=F32)
                 + jnp.dot(p_m.astype(BF16), vm_ref[:, cols], preferred_element_type=F32))
            o = (o / denom).astype(BF16)
            for h in range(Q_PER_KV):
                head = g * Q_PER_KV + h
                attn_ref[rows, head * HEAD_DIM:(head + 1) * HEAD_DIM] = o[h * BLOCK:(h + 1) * BLOCK, :]

    proj = jnp.dot(attn_ref[...], w_ref[...], preferred_element_type=F32)
    o_ref[0] = (ga_ref[0].astype(F32) * proj).astype(BF16)


def _attention(q, k, v, k_meta, v_meta, sink_rows, g_attn, w_bf):
    batch, seq, _ = q.shape
    n_blk = seq // BLOCK
    main = lambda w: pl.BlockSpec((1, _TQ, w), lambda b, i: (b, i, 0))
    prev = pl.BlockSpec((1, BLOCK, KV_WIDTH), lambda b, i: (b, jnp.maximum(i * _SUB - 1, 0), 0))
    nxt = pl.BlockSpec((1, BLOCK, KV_WIDTH), lambda b, i: (b, jnp.minimum((i + 1) * _SUB, n_blk - 1), 0))
    full = lambda a: pl.BlockSpec(a.shape, lambda b, i: (0,) * a.ndim)
    return pl.pallas_call(
        _attn_kernel,
        grid=(batch, seq // _TQ),
        in_specs=[main(ATTN_WIDTH), prev, main(KV_WIDTH), nxt, prev, main(KV_WIDTH), nxt,
                  full(k_meta), full(v_meta), full(sink_rows), main(D_MODEL), full(w_bf)],
        out_specs=main(D_MODEL),
        out_shape=jax.ShapeDtypeStruct((batch, seq, D_MODEL), BF16),
        scratch_shapes=[pltpu.VMEM((3, N_HEADS * BLOCK, BLOCK), F32),
                        pltpu.VMEM((_TQ, ATTN_WIDTH), BF16)],
        compiler_params=_params(2),
        name="attention",
    )(q, k, k, k, v, v, v, k_meta, v_meta, sink_rows, g_attn, w_bf)


_TC = 512
_HALO = SUBLANES


def _lru_gates(ext, n_rows, cw_ref, cb_ref, wg_ref, bg_ref, lam_ref, a_ref, u_ref):
    xc = cb_ref[...] + sum(
        cw_ref[t:t + 1, :] * ext[_HALO - CONV_WIDTH // 2 + t:_HALO - CONV_WIDTH // 2 + t + n_rows, :]
        for t in range(CONV_WIDTH))
    lam = lam_ref[...]
    neg_c_softplus = -LRU_C * (jnp.maximum(-lam, 0.0) + jnp.log(1.0 + jnp.exp(-jnp.abs(lam))))
    xc_bf = xc.astype(BF16)
    for n in range(LRU_BLOCKS):
        cols = slice(n * LRU_BLOCK_DIM, (n + 1) * LRU_BLOCK_DIM)
        pre = jnp.dot(xc_bf[:, cols], wg_ref[n], preferred_element_type=F32)
        r = _sigmoid(pre[:, :LRU_BLOCK_DIM] + bg_ref[0:1, cols])
        gate_i = _sigmoid(pre[:, LRU_BLOCK_DIM:] + bg_ref[1:2, cols])
        a = jnp.exp(r * neg_c_softplus[:, cols])
        a_ref[0:n_rows, cols] = a
        u_ref[0:n_rows, cols] = jnp.sqrt(1.0 - a * a) * (gate_i * xc[:, cols])


def _lru_scan(n_rows, reverse, carry, a_ref, u_ref, write):
    n_groups = n_rows // SUBLANES
    sub = lax.broadcasted_iota(jnp.int32, (SUBLANES, LRU_WIDTH), 0)

    def body(gi, carry):
        g = (n_groups - 1 - gi) if reverse else gi
        r0 = pl.multiple_of(g * SUBLANES, SUBLANES)
        a = a_ref[pl.ds(r0, SUBLANES), :]
        u = u_ref[pl.ds(r0, SUBLANES), :]
        for d in (1, 2, 4):
            if reverse:
                a_s = pltpu.roll(a, SUBLANES - d, axis=0)
                u_s = pltpu.roll(u, SUBLANES - d, axis=0)
                ok = sub < SUBLANES - d
            else:
                a_s = pltpu.roll(a, d, axis=0)
                u_s = pltpu.roll(u, d, axis=0)
                ok = sub >= d
            u = u + a * jnp.where(ok, u_s, 0.0)
            a = a * jnp.where(ok, a_s, 1.0)
        h = a * carry + u
        write(r0, h)
        edge = 0 if reverse else SUBLANES - 1
        return jnp.broadcast_to(h[edge:edge + 1, :], (SUBLANES, LRU_WIDTH))

    return lax.fori_loop(0, n_groups, body, carry)


def _lru_kernel(reverse, xr_ref, xp_ref, xn_ref, xm_ref, cw_ref, cb_ref, wg_ref, bg_ref, lam_ref,
                h_ref, carry_ref, a_ref, u_ref):
    step = pl.program_id(1)
    n_steps = pl.num_programs(1)
    t = (n_steps - 1 - step) if reverse else step
    args = (cw_ref, cb_ref, wg_ref, bg_ref, lam_ref, a_ref, u_ref)

    if reverse:
        @pl.when(step == 0)
        def _zero_state():
            carry_ref[...] = jnp.zeros_like(carry_ref)
    else:
        @pl.when(step == 0)
        def _meta_state():
            ext = jnp.concatenate(
                [jnp.zeros((_HALO, LRU_WIDTH), F32), xm_ref[...], xr_ref[0, 0:_HALO, :]], axis=0)
            _lru_gates(ext, N_META, *args)
            carry_ref[...] = _lru_scan(N_META, False, jnp.zeros((SUBLANES, LRU_WIDTH), F32), a_ref, u_ref,
                                       lambda r0, h: None)

    before = jnp.where(t == 0, xm_ref[N_META - _HALO:N_META, :], xp_ref[0])
    after = jnp.where(t == n_steps - 1, 0.0, xn_ref[0])
    ext = jnp.concatenate([before, xr_ref[0], after], axis=0)
    _lru_gates(ext, _TC, *args)

    def write(r0, h):
        h_ref[0, pl.ds(r0, SUBLANES), :] = h

    carry_ref[...] = _lru_scan(_TC, reverse, carry_ref[...], a_ref, u_ref, write)


def _lru(xr, xr_meta, conv_w, conv_b, wg_bf, bg, lam, reverse):
    batch, seq, _ = xr.shape
    n_steps = seq // _TC
    n_halo = seq // _HALO
    per_tile = _TC // _HALO
    tile = (lambda s: n_steps - 1 - s) if reverse else (lambda s: s)
    main = pl.BlockSpec((1, _TC, LRU_WIDTH), lambda b, s: (b, tile(s), 0))
    before = pl.BlockSpec((1, _HALO, LRU_WIDTH), lambda b, s: (b, jnp.maximum(tile(s) * per_tile - 1, 0), 0))
    after = pl.BlockSpec((1, _HALO, LRU_WIDTH),
                         lambda b, s: (b, jnp.minimum((tile(s) + 1) * per_tile, n_halo - 1), 0))
    full = lambda a: pl.BlockSpec(a.shape, lambda b, s: (0,) * a.ndim)
    return pl.pallas_call(
        functools.partial(_lru_kernel, reverse),
        grid=(batch, n_steps),
        in_specs=[main, before, after, full(xr_meta), full(conv_w), full(conv_b), full(wg_bf), full(bg),
                  full(lam)],
        out_specs=main,
        out_shape=jax.ShapeDtypeStruct((batch, seq, LRU_WIDTH), F32),
        scratch_shapes=[pltpu.VMEM((SUBLANES, LRU_WIDTH), F32),
                        pltpu.VMEM((_TC, LRU_WIDTH), F32),
                        pltpu.VMEM((_TC, LRU_WIDTH), F32)],
        compiler_params=_params(2),
        name="lru_bwd" if reverse else "lru_fwd",
    )(xr, xr, xr, xr_meta, conv_w, conv_b, wg_bf, bg, lam)


_ROUTER_LANES = LANES


def _split_dot(a, b_hi, b_lo):
    a_hi = a.astype(BF16)
    a_lo = (a - a_hi.astype(F32)).astype(BF16)
    return (jnp.dot(a_hi, b_hi, preferred_element_type=F32)
            + (jnp.dot(a_lo, b_hi, preferred_element_type=F32) + jnp.dot(a_hi, b_lo, preferred_element_type=F32)))


def _merge_kernel(x_ref, hf_ref, hb_ref, gy_ref, ga_ref, gr_ref, wrec_ref, wout_ref, g_ref,
                  wr_hi_ref, wr_lo_ref, br_ref, h1_ref, n2_ref, gates_ref):
    rec_in = ((hf_ref[...] + hb_ref[...]) * gy_ref[...].astype(F32)).astype(BF16)
    rec = jnp.dot(rec_in, wrec_ref[...], preferred_element_type=F32)
    mix = (ga_ref[...].astype(F32) + gr_ref[...].astype(F32) * rec).astype(BF16)
    h1 = x_ref[...] + jnp.dot(mix, wout_ref[...], preferred_element_type=F32)
    h1_ref[...] = h1
    n2 = _rms_norm(h1, g_ref[...])
    n2_ref[...] = n2.astype(BF16)

    logits = _split_dot(n2, wr_hi_ref[...], wr_lo_ref[...]) + br_ref[...]
    lane_i = lax.broadcasted_iota(jnp.int32, logits.shape, 1)
    lane = lane_i.astype(F32)
    first = lambda mask: jnp.min(jnp.where(mask, lane, float(_ROUTER_LANES)), axis=-1, keepdims=True)
    lg = jnp.where(lane < N_GROUPS, logits, -jnp.inf)
    g_max = jnp.max(lg, axis=-1, keepdims=True)
    g_top_p = 1.0 / jnp.sum(jnp.exp(lg - g_max), axis=-1, keepdims=True)
    g_idx = first(lg == g_max)
    e = lane_i - N_GROUPS
    e_group = jnp.right_shift(e, int(math.log2(EXPERTS_PER_GROUP))).astype(F32)
    in_group = (e >= 0) & (e < N_EXPERTS) & (e_group == g_idx)
    le = jnp.where(in_group, logits, -jnp.inf)
    m1 = jnp.max(le, axis=-1, keepdims=True)
    i1 = first(le == m1)
    le2 = jnp.where(lane == i1, -jnp.inf, le)
    m2 = jnp.max(le2, axis=-1, keepdims=True)
    i2 = first(le2 == m2)
    e2 = jnp.exp(m2 - m1)
    w1 = g_top_p / (1.0 + e2)
    w2 = g_top_p * e2 / (1.0 + e2)
    gates_ref[...] = jnp.where(lane == i1, w1, 0.0) + jnp.where(lane == i2, w2, 0.0)


def _merge(x2, hf, hb, gy, ga, gr, wrec_bf, wout_bf, g, wr_hi, wr_lo, br, tm):
    n_rows = x2.shape[0]
    row = lambda w: pl.BlockSpec((tm, w), lambda i: (i, 0))
    full = lambda a: pl.BlockSpec(a.shape, lambda i: (0,) * a.ndim)
    return pl.pallas_call(
        _merge_kernel,
        grid=(n_rows // tm,),
        in_specs=[row(D_MODEL)] * 6 + [full(wrec_bf), full(wout_bf), full(g), full(wr_hi), full(wr_lo), full(br)],
        out_specs=(row(D_MODEL), row(D_MODEL), row(_ROUTER_LANES)),
        out_shape=(jax.ShapeDtypeStruct((n_rows, D_MODEL), F32),
                   jax.ShapeDtypeStruct((n_rows, D_MODEL), BF16),
                   jax.ShapeDtypeStruct((n_rows, _ROUTER_LANES), F32)),
        compiler_params=_params(1),
        name="merge",
    )(x2, hf, hb, gy, ga, gr, wrec_bf, wout_bf, g, wr_hi, wr_lo, br)


_TT = 512
_CHUNK = 16
_TM = 512
_SLOTS = 1280
_BIG = 1.0e6


def _moe_plan(gates, n_tiles):
    i32 = jnp.int32
    sel = gates[:, N_GROUPS:N_GROUPS + N_EXPERTS] > 0.0
    cnt = jnp.sum(sel.reshape(n_tiles, _TT, N_EXPERTS), axis=1, dtype=i32)
    padc = (cnt + _CHUNK - 1) // _CHUNK * _CHUNK
    lstart = jnp.cumsum(padc, axis=1) - padc
    tot = jnp.sum(padc, axis=0)
    ntile = (tot + _TM - 1) // _TM
    tile_end = jnp.cumsum(ntile)
    base = (tile_end - ntile) * _TM
    roff = base[None, :] + jnp.cumsum(padc, axis=0) - padc
    n_active = tile_end[-1]
    max_tiles = (2 * n_tiles * _TT + n_tiles * N_EXPERTS * (_CHUNK - 1)) // _TM + N_EXPERTS
    g = jnp.minimum(jnp.arange(max_tiles, dtype=i32), n_active - 1)
    tile_expert = jnp.sum(g[:, None] >= tile_end[None, :], axis=1, dtype=i32)
    lstart_vec = jnp.zeros((n_tiles, 1, _ROUTER_LANES), F32).at[:, 0, N_GROUPS:N_GROUPS + N_EXPERTS].set(
        lstart.astype(F32))
    plan = dict(
        nchunk=(padc // _CHUNK).reshape(-1), lstart=lstart.reshape(-1), roff=roff.reshape(-1),
        tile_chunks=jnp.sum(padc // _CHUNK, axis=1, dtype=i32),
        tail_start=base + tot, tail_chunks=(ntile * _TM - tot) // _CHUNK,
        tile_expert=tile_expert, tile_block=g, n_active=n_active.reshape(1), lstart_vec=lstart_vec)
    return plan, max_tiles


def _slot_positions(gates, lstart_vec):
    sel = gates > 0.0
    r = lax.broadcasted_iota(jnp.int32, (_TT, _TT), 0)
    c = lax.broadcasted_iota(jnp.int32, (_TT, _TT), 1)
    before = (c < r).astype(BF16)
    rank = jnp.dot(before, sel.astype(BF16), preferred_element_type=F32)
    return sel, rank + lstart_vec


def _dispatch_kernel(nchunk_ref, lstart_ref, roff_ref, tchunks_ref, tail_start_ref, tail_chunks_ref, n_active_ref,
                     n2_ref, gates_ref, lvec_ref, xs_ref, xloc_ref, zero_ref, sem, zsem):
    i = pl.program_id(0)
    last = pl.num_programs(0) - 1
    buf = i % 2

    def run_copy(b, src0, dst0, c):
        src = pl.multiple_of(src0 + c * _CHUNK, _CHUNK)
        dst = pl.multiple_of(dst0 + c * _CHUNK, _CHUNK)
        return pltpu.make_async_copy(xloc_ref.at[b, pl.ds(src, _CHUNK), :], xs_ref.at[pl.ds(dst, _CHUNK), :],
                                     sem.at[b])

    def wait_step(b, step):
        def wait(c, carry):
            run_copy(b, 0, 0, 0).wait()
            return carry

        lax.fori_loop(0, tchunks_ref[step], wait, 0)

    @pl.when(i >= 2)
    def _buffer_free():
        wait_step(buf, i - 2)

    sel, pos = _slot_positions(gates_ref[...], lvec_ref[0])
    lo = jnp.min(jnp.where(sel, pos, _BIG).T, axis=0, keepdims=True)
    hi = jnp.max(jnp.where(sel, pos, -1.0).T, axis=0, keepdims=True)
    slot = lax.broadcasted_iota(jnp.int32, (_SLOTS, _TT), 0).astype(F32)
    onehot = ((slot == lo) | (slot == hi)).astype(BF16)
    xloc_ref[buf] = jnp.dot(onehot, n2_ref[...], preferred_element_type=F32).astype(BF16)

    for e in range(N_EXPERTS):
        idx = i * N_EXPERTS + e
        src0, dst0 = lstart_ref[idx], roff_ref[idx]

        def start(c, carry, src0=src0, dst0=dst0):
            run_copy(buf, src0, dst0, c).start()
            return carry

        lax.fori_loop(0, nchunk_ref[idx], start, 0)

    @pl.when(i == last)
    def _drain():
        @pl.when(i >= 1)
        def _previous():
            wait_step(1 - buf, i - 1)

        wait_step(buf, i)

    @pl.when(i == last)
    def _zero_tails():
        zero_ref[...] = jnp.zeros_like(zero_ref)

        def tail_copy(dst0, c):
            dst = pl.multiple_of(dst0 + c * _CHUNK, _CHUNK)
            return pltpu.make_async_copy(zero_ref.at[pl.ds(0, _CHUNK), :], xs_ref.at[pl.ds(dst, _CHUNK), :], zsem)

        def tile_copy(t):
            dst = pl.multiple_of(t * _TM, _TM)
            return pltpu.make_async_copy(zero_ref, xs_ref.at[pl.ds(dst, _TM), :], zsem)

        n_tiles_total = xs_ref.shape[0] // _TM

        def tstart(t, carry):
            tile_copy(t).start()
            return carry

        def twait(t, carry):
            tile_copy(0).wait()
            return carry

        lax.fori_loop(n_active_ref[0], n_tiles_total, tstart, 0)
        lax.fori_loop(n_active_ref[0], n_tiles_total, twait, 0)

        for e in range(N_EXPERTS):
            dst0 = tail_start_ref[e]

            def zstart(c, carry, dst0=dst0):
                tail_copy(dst0, c).start()
                return carry

            def zwait(c, carry):
                tail_copy(0, 0).wait()
                return carry

            lax.fori_loop(0, tail_chunks_ref[e], zstart, 0)
            lax.fori_loop(0, tail_chunks_ref[e], zwait, 0)


def _dispatch(plan, n2, gates, n_tiles, n_sorted):
    row = lambda w: pl.BlockSpec((_TT, w), lambda i, *_: (i, 0))
    return pl.pallas_call(
        _dispatch_kernel,
        grid_spec=pltpu.PrefetchScalarGridSpec(
            num_scalar_prefetch=7,
            grid=(n_tiles,),
            in_specs=[row(D_MODEL), row(_ROUTER_LANES),
                      pl.BlockSpec((1, 1, _ROUTER_LANES), lambda i, *_: (i, 0, 0))],
            out_specs=pl.BlockSpec(memory_space=pl.ANY),
            scratch_shapes=[pltpu.VMEM((2, _SLOTS, D_MODEL), BF16), pltpu.VMEM((_TM, D_MODEL), BF16),
                            pltpu.SemaphoreType.DMA((2,)), pltpu.SemaphoreType.DMA],
        ),
        out_shape=jax.ShapeDtypeStruct((n_sorted, D_MODEL), BF16),
        compiler_params=_params(1),
        name="moe_dispatch",
    )(plan["nchunk"], plan["lstart"], plan["roff"], plan["tile_chunks"], plan["tail_start"], plan["tail_chunks"],
      plan["n_active"], n2, gates, plan["lstart_vec"])


def _experts_kernel(tile_expert_ref, tile_block_ref, n_active_ref, xs_ref, wg_ref, wu_ref, wd_ref, ys_ref):
    active = pl.program_id(0) < n_active_ref[0]

    @pl.when(active)
    def _ffn():
        xs = xs_ref[...]
        gate = jnp.dot(xs, wg_ref[0], preferred_element_type=F32)
        up = jnp.dot(xs, wu_ref[0], preferred_element_type=F32)
        hidden = (gate * _sigmoid(gate) * up).astype(BF16)
        ys_ref[...] = jnp.dot(hidden, wd_ref[0], preferred_element_type=F32).astype(BF16)

    @pl.when(jnp.logical_not(active))
    def _unused_tile():
        ys_ref[...] = jnp.zeros_like(ys_ref)


def _experts(plan, xs, wg_bf, wu_bf, wd_bf, max_tiles):
    rows_in = pl.BlockSpec((_TM, D_MODEL), lambda g, te, tb, na: (tb[g], 0))
    rows_out = pl.BlockSpec((_TM, D_MODEL), lambda g, te, tb, na: (g, 0))
    expert = lambda a: pl.BlockSpec((1,) + a.shape[1:], lambda g, te, tb, na: (te[g], 0, 0))
    return pl.pallas_call(
        _experts_kernel,
        grid_spec=pltpu.PrefetchScalarGridSpec(
            num_scalar_prefetch=3,
            grid=(max_tiles,),
            in_specs=[rows_in, expert(wg_bf), expert(wu_bf), expert(wd_bf)],
            out_specs=rows_out,
        ),
        out_shape=jax.ShapeDtypeStruct(xs.shape, BF16),
        compiler_params=_params(1),
        name="moe_experts",
    )(plan["tile_expert"], plan["tile_block"], plan["n_active"], xs, wg_bf, wu_bf, wd_bf)


def _combine_kernel(nchunk_ref, lstart_ref, roff_ref, tchunks_ref,
                    gates_ref, lvec_ref, h1_ref, g_ref, ys_ref, o_ref, yloc_ref, sem):
    i = pl.program_id(0)
    buf = i % 2

    def run_copy(b, src0, dst0, c):
        src = pl.multiple_of(src0 + c * _CHUNK, _CHUNK)
        dst = pl.multiple_of(dst0 + c * _CHUNK, _CHUNK)
        return pltpu.make_async_copy(ys_ref.at[pl.ds(src, _CHUNK), :], yloc_ref.at[b, pl.ds(dst, _CHUNK), :],
                                     sem.at[b])

    def fetch(b, step):
        for e in range(N_EXPERTS):
            idx = step * N_EXPERTS + e
            src0, dst0 = roff_ref[idx], lstart_ref[idx]

            def start(c, carry, src0=src0, dst0=dst0):
                run_copy(b, src0, dst0, c).start()
                return carry

            lax.fori_loop(0, nchunk_ref[idx], start, 0)

    @pl.when(i == 0)
    def _first():
        yloc_ref[...] = jnp.zeros_like(yloc_ref)
        fetch(buf, i)

    @pl.when(i + 1 < pl.num_programs(0))
    def _prefetch():
        fetch(1 - buf, i + 1)

    gates = gates_ref[...]
    sel, pos = _slot_positions(gates, lvec_ref[0])
    pos_lo = jnp.where(sel, pos, _BIG)
    pos_hi = jnp.where(sel, pos, -1.0)
    lo = jnp.min(pos_lo, axis=-1, keepdims=True)
    hi = jnp.max(pos_hi, axis=-1, keepdims=True)
    w_lo = jnp.sum(jnp.where(pos_lo == lo, gates, 0.0), axis=-1, keepdims=True)
    w_hi = jnp.where(hi == lo, 0.0, jnp.sum(jnp.where(pos_hi == hi, gates, 0.0), axis=-1, keepdims=True))
    slot = lax.broadcasted_iota(jnp.int32, (_TT, _SLOTS), 1).astype(F32)
    weights = (jnp.where(slot == lo, w_lo, 0.0) + jnp.where(slot == hi, w_hi, 0.0)).astype(BF16)

    def wait(c, carry):
        run_copy(buf, 0, 0, 0).wait()
        return carry

    lax.fori_loop(0, tchunks_ref[i], wait, 0)
    moe = jnp.dot(weights, yloc_ref[buf], preferred_element_type=F32)
    o_ref[...] = _rms_norm(h1_ref[...] + moe, g_ref[...])


def _combine(plan, gates, h1, g, ys, n_tiles):
    row = lambda w: pl.BlockSpec((_TT, w), lambda i, *_: (i, 0))
    return pl.pallas_call(
        _combine_kernel,
        grid_spec=pltpu.PrefetchScalarGridSpec(
            num_scalar_prefetch=4,
            grid=(n_tiles,),
            in_specs=[row(_ROUTER_LANES), pl.BlockSpec((1, 1, _ROUTER_LANES), lambda i, *_: (i, 0, 0)),
                      row(D_MODEL), pl.BlockSpec(g.shape, lambda i, *_: (0, 0)),
                      pl.BlockSpec(memory_space=pl.ANY)],
            out_specs=row(D_MODEL),
            scratch_shapes=[pltpu.VMEM((2, _SLOTS, D_MODEL), BF16), pltpu.SemaphoreType.DMA((2,))],
        ),
        out_shape=jax.ShapeDtypeStruct(h1.shape, F32),
        compiler_params=_params(1),
        name="moe_combine",
    )(plan["nchunk"], plan["lstart"], plan["roff"], plan["tile_chunks"], gates, plan["lstart_vec"], h1, g, ys)


def _moe(n2, gates, h1, wg_bf, wu_bf, wd_bf, g):
    n_rows = n2.shape[0]
    assert n_rows % _TT == 0 and _SLOTS >= 2 * _TT + N_EXPERTS * (_CHUNK - 1)
    n_tiles = n_rows // _TT
    plan, max_tiles = _moe_plan(gates, n_tiles)
    xs = _dispatch(plan, n2, gates, n_tiles, max_tiles * _TM)
    ys = _experts(plan, xs, wg_bf, wu_bf, wd_bf, max_tiles)
    return _combine(plan, gates, h1, g, ys, n_tiles)


def kernel(x, meta_tokens, norm_mix_g, w_in, conv_w, conv_b, lru_w_a, lru_b_a, lru_w_x, lru_b_x, lru_lambda, attn_sink, w_attn_branch, w_rec_branch, w_out, norm_ffn_g, w_group, b_group, w_router, b_router, moe_w_gate, moe_w_up, moe_w_down, final_norm_g):
    batch, seq, _ = x.shape
    assert norm_mix_g.shape[0] == 1, "single-layer block"
    assert seq % _TQ == 0 and seq % _TC == 0
    n_rows = batch * seq
    x2 = x.reshape(n_rows, D_MODEL)
    row = lambda a: a.reshape(1, -1).astype(F32)

    w_in_bf = w_in[0].astype(BF16)
    g_mix = row(norm_mix_g[0])
    q, k, v, xr, gy, ga, gr = _in_proj(x2, g_mix, w_in_bf, 512)
    _, k_meta, v_meta, xr_meta, _, _, _ = _in_proj(meta_tokens.astype(F32), g_mix, w_in_bf, N_META)

    sink_rows = jnp.repeat(attn_sink[0].astype(F32), BLOCK).reshape(N_HEADS * BLOCK, 1)
    shape3 = lambda a: a.reshape(batch, seq, a.shape[-1])
    attn = _attention(shape3(q), shape3(k), shape3(v), k_meta, v_meta, sink_rows, shape3(ga),
                      w_attn_branch[0].astype(BF16))

    h_dirs = []
    for d, reverse in enumerate((False, True)):
        wg = jnp.concatenate([lru_w_a[0, d], lru_w_x[0, d]], axis=-1).astype(BF16)
        bg = jnp.stack([lru_b_a[0, d], lru_b_x[0, d]]).astype(F32)
        h_dirs.append(_lru(shape3(xr), xr_meta, conv_w[0].astype(F32), row(conv_b[0]), wg, bg,
                           row(lru_lambda[0, d]), reverse))

    w_route = jnp.concatenate([w_group[0], w_router[0]], axis=1).astype(F32)
    w_route = jnp.pad(w_route, ((0, 0), (0, _ROUTER_LANES - w_route.shape[1])))
    wr_hi = w_route.astype(BF16)
    wr_lo = (w_route - wr_hi.astype(F32)).astype(BF16)
    b_route = jnp.pad(jnp.concatenate([b_group[0], b_router[0]]).astype(F32),
                      (0, _ROUTER_LANES - N_GROUPS - N_EXPERTS)).reshape(1, _ROUTER_LANES)
    h1, n2, gates = _merge(x2, h_dirs[0].reshape(n_rows, LRU_WIDTH), h_dirs[1].reshape(n_rows, LRU_WIDTH),
                           gy, attn.reshape(n_rows, D_MODEL), gr,
                           w_rec_branch[0].astype(BF16), w_out[0].astype(BF16), row(norm_ffn_g[0]),
                           wr_hi, wr_lo, b_route, 512)

    out = _moe(n2, gates, h1, moe_w_gate[0].astype(BF16), moe_w_up[0].astype(BF16), moe_w_down[0].astype(BF16),
               row(final_norm_g))
    return out.reshape(batch, seq, D_MODEL)
```

```python
import functools
import math

import jax
import jax.numpy as jnp
from jax import lax
from jax.experimental import pallas as pl
from jax.experimental.pallas import tpu as pltpu

D_MODEL = 1024
N_META = 16
N_HEADS = 8
N_KV_HEADS = 2
HEAD_DIM = 128
Q_PER_KV = N_HEADS // N_KV_HEADS
ATTN_WIDTH = N_HEADS * HEAD_DIM
KV_WIDTH = N_KV_HEADS * HEAD_DIM
WINDOW = 128
BLOCK = 128
LRU_WIDTH = D_MODEL
LRU_BLOCKS = 8
LRU_BLOCK_DIM = LRU_WIDTH // LRU_BLOCKS
CONV_WIDTH = 4
LRU_C = 8.0
N_GROUPS = 4
EXPERTS_PER_GROUP = 4
N_EXPERTS = N_GROUPS * EXPERTS_PER_GROUP
EXPERT_FF = 512
IN_WIDTH = ATTN_WIDTH + 2 * KV_WIDTH + 2 * LRU_WIDTH + 2 * D_MODEL
EPS = 1e-6
NEG_INF = -1e30

LANES = 128
SUBLANES = 8
VMEM_LIMIT = 56 * 1024 * 1024

BF16 = jnp.bfloat16
F32 = jnp.float32


def _params(n_grid_dims):
    return pltpu.CompilerParams(
        dimension_semantics=("arbitrary",) * n_grid_dims,
        vmem_limit_bytes=VMEM_LIMIT,
    )


def _sigmoid(x):
    return 0.5 * jnp.tanh(0.5 * x) + 0.5


def _gelu_tanh(x):
    c = math.sqrt(2.0 / math.pi)
    return 0.5 * x * (1.0 + jnp.tanh(c * (x + 0.044715 * (x * x * x))))


def _rms_norm(xf, g):
    ms = jnp.mean(xf * xf, axis=-1, keepdims=True)
    return xf * lax.rsqrt(ms + EPS) * g


_IN_CHUNK = 512


def _in_proj_kernel(x_ref, g_ref, w_ref, q_ref, k_ref, v_ref, xr_ref, gy_ref, ga_ref, gr_ref):
    n = _rms_norm(x_ref[...], g_ref[...]).astype(BF16)

    def proj(c0, width):
        return jnp.dot(n, w_ref[:, c0:c0 + width], preferred_element_type=F32)

    c = 0
    for j in range(ATTN_WIDTH // _IN_CHUNK):
        q_ref[:, j * _IN_CHUNK:(j + 1) * _IN_CHUNK] = proj(c, _IN_CHUNK).astype(BF16)
        c += _IN_CHUNK
    kv = proj(c, 2 * KV_WIDTH)
    k_ref[...] = kv[:, :KV_WIDTH].astype(BF16)
    v_ref[...] = kv[:, KV_WIDTH:].astype(BF16)
    c += 2 * KV_WIDTH
    for j in range(LRU_WIDTH // _IN_CHUNK):
        xr_ref[:, j * _IN_CHUNK:(j + 1) * _IN_CHUNK] = proj(c, _IN_CHUNK)
        c += _IN_CHUNK
    for j in range(LRU_WIDTH // _IN_CHUNK):
        gy_ref[:, j * _IN_CHUNK:(j + 1) * _IN_CHUNK] = _gelu_tanh(proj(c, _IN_CHUNK)).astype(BF16)
        c += _IN_CHUNK
    for ref in (ga_ref, gr_ref):
        for j in range(D_MODEL // _IN_CHUNK):
            ref[:, j * _IN_CHUNK:(j + 1) * _IN_CHUNK] = _sigmoid(proj(c, _IN_CHUNK)).astype(BF16)
            c += _IN_CHUNK


def _in_proj(x2, g, w_bf, tm):
    n_rows = x2.shape[0]
    row = lambda w: pl.BlockSpec((tm, w), lambda i: (i, 0))
    full = lambda a: pl.BlockSpec(a.shape, lambda i: (0,) * a.ndim)
    out_shapes = (
        jax.ShapeDtypeStruct((n_rows, ATTN_WIDTH), BF16),
        jax.ShapeDtypeStruct((n_rows, KV_WIDTH), BF16),
        jax.ShapeDtypeStruct((n_rows, KV_WIDTH), BF16),
        jax.ShapeDtypeStruct((n_rows, LRU_WIDTH), F32),
        jax.ShapeDtypeStruct((n_rows, LRU_WIDTH), BF16),
        jax.ShapeDtypeStruct((n_rows, D_MODEL), BF16),
        jax.ShapeDtypeStruct((n_rows, D_MODEL), BF16),
    )
    return pl.pallas_call(
        _in_proj_kernel,
        grid=(n_rows // tm,),
        in_specs=[row(D_MODEL), full(g), full(w_bf)],
        out_specs=tuple(row(s.shape[1]) for s in out_shapes),
        out_shape=out_shapes,
        compiler_params=_params(1),
        name="in_proj",
    )(x2, g, w_bf)


_TQ = 512
_SUB = _TQ // BLOCK
_GROUP_ROWS = Q_PER_KV * BLOCK


def _attn_kernel(q_ref, kp_ref, kc_ref, kn_ref, vp_ref, vc_ref, vn_ref, km_ref, vm_ref,
                 sink_ref, ga_ref, w_ref, o_ref, bias_ref, attn_ref):
    i = pl.program_id(1)
    n_i = pl.num_programs(1)

    @pl.when((pl.program_id(0) == 0) & (i == 0))
    def _init_bias():
        r = lax.broadcasted_iota(jnp.int32, (BLOCK, BLOCK), 0)
        c = lax.broadcasted_iota(jnp.int32, (BLOCK, BLOCK), 1)
        d_prev = (r + BLOCK - c).astype(F32)
        d_cur = jnp.abs(r - c).astype(F32)
        d_next = (c + BLOCK - r).astype(F32)
        for h in range(N_HEADS):
            slope = 2.0 ** (-8.0 * (h + 1.0) / N_HEADS)
            rows = slice(h * BLOCK, (h + 1) * BLOCK)
            bias_ref[0, rows, :] = jnp.where(c >= r, -slope * d_prev, NEG_INF)
            bias_ref[1, rows, :] = -slope * d_cur
            bias_ref[2, rows, :] = jnp.where(c <= r, -slope * d_next, NEG_INF)

    scale = HEAD_DIM ** -0.5
    nt = (((1,), (1,)), ((), ()))
    for j in range(_SUB):
        rows = slice(j * BLOCK, (j + 1) * BLOCK)
        q = q_ref[0, rows, :]
        if j == 0:
            k_prev, v_prev, prev_ok = kp_ref[0], vp_ref[0], i > 0
        else:
            prows = slice((j - 1) * BLOCK, j * BLOCK)
            k_prev, v_prev, prev_ok = kc_ref[0, prows, :], vc_ref[0, prows, :], None
        if j == _SUB - 1:
            k_next, v_next, next_ok = kn_ref[0], vn_ref[0], i < n_i - 1
        else:
            nrows = slice((j + 1) * BLOCK, (j + 2) * BLOCK)
            k_next, v_next, next_ok = kc_ref[0, nrows, :], vc_ref[0, nrows, :], None
        k_cur, v_cur = kc_ref[0, rows, :], vc_ref[0, rows, :]
        for g in range(N_KV_HEADS):
            cols = slice(g * HEAD_DIM, (g + 1) * HEAD_DIM)
            grows = slice(g * _GROUP_ROWS, (g + 1) * _GROUP_ROWS)
            qg = jnp.concatenate(
                [q[:, (g * Q_PER_KV + h) * HEAD_DIM:(g * Q_PER_KV + h + 1) * HEAD_DIM] for h in range(Q_PER_KV)],
                axis=0)

            def scores(kk):
                return lax.dot_general(qg, kk, nt, preferred_element_type=F32) * scale

            s_p = scores(k_prev[:, cols]) + bias_ref[0, grows, :]
            if prev_ok is not None:
                s_p = jnp.where(prev_ok, s_p, NEG_INF)
            s_c = scores(k_cur[:, cols]) + bias_ref[1, grows, :]
            s_n = scores(k_next[:, cols]) + bias_ref[2, grows, :]
            if next_ok is not None:
                s_n = jnp.where(next_ok, s_n, NEG_INF)
            s_m = scores(km_ref[:, cols])
            sink = sink_ref[grows, :]
            m = jnp.maximum(
                jnp.maximum(jnp.max(s_p, axis=-1, keepdims=True), jnp.max(s_c, axis=-1, keepdims=True)),
                jnp.maximum(jnp.max(s_n, axis=-1, keepdims=True), jnp.max(s_m, axis=-1, keepdims=True)))
            m = jnp.maximum(m, sink)
            p_p = jnp.exp(s_p - m)
            p_c = jnp.exp(s_c - m)
            p_n = jnp.exp(s_n - m)
            p_m = jnp.exp(s_m - m)
            denom = (jnp.sum(p_p, axis=-1, keepdims=True) + jnp.sum(p_c, axis=-1, keepdims=True)
                     + jnp.sum(p_n, axis=-1, keepdims=True) + jnp.sum(p_m, axis=-1, keepdims=True)
                     + jnp.exp(sink - m))
            o = (jnp.dot(p_p.astype(BF16), v_prev[:, cols], preferred_element_type=F32)
                 + jnp.dot(p_c.astype(BF16), v_cur[:, cols], preferred_element_type=F32)
                 + jnp.dot(p_n.astype(BF16), v_next[:, cols], preferred_element_type=F32)
                 + jnp.dot(p_m.astype(BF16), vm_ref[:, cols], preferred_element_type=F32))
            o = (o / denom).astype(BF16)
            for h in range(Q_PER_KV):
                head = g * Q_PER_KV + h
                attn_ref[rows, head * HEAD_DIM:(head + 1) * HEAD_DIM] = o[h * BLOCK:(h + 1) * BLOCK, :]

    proj = jnp.dot(attn_ref[...], w_ref[...], preferred_element_type=F32)
    o_ref[0] = (ga_ref[0].astype(F32) * proj).astype(BF16)


def _attention(q, k, v, k_meta, v_meta, sink_rows, g_attn, w_bf):
    batch, seq, _ = q.shape
    n_blk = seq // BLOCK
    main = lambda w: pl.BlockSpec((1, _TQ, w), lambda b, i: (b, i, 0))
    prev = pl.BlockSpec((1, BLOCK, KV_WIDTH), lambda b, i: (b, jnp.maximum(i * _SUB - 1, 0), 0))
    nxt = pl.BlockSpec((1, BLOCK, KV_WIDTH), lambda b, i: (b, jnp.minimum((i + 1) * _SUB, n_blk - 1), 0))
    full = lambda a: pl.BlockSpec(a.shape, lambda b, i: (0,) * a.ndim)
    return pl.pallas_call(
        _attn_kernel,
        grid=(batch, seq // _TQ),
        in_specs=[main(ATTN_WIDTH), prev, main(KV_WIDTH), nxt, prev, main(KV_WIDTH), nxt,
                  full(k_meta), full(v_meta), full(sink_rows), main(D_MODEL), full(w_bf)],
        out_specs=main(D_MODEL),
        out_shape=jax.ShapeDtypeStruct((batch, seq, D_MODEL), BF16),
        scratch_shapes=[pltpu.VMEM((3, N_HEADS * BLOCK, BLOCK), F32),
                        pltpu.VMEM((_TQ, ATTN_WIDTH), BF16)],
        compiler_params=_params(2),
        name="attention",
    )(q, k, k, k, v, v, v, k_meta, v_meta, sink_rows, g_attn, w_bf)


_TC = 512
_HALO = SUBLANES


def _interleave_in(dst_ref, src, n_rows):
    seg = n_rows // SUBLANES
    for n in range(LRU_BLOCKS):
        for s in range(SUBLANES):
            dst_ref[n, pl.ds(s, seg, stride=SUBLANES), :] = src(s * seg, seg, n)


def _interleave_out(write, src_ref, n_rows):
    seg = n_rows // SUBLANES
    for n in range(LRU_BLOCKS):
        for s in range(SUBLANES):
            write(s * seg, seg, n, src_ref[n, pl.ds(s, seg, stride=SUBLANES), :])


def _lru_gates(n_rows, prev2, prev1, next0, x_ref, cw_ref, cb_ref, wg_ref, bg_ref, lam_ref, a_ref, u_ref):
    seg = n_rows // SUBLANES
    sub = lax.broadcasted_iota(jnp.int32, (SUBLANES, LRU_BLOCK_DIM), 0)
    lam = lam_ref[...]
    decay_scale = (-0.5 * LRU_C * math.log2(math.e)) * (
        jnp.maximum(-lam, 0.0) + jnp.log(1.0 + jnp.exp(-jnp.abs(lam))))
    for n in range(LRU_BLOCKS):
        cols = slice(n * LRU_BLOCK_DIM, (n + 1) * LRU_BLOCK_DIM)
        x = x_ref[n, 0:n_rows, :]
        group = lambda j: x[j * SUBLANES:(j + 1) * SUBLANES, :]
        e0 = jnp.where(sub == 0, prev2(n), pltpu.roll(group(seg - 2), 1, axis=0))
        e1 = jnp.where(sub == 0, prev1(n), pltpu.roll(group(seg - 1), 1, axis=0))
        e_next = jnp.where(sub == SUBLANES - 1, next0(n), pltpu.roll(group(0), SUBLANES - 1, axis=0))
        ext = jnp.concatenate([e0, e1, x, e_next], axis=0)
        xh = cb_ref[:, cols] + sum(
            cw_ref[t:t + 1, cols] * ext[t * SUBLANES:t * SUBLANES + n_rows, :] for t in range(CONV_WIDTH))
        pre = jnp.dot(xh.astype(BF16), wg_ref[n], preferred_element_type=F32)
        t_a = jnp.tanh(pre[:, :LRU_BLOCK_DIM] + bg_ref[0:1, cols])
        t_x = jnp.tanh(pre[:, LRU_BLOCK_DIM:] + bg_ref[1:2, cols])
        scale = decay_scale[:, cols]
        a = jnp.exp2(t_a * scale + scale)
        y = 1.0 - a * a
        a_ref[n, 0:n_rows, :] = a
        u_ref[n, 0:n_rows, :] = (y * lax.rsqrt(jnp.maximum(y, 1e-30))) * ((t_x + 1.0) * xh)


def _lru_scan(n_rows, reverse, carry_in, a_ref, u_ref, h_ref):
    seg = n_rows // SUBLANES
    unroll = min(16, seg)
    sub = lax.broadcasted_iota(jnp.int32, (SUBLANES, LRU_BLOCK_DIM), 0)
    blocks = range(LRU_BLOCKS)

    def rows(jj):
        j = (seg - 1 - jj) if reverse else jj
        return pl.ds(pl.multiple_of(j * SUBLANES, SUBLANES), SUBLANES)

    def local(jj, state):
        hs, ps = state
        r = rows(jj)
        a = [a_ref[n, r, :] for n in blocks]
        return (tuple(a[n] * hs[n] + u_ref[n, r, :] for n in blocks), tuple(a[n] * ps[n] for n in blocks))

    zeros = tuple(jnp.zeros((SUBLANES, LRU_BLOCK_DIM), F32) for _ in blocks)
    ones = tuple(jnp.ones((SUBLANES, LRU_BLOCK_DIM), F32) for _ in blocks)
    h_end, p_end = lax.fori_loop(0, seg, local, (zeros, ones), unroll=unroll)

    seg_in, carry_out = [], []
    for n in blocks:
        p, h = p_end[n], h_end[n]
        for d in (1, 2, 4):
            shift = SUBLANES - d if reverse else d
            ok = (sub < SUBLANES - d) if reverse else (sub >= d)
            h = h + p * jnp.where(ok, pltpu.roll(h, shift, axis=0), 0.0)
            p = p * jnp.where(ok, pltpu.roll(p, shift, axis=0), 1.0)
        seg_out = h + p * carry_in[n]
        first, last = (SUBLANES - 1, 0) if reverse else (0, SUBLANES - 1)
        shift = SUBLANES - 1 if reverse else 1
        seg_in.append(jnp.where(sub == first, carry_in[n], pltpu.roll(seg_out, shift, axis=0)))
        carry_out.append(seg_out[last:last + 1, :])

    if h_ref is not None:
        def final(jj, hs):
            r = rows(jj)
            new = tuple(a_ref[n, r, :] * hs[n] + u_ref[n, r, :] for n in blocks)
            for n in blocks:
                h_ref[n, r, :] = new[n]
            return new

        lax.fori_loop(0, seg, final, tuple(seg_in), unroll=unroll)
    return carry_out


def _lru_kernel(reverse, xr_ref, xp_ref, xn_ref, xm_ref, cw_ref, cb_ref, wg_ref, bg_ref, lam_ref,
                h_ref, carry_ref, x_scr, a_scr, u_scr, h_scr):
    step = pl.program_id(1)
    n_steps = pl.num_programs(1)
    t = (n_steps - 1 - step) if reverse else step
    args = (x_scr, cw_ref, cb_ref, wg_ref, bg_ref, lam_ref, a_scr, u_scr)
    lanes = lambda n: slice(n * LRU_BLOCK_DIM, (n + 1) * LRU_BLOCK_DIM)
    zero_row = lambda n: jnp.zeros((1, LRU_BLOCK_DIM), F32)

    if reverse:
        @pl.when(step == 0)
        def _zero_state():
            carry_ref[...] = jnp.zeros_like(carry_ref)
    else:
        @pl.when(step == 0)
        def _meta_state():
            _interleave_in(x_scr, lambda r0, nr, n: xm_ref[r0:r0 + nr, lanes(n)], N_META)
            _lru_gates(N_META, zero_row, zero_row, lambda n: xr_ref[0, 0:1, lanes(n)], *args)
            state = _lru_scan(N_META, False, [zero_row(n) for n in range(LRU_BLOCKS)], a_scr, u_scr, None)
            for n in range(LRU_BLOCKS):
                carry_ref[0:1, lanes(n)] = state[n]

    def before(row):
        return lambda n: jnp.where(t == 0, xm_ref[N_META - _HALO + row:N_META - _HALO + row + 1, lanes(n)],
                                   xp_ref[0, row:row + 1, lanes(n)])

    after = lambda n: jnp.where(t == n_steps - 1, 0.0, xn_ref[0, 0:1, lanes(n)])
    _interleave_in(x_scr, lambda r0, nr, n: xr_ref[0, r0:r0 + nr, lanes(n)], _TC)
    _lru_gates(_TC, before(_HALO - 2), before(_HALO - 1), after, *args)
    state = _lru_scan(_TC, reverse, [carry_ref[0:1, lanes(n)] for n in range(LRU_BLOCKS)], a_scr, u_scr, h_scr)
    for n in range(LRU_BLOCKS):
        carry_ref[0:1, lanes(n)] = state[n]

    def write(r0, nr, n, rows):
        h_ref[0, r0:r0 + nr, lanes(n)] = rows

    _interleave_out(write, h_scr, _TC)


def _lru(xr, xr_meta, conv_w, conv_b, wg_bf, bg, lam, reverse):
    batch, seq, _ = xr.shape
    n_steps = seq // _TC
    n_halo = seq // _HALO
    per_tile = _TC // _HALO
    tile = (lambda s: n_steps - 1 - s) if reverse else (lambda s: s)
    main = pl.BlockSpec((1, _TC, LRU_WIDTH), lambda b, s: (b, tile(s), 0))
    before = pl.BlockSpec((1, _HALO, LRU_WIDTH), lambda b, s: (b, jnp.maximum(tile(s) * per_tile - 1, 0), 0))
    after = pl.BlockSpec((1, _HALO, LRU_WIDTH),
                         lambda b, s: (b, jnp.minimum((tile(s) + 1) * per_tile, n_halo - 1), 0))
    full = lambda a: pl.BlockSpec(a.shape, lambda b, s: (0,) * a.ndim)
    return pl.pallas_call(
        functools.partial(_lru_kernel, reverse),
        grid=(batch, n_steps),
        in_specs=[main, before, after, full(xr_meta), full(conv_w), full(conv_b), full(wg_bf), full(bg),
                  full(lam)],
        out_specs=main,
        out_shape=jax.ShapeDtypeStruct((batch, seq, LRU_WIDTH), F32),
        scratch_shapes=[pltpu.VMEM((SUBLANES, LRU_WIDTH), F32)]
        + [pltpu.VMEM((LRU_BLOCKS, _TC, LRU_BLOCK_DIM), F32)] * 4,
        compiler_params=_params(2),
        name="lru_bwd" if reverse else "lru_fwd",
    )(xr, xr, xr, xr_meta, conv_w, conv_b, wg_bf, bg, lam)


_ROUTER_LANES = LANES


def _split_dot(a, b_hi, b_lo):
    a_hi = a.astype(BF16)
    a_lo = (a - a_hi.astype(F32)).astype(BF16)
    return (jnp.dot(a_hi, b_hi, preferred_element_type=F32)
            + (jnp.dot(a_lo, b_hi, preferred_element_type=F32) + jnp.dot(a_hi, b_lo, preferred_element_type=F32)))


def _merge_kernel(x_ref, hf_ref, hb_ref, gy_ref, ga_ref, gr_ref, wrec_ref, wout_ref, g_ref,
                  wr_hi_ref, wr_lo_ref, br_ref, h1_ref, n2_ref, gates_ref):
    rec_in = ((hf_ref[...] + hb_ref[...]) * gy_ref[...].astype(F32)).astype(BF16)
    rec = jnp.dot(rec_in, wrec_ref[...], preferred_element_type=F32)
    mix = (ga_ref[...].astype(F32) + gr_ref[...].astype(F32) * rec).astype(BF16)
    h1 = x_ref[...] + jnp.dot(mix, wout_ref[...], preferred_element_type=F32)
    h1_ref[...] = h1
    n2 = _rms_norm(h1, g_ref[...])
    n2_ref[...] = n2.astype(BF16)

    logits = _split_dot(n2, wr_hi_ref[...], wr_lo_ref[...]) + br_ref[...]
    lane_i = lax.broadcasted_iota(jnp.int32, logits.shape, 1)
    lane = lane_i.astype(F32)
    first = lambda mask: jnp.min(jnp.where(mask, lane, float(_ROUTER_LANES)), axis=-1, keepdims=True)
    lg = jnp.where(lane < N_GROUPS, logits, -jnp.inf)
    g_max = jnp.max(lg, axis=-1, keepdims=True)
    g_top_p = 1.0 / jnp.sum(jnp.exp(lg - g_max), axis=-1, keepdims=True)
    g_idx = first(lg == g_max)
    e = lane_i - N_GROUPS
    e_group = jnp.right_shift(e, int(math.log2(EXPERTS_PER_GROUP))).astype(F32)
    in_group = (e >= 0) & (e < N_EXPERTS) & (e_group == g_idx)
    le = jnp.where(in_group, logits, -jnp.inf)
    m1 = jnp.max(le, axis=-1, keepdims=True)
    i1 = first(le == m1)
    le2 = jnp.where(lane == i1, -jnp.inf, le)
    m2 = jnp.max(le2, axis=-1, keepdims=True)
    i2 = first(le2 == m2)
    e2 = jnp.exp(m2 - m1)
    w1 = g_top_p / (1.0 + e2)
    w2 = g_top_p * e2 / (1.0 + e2)
    gates_ref[...] = jnp.where(lane == i1, w1, 0.0) + jnp.where(lane == i2, w2, 0.0)


def _merge(x2, hf, hb, gy, ga, gr, wrec_bf, wout_bf, g, wr_hi, wr_lo, br, tm):
    n_rows = x2.shape[0]
    row = lambda w: pl.BlockSpec((tm, w), lambda i: (i, 0))
    full = lambda a: pl.BlockSpec(a.shape, lambda i: (0,) * a.ndim)
    return pl.pallas_call(
        _merge_kernel,
        grid=(n_rows // tm,),
        in_specs=[row(D_MODEL)] * 6 + [full(wrec_bf), full(wout_bf), full(g), full(wr_hi), full(wr_lo), full(br)],
        out_specs=(row(D_MODEL), row(D_MODEL), row(_ROUTER_LANES)),
        out_shape=(jax.ShapeDtypeStruct((n_rows, D_MODEL), F32),
                   jax.ShapeDtypeStruct((n_rows, D_MODEL), BF16),
                   jax.ShapeDtypeStruct((n_rows, _ROUTER_LANES), F32)),
        compiler_params=_params(1),
        name="merge",
    )(x2, hf, hb, gy, ga, gr, wrec_bf, wout_bf, g, wr_hi, wr_lo, br)


_TT = 512
_CHUNK = 16
_TM = 512
_SLOTS = 1280
_BIG = 1.0e6


def _moe_plan(gates, n_tiles):
    i32 = jnp.int32
    sel = gates[:, N_GROUPS:N_GROUPS + N_EXPERTS] > 0.0
    cnt = jnp.sum(sel.reshape(n_tiles, _TT, N_EXPERTS), axis=1, dtype=i32)
    padc = (cnt + _CHUNK - 1) // _CHUNK * _CHUNK
    lstart = jnp.cumsum(padc, axis=1) - padc
    tot = jnp.sum(padc, axis=0)
    ntile = (tot + _TM - 1) // _TM
    tile_end = jnp.cumsum(ntile)
    base = (tile_end - ntile) * _TM
    roff = base[None, :] + jnp.cumsum(padc, axis=0) - padc
    n_active = tile_end[-1]
    max_tiles = (2 * n_tiles * _TT + n_tiles * N_EXPERTS * (_CHUNK - 1)) // _TM + N_EXPERTS
    g = jnp.minimum(jnp.arange(max_tiles, dtype=i32), n_active - 1)
    tile_expert = jnp.sum(g[:, None] >= tile_end[None, :], axis=1, dtype=i32)
    lstart_vec = jnp.zeros((n_tiles, 1, _ROUTER_LANES), F32).at[:, 0, N_GROUPS:N_GROUPS + N_EXPERTS].set(
        lstart.astype(F32))
    plan = dict(
        nchunk=(padc // _CHUNK).reshape(-1), lstart=lstart.reshape(-1), roff=roff.reshape(-1),
        tile_chunks=jnp.sum(padc // _CHUNK, axis=1, dtype=i32),
        tail_start=base + tot, tail_chunks=(ntile * _TM - tot) // _CHUNK,
        tile_expert=tile_expert, tile_block=g, n_active=n_active.reshape(1), lstart_vec=lstart_vec)
    return plan, max_tiles


def _slot_positions(gates, lstart_vec):
    sel = gates > 0.0
    r = lax.broadcasted_iota(jnp.int32, (_TT, _TT), 0)
    c = lax.broadcasted_iota(jnp.int32, (_TT, _TT), 1)
    before = (c < r).astype(BF16)
    rank = jnp.dot(before, sel.astype(BF16), preferred_element_type=F32)
    return sel, rank + lstart_vec


def _dispatch_kernel(nchunk_ref, lstart_ref, roff_ref, tchunks_ref, tail_start_ref, tail_chunks_ref, n_active_ref,
                     n2_ref, gates_ref, lvec_ref, xs_ref, xloc_ref, zero_ref, sem, zsem):
    i = pl.program_id(0)
    last = pl.num_programs(0) - 1
    buf = i % 2

    def run_copy(b, src0, dst0, c):
        src = pl.multiple_of(src0 + c * _CHUNK, _CHUNK)
        dst = pl.multiple_of(dst0 + c * _CHUNK, _CHUNK)
        return pltpu.make_async_copy(xloc_ref.at[b, pl.ds(src, _CHUNK), :], xs_ref.at[pl.ds(dst, _CHUNK), :],
                                     sem.at[b])

    def wait_step(b, step):
        def wait(c, carry):
            run_copy(b, 0, 0, 0).wait()
            return carry

        lax.fori_loop(0, tchunks_ref[step], wait, 0)

    @pl.when(i >= 2)
    def _buffer_free():
        wait_step(buf, i - 2)

    sel, pos = _slot_positions(gates_ref[...], lvec_ref[0])
    lo = jnp.min(jnp.where(sel, pos, _BIG).T, axis=0, keepdims=True)
    hi = jnp.max(jnp.where(sel, pos, -1.0).T, axis=0, keepdims=True)
    slot = lax.broadcasted_iota(jnp.int32, (_SLOTS, _TT), 0).astype(F32)
    onehot = ((slot == lo) | (slot == hi)).astype(BF16)
    xloc_ref[buf] = jnp.dot(onehot, n2_ref[...], preferred_element_type=F32).astype(BF16)

    for e in range(N_EXPERTS):
        idx = i * N_EXPERTS + e
        src0, dst0 = lstart_ref[idx], roff_ref[idx]

        def start(c, carry, src0=src0, dst0=dst0):
            run_copy(buf, src0, dst0, c).start()
            return carry

        lax.fori_loop(0, nchunk_ref[idx], start, 0)

    @pl.when(i == last)
    def _drain():
        @pl.when(i >= 1)
        def _previous():
            wait_step(1 - buf, i - 1)

        wait_step(buf, i)

    @pl.when(i == last)
    def _zero_tails():
        zero_ref[...] = jnp.zeros_like(zero_ref)

        def tail_copy(dst0, c):
            dst = pl.multiple_of(dst0 + c * _CHUNK, _CHUNK)
            return pltpu.make_async_copy(zero_ref.at[pl.ds(0, _CHUNK), :], xs_ref.at[pl.ds(dst, _CHUNK), :], zsem)

        def tile_copy(t):
            dst = pl.multiple_of(t * _TM, _TM)
            return pltpu.make_async_copy(zero_ref, xs_ref.at[pl.ds(dst, _TM), :], zsem)

        n_tiles_total = xs_ref.shape[0] // _TM

        def tstart(t, carry):
            tile_copy(t).start()
            return carry

        def twait(t, carry):
            tile_copy(0).wait()
            return carry

        lax.fori_loop(n_active_ref[0], n_tiles_total, tstart, 0)
        lax.fori_loop(n_active_ref[0], n_tiles_total, twait, 0)

        for e in range(N_EXPERTS):
            dst0 = tail_start_ref[e]

            def zstart(c, carry, dst0=dst0):
                tail_copy(dst0, c).start()
                return carry

            def zwait(c, carry):
                tail_copy(0, 0).wait()
                return carry

            lax.fori_loop(0, tail_chunks_ref[e], zstart, 0)
            lax.fori_loop(0, tail_chunks_ref[e], zwait, 0)


def _dispatch(plan, n2, gates, n_tiles, n_sorted):
    row = lambda w: pl.BlockSpec((_TT, w), lambda i, *_: (i, 0))
    return pl.pallas_call(
        _dispatch_kernel,
        grid_spec=pltpu.PrefetchScalarGridSpec(
            num_scalar_prefetch=7,
            grid=(n_tiles,),
            in_specs=[row(D_MODEL), row(_ROUTER_LANES),
                      pl.BlockSpec((1, 1, _ROUTER_LANES), lambda i, *_: (i, 0, 0))],
            out_specs=pl.BlockSpec(memory_space=pl.ANY),
            scratch_shapes=[pltpu.VMEM((2, _SLOTS, D_MODEL), BF16), pltpu.VMEM((_TM, D_MODEL), BF16),
                            pltpu.SemaphoreType.DMA((2,)), pltpu.SemaphoreType.DMA],
        ),
        out_shape=jax.ShapeDtypeStruct((n_sorted, D_MODEL), BF16),
        compiler_params=_params(1),
        name="moe_dispatch",
    )(plan["nchunk"], plan["lstart"], plan["roff"], plan["tile_chunks"], plan["tail_start"], plan["tail_chunks"],
      plan["n_active"], n2, gates, plan["lstart_vec"])


def _experts_kernel(tile_expert_ref, tile_block_ref, n_active_ref, xs_ref, wg_ref, wu_ref, wd_ref, ys_ref):
    active = pl.program_id(0) < n_active_ref[0]

    @pl.when(active)
    def _ffn():
        xs = xs_ref[...]
        gate = jnp.dot(xs, wg_ref[0], preferred_element_type=F32)
        up = jnp.dot(xs, wu_ref[0], preferred_element_type=F32)
        hidden = (gate * _sigmoid(gate) * up).astype(BF16)
        ys_ref[...] = jnp.dot(hidden, wd_ref[0], preferred_element_type=F32).astype(BF16)

    @pl.when(jnp.logical_not(active))
    def _unused_tile():
        ys_ref[...] = jnp.zeros_like(ys_ref)


def _experts(plan, xs, wg_bf, wu_bf, wd_bf, max_tiles):
    rows_in = pl.BlockSpec((_TM, D_MODEL), lambda g, te, tb, na: (tb[g], 0))
    rows_out = pl.BlockSpec((_TM, D_MODEL), lambda g, te, tb, na: (g, 0))
    expert = lambda a: pl.BlockSpec((1,) + a.shape[1:], lambda g, te, tb, na: (te[g], 0, 0))
    return pl.pallas_call(
        _experts_kernel,
        grid_spec=pltpu.PrefetchScalarGridSpec(
            num_scalar_prefetch=3,
            grid=(max_tiles,),
            in_specs=[rows_in, expert(wg_bf), expert(wu_bf), expert(wd_bf)],
            out_specs=rows_out,
        ),
        out_shape=jax.ShapeDtypeStruct(xs.shape, BF16),
        compiler_params=_params(1),
        name="moe_experts",
    )(plan["tile_expert"], plan["tile_block"], plan["n_active"], xs, wg_bf, wu_bf, wd_bf)


def _combine_kernel(nchunk_ref, lstart_ref, roff_ref, tchunks_ref,
                    gates_ref, lvec_ref, h1_ref, g_ref, ys_ref, o_ref, yloc_ref, sem):
    i = pl.program_id(0)
    buf = i % 2

    def run_copy(b, src0, dst0, c):
        src = pl.multiple_of(src0 + c * _CHUNK, _CHUNK)
        dst = pl.multiple_of(dst0 + c * _CHUNK, _CHUNK)
        return pltpu.make_async_copy(ys_ref.at[pl.ds(src, _CHUNK), :], yloc_ref.at[b, pl.ds(dst, _CHUNK), :],
                                     sem.at[b])

    def fetch(b, step):
        for e in range(N_EXPERTS):
            idx = step * N_EXPERTS + e
            src0, dst0 = roff_ref[idx], lstart_ref[idx]

            def start(c, carry, src0=src0, dst0=dst0):
                run_copy(b, src0, dst0, c).start()
                return carry

            lax.fori_loop(0, nchunk_ref[idx], start, 0)

    @pl.when(i == 0)
    def _first():
        yloc_ref[...] = jnp.zeros_like(yloc_ref)
        fetch(buf, i)

    @pl.when(i + 1 < pl.num_programs(0))
    def _prefetch():
        fetch(1 - buf, i + 1)

    gates = gates_ref[...]
    sel, pos = _slot_positions(gates, lvec_ref[0])
    pos_lo = jnp.where(sel, pos, _BIG)
    pos_hi = jnp.where(sel, pos, -1.0)
    lo = jnp.min(pos_lo, axis=-1, keepdims=True)
    hi = jnp.max(pos_hi, axis=-1, keepdims=True)
    w_lo = jnp.sum(jnp.where(pos_lo == lo, gates, 0.0), axis=-1, keepdims=True)
    w_hi = jnp.where(hi == lo, 0.0, jnp.sum(jnp.where(pos_hi == hi, gates, 0.0), axis=-1, keepdims=True))
    slot = lax.broadcasted_iota(jnp.int32, (_TT, _SLOTS), 1).astype(F32)
    weights = (jnp.where(slot == lo, w_lo, 0.0) + jnp.where(slot == hi, w_hi, 0.0)).astype(BF16)

    def wait(c, carry):
        run_copy(buf, 0, 0, 0).wait()
        return carry

    lax.fori_loop(0, tchunks_ref[i], wait, 0)
    moe = jnp.dot(weights, yloc_ref[buf], preferred_element_type=F32)
    o_ref[...] = _rms_norm(h1_ref[...] + moe, g_ref[...])


def _combine(plan, gates, h1, g, ys, n_tiles):
    row = lambda w: pl.BlockSpec((_TT, w), lambda i, *_: (i, 0))
    return pl.pallas_call(
        _combine_kernel,
        grid_spec=pltpu.PrefetchScalarGridSpec(
            num_scalar_prefetch=4,
            grid=(n_tiles,),
            in_specs=[row(_ROUTER_LANES), pl.BlockSpec((1, 1, _ROUTER_LANES), lambda i, *_: (i, 0, 0)),
                      row(D_MODEL), pl.BlockSpec(g.shape, lambda i, *_: (0, 0)),
                      pl.BlockSpec(memory_space=pl.ANY)],
            out_specs=row(D_MODEL),
            scratch_shapes=[pltpu.VMEM((2, _SLOTS, D_MODEL), BF16), pltpu.SemaphoreType.DMA((2,))],
        ),
        out_shape=jax.ShapeDtypeStruct(h1.shape, F32),
        compiler_params=_params(1),
        name="moe_combine",
    )(plan["nchunk"], plan["lstart"], plan["roff"], plan["tile_chunks"], gates, plan["lstart_vec"], h1, g, ys)


def _moe(n2, gates, h1, wg_bf, wu_bf, wd_bf, g):
    n_rows = n2.shape[0]
    assert n_rows % _TT == 0 and _SLOTS >= 2 * _TT + N_EXPERTS * (_CHUNK - 1)
    n_tiles = n_rows // _TT
    plan, max_tiles = _moe_plan(gates, n_tiles)
    xs = _dispatch(plan, n2, gates, n_tiles, max_tiles * _TM)
    ys = _experts(plan, xs, wg_bf, wu_bf, wd_bf, max_tiles)
    return _combine(plan, gates, h1, g, ys, n_tiles)


def kernel(x, meta_tokens, norm_mix_g, w_in, conv_w, conv_b, lru_w_a, lru_b_a, lru_w_x, lru_b_x, lru_lambda, attn_sink, w_attn_branch, w_rec_branch, w_out, norm_ffn_g, w_group, b_group, w_router, b_router, moe_w_gate, moe_w_up, moe_w_down, final_norm_g):
    batch, seq, _ = x.shape
    assert norm_mix_g.shape[0] == 1, "single-layer block"
    assert seq % _TQ == 0 and seq % _TC == 0
    n_rows = batch * seq
    x2 = x.reshape(n_rows, D_MODEL)
    row = lambda a: a.reshape(1, -1).astype(F32)

    w_in_bf = w_in[0].astype(BF16)
    g_mix = row(norm_mix_g[0])
    q, k, v, xr, gy, ga, gr = _in_proj(x2, g_mix, w_in_bf, 512)
    _, k_meta, v_meta, xr_meta, _, _, _ = _in_proj(meta_tokens.astype(F32), g_mix, w_in_bf, N_META)

    sink_rows = jnp.repeat(attn_sink[0].astype(F32), BLOCK).reshape(N_HEADS * BLOCK, 1)
    shape3 = lambda a: a.reshape(batch, seq, a.shape[-1])
    attn = _attention(shape3(q), shape3(k), shape3(v), k_meta, v_meta, sink_rows, shape3(ga),
                      w_attn_branch[0].astype(BF16))

    h_dirs = []
    for d, reverse in enumerate((False, True)):
        wg = jnp.concatenate([lru_w_a[0, d], lru_w_x[0, d]], axis=-1).astype(BF16)
        bg_half = 0.5 * jnp.stack([lru_b_a[0, d], lru_b_x[0, d]]).astype(F32)
        h_dirs.append(_lru(shape3(xr), xr_meta, 0.5 * conv_w[0].astype(F32), 0.5 * row(conv_b[0]), wg, bg_half,
                           row(lru_lambda[0, d]), reverse))

    w_route = jnp.concatenate([w_group[0], w_router[0]], axis=1).astype(F32)
    w_route = jnp.pad(w_route, ((0, 0), (0, _ROUTER_LANES - w_route.shape[1])))
    wr_hi = w_route.astype(BF16)
    wr_lo = (w_route - wr_hi.astype(F32)).astype(BF16)
    b_route = jnp.pad(jnp.concatenate([b_group[0], b_router[0]]).astype(F32),
                      (0, _ROUTER_LANES - N_GROUPS - N_EXPERTS)).reshape(1, _ROUTER_LANES)
    h1, n2, gates = _merge(x2, h_dirs[0].reshape(n_rows, LRU_WIDTH), h_dirs[1].reshape(n_rows, LRU_WIDTH),
                           gy, attn.reshape(n_rows, D_MODEL), gr,
                           w_rec_branch[0].astype(BF16), w_out[0].astype(BF16), row(norm_ffn_g[0]),
                           wr_hi, wr_lo, b_route, 512)

    out = _moe(n2, gates, h1, moe_w_gate[0].astype(BF16), moe_w_up[0].astype(BF16), moe_w_down[0].astype(BF16),
               row(final_norm_g))
    return out.reshape(batch, seq, D_MODEL)
```

```python
import functools
import math

import jax
import jax.numpy as jnp
from jax import lax
from jax.experimental import pallas as pl
from jax.experimental.pallas import tpu as pltpu

D_MODEL = 1024
N_META = 16
N_HEADS = 8
N_KV_HEADS = 2
HEAD_DIM = 128
Q_PER_KV = N_HEADS // N_KV_HEADS
ATTN_WIDTH = N_HEADS * HEAD_DIM
KV_WIDTH = N_KV_HEADS * HEAD_DIM
WINDOW = 128
BLOCK = 128
LRU_WIDTH = D_MODEL
LRU_BLOCKS = 8
LRU_BLOCK_DIM = LRU_WIDTH // LRU_BLOCKS
CONV_WIDTH = 4
LRU_C = 8.0
N_GROUPS = 4
EXPERTS_PER_GROUP = 4
N_EXPERTS = N_GROUPS * EXPERTS_PER_GROUP
EXPERT_FF = 512
IN_WIDTH = ATTN_WIDTH + 2 * KV_WIDTH + 2 * LRU_WIDTH + 2 * D_MODEL
EPS = 1e-6
NEG_INF = -1e30

LANES = 128
SUBLANES = 8
VMEM_LIMIT = 56 * 1024 * 1024

BF16 = jnp.bfloat16
F32 = jnp.float32


def _params(n_grid_dims):
    return pltpu.CompilerParams(
        dimension_semantics=("arbitrary",) * n_grid_dims,
        vmem_limit_bytes=VMEM_LIMIT,
    )


def _sigmoid(x):
    return 0.5 * jnp.tanh(0.5 * x) + 0.5


def _gelu_tanh(x):
    c = math.sqrt(2.0 / math.pi)
    return 0.5 * x * (1.0 + jnp.tanh(c * (x + 0.044715 * (x * x * x))))


def _rms_norm(xf, g):
    ms = jnp.mean(xf * xf, axis=-1, keepdims=True)
    return xf * lax.rsqrt(ms + EPS) * g


_IN_CHUNK = 512


def _in_proj_kernel(x_ref, g_ref, w_ref, q_ref, k_ref, v_ref, xr_ref, gy_ref, ga_ref, gr_ref):
    n = _rms_norm(x_ref[...], g_ref[...]).astype(BF16)

    def proj(c0, width):
        return jnp.dot(n, w_ref[:, c0:c0 + width], preferred_element_type=F32)

    c = 0
    for j in range(ATTN_WIDTH // _IN_CHUNK):
        q_ref[:, j * _IN_CHUNK:(j + 1) * _IN_CHUNK] = proj(c, _IN_CHUNK).astype(BF16)
        c += _IN_CHUNK
    kv = proj(c, 2 * KV_WIDTH)
    k_ref[...] = kv[:, :KV_WIDTH].astype(BF16)
    v_ref[...] = kv[:, KV_WIDTH:].astype(BF16)
    c += 2 * KV_WIDTH
    for j in range(LRU_WIDTH // _IN_CHUNK):
        xr_ref[:, j * _IN_CHUNK:(j + 1) * _IN_CHUNK] = proj(c, _IN_CHUNK)
        c += _IN_CHUNK
    for j in range(LRU_WIDTH // _IN_CHUNK):
        gy_ref[:, j * _IN_CHUNK:(j + 1) * _IN_CHUNK] = _gelu_tanh(proj(c, _IN_CHUNK)).astype(BF16)
        c += _IN_CHUNK
    for ref in (ga_ref, gr_ref):
        for j in range(D_MODEL // _IN_CHUNK):
            ref[:, j * _IN_CHUNK:(j + 1) * _IN_CHUNK] = _sigmoid(proj(c, _IN_CHUNK)).astype(BF16)
            c += _IN_CHUNK


def _in_proj(x2, g, w_bf, tm):
    n_rows = x2.shape[0]
    row = lambda w: pl.BlockSpec((tm, w), lambda i: (i, 0))
    full = lambda a: pl.BlockSpec(a.shape, lambda i: (0,) * a.ndim)
    out_shapes = (
        jax.ShapeDtypeStruct((n_rows, ATTN_WIDTH), BF16),
        jax.ShapeDtypeStruct((n_rows, KV_WIDTH), BF16),
        jax.ShapeDtypeStruct((n_rows, KV_WIDTH), BF16),
        jax.ShapeDtypeStruct((n_rows, LRU_WIDTH), F32),
        jax.ShapeDtypeStruct((n_rows, LRU_WIDTH), BF16),
        jax.ShapeDtypeStruct((n_rows, D_MODEL), BF16),
        jax.ShapeDtypeStruct((n_rows, D_MODEL), BF16),
    )
    return pl.pallas_call(
        _in_proj_kernel,
        grid=(n_rows // tm,),
        in_specs=[row(D_MODEL), full(g), full(w_bf)],
        out_specs=tuple(row(s.shape[1]) for s in out_shapes),
        out_shape=out_shapes,
        compiler_params=_params(1),
        name="in_proj",
    )(x2, g, w_bf)


_TQ = 512
_SUB = _TQ // BLOCK
_GROUP_ROWS = Q_PER_KV * BLOCK


_KEYS = 4 * BLOCK
_SM_ROWS = 32


def _attn_kernel(q_ref, kp_ref, kc_ref, kn_ref, vp_ref, vc_ref, vn_ref, km_ref, vm_ref,
                 sink_ref, ga_ref, w_ref, o_ref, bias_ref, attn_ref, s_ref, p_ref, m_ref):
    i = pl.program_id(1)
    n_i = pl.num_programs(1)
    scale = HEAD_DIM ** -0.5
    exp_scale = scale * math.log2(math.e)

    @pl.when((pl.program_id(0) == 0) & (i == 0))
    def _init_bias():
        r = lax.broadcasted_iota(jnp.int32, (BLOCK, BLOCK), 0)
        c = lax.broadcasted_iota(jnp.int32, (BLOCK, BLOCK), 1)
        d_prev = (r + BLOCK - c).astype(F32)
        d_cur = jnp.abs(r - c).astype(F32)
        d_next = (c + BLOCK - r).astype(F32)
        for h in range(N_HEADS):
            slope = 2.0 ** (-8.0 * (h + 1.0) / N_HEADS) / scale
            rows = slice(h * BLOCK, (h + 1) * BLOCK)
            bias_ref[rows, 0:BLOCK] = jnp.where(c >= r, -slope * d_prev, NEG_INF / scale)
            bias_ref[rows, BLOCK:2 * BLOCK] = -slope * d_cur
            bias_ref[rows, 2 * BLOCK:3 * BLOCK] = jnp.where(c <= r, -slope * d_next, NEG_INF / scale)
            bias_ref[rows, 3 * BLOCK:4 * BLOCK] = jnp.where(c < N_META, 0.0, NEG_INF / scale)

    nt = (((1,), (1,)), ((), ()))
    for j in range(_SUB):
        rows = slice(j * BLOCK, (j + 1) * BLOCK)
        q = q_ref[0, rows, :]
        if j == 0:
            k3 = [kp_ref[0], kc_ref[0, 0:2 * BLOCK, :]]
            v3 = [vp_ref[0], vc_ref[0, 0:2 * BLOCK, :]]
        elif j == _SUB - 1:
            k3 = [kc_ref[0, (j - 1) * BLOCK:(j + 1) * BLOCK, :], kn_ref[0]]
            v3 = [vc_ref[0, (j - 1) * BLOCK:(j + 1) * BLOCK, :], vn_ref[0]]
        else:
            k3 = [kc_ref[0, (j - 1) * BLOCK:(j + 2) * BLOCK, :]]
            v3 = [vc_ref[0, (j - 1) * BLOCK:(j + 2) * BLOCK, :]]
        k_cat = jnp.concatenate(k3 + [km_ref[...]], axis=0)
        v_cat = jnp.concatenate(v3 + [vm_ref[...]], axis=0)
        masked = []
        if j == 0:
            masked.append((slice(0, BLOCK), i == 0))
        if j == _SUB - 1:
            masked.append((slice(2 * BLOCK, 3 * BLOCK), i == n_i - 1))
        for g in range(N_KV_HEADS):
            cols = slice(g * HEAD_DIM, (g + 1) * HEAD_DIM)
            row0 = g * _GROUP_ROWS
            qg = jnp.concatenate(
                [q[:, (g * Q_PER_KV + h) * HEAD_DIM:(g * Q_PER_KV + h + 1) * HEAD_DIM] for h in range(Q_PER_KV)],
                axis=0)
            pair = j * N_KV_HEADS + g
            s_ref[pair] = lax.dot_general(qg, k_cat[:, cols], nt, preferred_element_type=F32)
            for mask_cols, mask_on in masked:
                s_ref[pair, :, mask_cols] = jnp.where(mask_on, NEG_INF / scale, s_ref[pair, :, mask_cols])

            chunks = [(slice(c * _SM_ROWS, (c + 1) * _SM_ROWS), slice(row0 + c * _SM_ROWS, row0 + (c + 1) * _SM_ROWS))
                      for c in range(_GROUP_ROWS // _SM_ROWS)]
            wide = lambda col: jnp.broadcast_to(col, (_SM_ROWS, BLOCK))
            tiled = lambda stat: jnp.concatenate([stat] * (_KEYS // BLOCK), axis=1)
            for r, rb in chunks:
                z = s_ref[pair, r, :] + bias_ref[rb, :]
                m_ref[pair, r, :] = jnp.maximum(wide(jnp.max(z, axis=-1, keepdims=True)), sink_ref[rb, :])
            for r, rb in chunks:
                m = m_ref[pair, r, :]
                p = jnp.exp2((s_ref[pair, r, :] + bias_ref[rb, :] - tiled(m)) * exp_scale)
                denom = wide(jnp.sum(p, axis=-1, keepdims=True)) + jnp.exp2((sink_ref[rb, :] - m) * exp_scale)
                p_ref[pair, r, :] = p.astype(BF16)
                m_ref[pair, r, :] = 1.0 / denom
            o = jnp.dot(p_ref[pair], v_cat[:, cols], preferred_element_type=F32)
            o = (o * m_ref[pair]).astype(BF16)
            for h in range(Q_PER_KV):
                head = g * Q_PER_KV + h
                attn_ref[rows, head * HEAD_DIM:(head + 1) * HEAD_DIM] = o[h * BLOCK:(h + 1) * BLOCK, :]

    proj = jnp.dot(attn_ref[...], w_ref[...], preferred_element_type=F32)
    o_ref[0] = (ga_ref[0].astype(F32) * proj).astype(BF16)


def _attention(q, k, v, k_meta, v_meta, sink_rows, g_attn, w_bf):
    batch, seq, _ = q.shape
    n_blk = seq // BLOCK
    main = lambda w: pl.BlockSpec((1, _TQ, w), lambda b, i: (b, i, 0))
    prev = pl.BlockSpec((1, BLOCK, KV_WIDTH), lambda b, i: (b, jnp.maximum(i * _SUB - 1, 0), 0))
    nxt = pl.BlockSpec((1, BLOCK, KV_WIDTH), lambda b, i: (b, jnp.minimum((i + 1) * _SUB, n_blk - 1), 0))
    full = lambda a: pl.BlockSpec(a.shape, lambda b, i: (0,) * a.ndim)
    return pl.pallas_call(
        _attn_kernel,
        grid=(batch, seq // _TQ),
        in_specs=[main(ATTN_WIDTH), prev, main(KV_WIDTH), nxt, prev, main(KV_WIDTH), nxt,
                  full(k_meta), full(v_meta), full(sink_rows), main(D_MODEL), full(w_bf)],
        out_specs=main(D_MODEL),
        out_shape=jax.ShapeDtypeStruct((batch, seq, D_MODEL), BF16),
        scratch_shapes=[pltpu.VMEM((N_HEADS * BLOCK, _KEYS), F32),
                        pltpu.VMEM((_TQ, ATTN_WIDTH), BF16),
                        pltpu.VMEM((_SUB * N_KV_HEADS, _GROUP_ROWS, _KEYS), F32),
                        pltpu.VMEM((_SUB * N_KV_HEADS, _GROUP_ROWS, _KEYS), BF16),
                        pltpu.VMEM((_SUB * N_KV_HEADS, _GROUP_ROWS, BLOCK), F32)],
        compiler_params=_params(2),
        name="attention",
    )(q, k, k, k, v, v, v, k_meta, v_meta, sink_rows, g_attn, w_bf)


_TC = 512
_HALO = SUBLANES


def _interleave_in(dst_ref, src, n_rows):
    seg = n_rows // SUBLANES
    for n in range(LRU_BLOCKS):
        for s in range(SUBLANES):
            dst_ref[n, pl.ds(s, seg, stride=SUBLANES), :] = src(s * seg, seg, n)


def _interleave_out(write, src_ref, n_rows):
    seg = n_rows // SUBLANES
    for n in range(LRU_BLOCKS):
        for s in range(SUBLANES):
            write(s * seg, seg, n, src_ref[n, pl.ds(s, seg, stride=SUBLANES), :])


def _lru_gates(n_rows, prev2, prev1, next0, x_ref, cw_ref, cb_ref, wg_ref, bg_ref, lam_ref, a_ref, u_ref):
    seg = n_rows // SUBLANES
    sub = lax.broadcasted_iota(jnp.int32, (SUBLANES, LRU_BLOCK_DIM), 0)
    lam = lam_ref[...]
    decay_scale = (-0.5 * LRU_C * math.log2(math.e)) * (
        jnp.maximum(-lam, 0.0) + jnp.log(1.0 + jnp.exp(-jnp.abs(lam))))
    for n in range(LRU_BLOCKS):
        cols = slice(n * LRU_BLOCK_DIM, (n + 1) * LRU_BLOCK_DIM)
        x = x_ref[n, 0:n_rows, :]
        group = lambda j: x[j * SUBLANES:(j + 1) * SUBLANES, :]
        e0 = jnp.where(sub == 0, prev2(n), pltpu.roll(group(seg - 2), 1, axis=0))
        e1 = jnp.where(sub == 0, prev1(n), pltpu.roll(group(seg - 1), 1, axis=0))
        e_next = jnp.where(sub == SUBLANES - 1, next0(n), pltpu.roll(group(0), SUBLANES - 1, axis=0))
        ext = jnp.concatenate([e0, e1, x, e_next], axis=0)
        xh = cb_ref[:, cols] + sum(
            cw_ref[t:t + 1, cols] * ext[t * SUBLANES:t * SUBLANES + n_rows, :] for t in range(CONV_WIDTH))
        pre = jnp.dot(xh.astype(BF16), wg_ref[n], preferred_element_type=F32)
        t_a = jnp.tanh(pre[:, :LRU_BLOCK_DIM] + bg_ref[0:1, cols])
        t_x = jnp.tanh(pre[:, LRU_BLOCK_DIM:] + bg_ref[1:2, cols])
        scale = decay_scale[:, cols]
        a = jnp.exp2(t_a * scale + scale)
        y = 1.0 - a * a
        a_ref[n, 0:n_rows, :] = a
        u_ref[n, 0:n_rows, :] = (y * lax.rsqrt(jnp.maximum(y, 1e-30))) * ((t_x + 1.0) * xh)


def _lru_scan(n_rows, reverse, carry_in, a_ref, u_ref, h_ref):
    seg = n_rows // SUBLANES
    unroll = min(16, seg)
    sub = lax.broadcasted_iota(jnp.int32, (SUBLANES, LRU_BLOCK_DIM), 0)
    blocks = range(LRU_BLOCKS)

    def rows(jj):
        j = (seg - 1 - jj) if reverse else jj
        return pl.ds(pl.multiple_of(j * SUBLANES, SUBLANES), SUBLANES)

    def local(jj, state):
        hs, ps = state
        r = rows(jj)
        a = [a_ref[n, r, :] for n in blocks]
        return (tuple(a[n] * hs[n] + u_ref[n, r, :] for n in blocks), tuple(a[n] * ps[n] for n in blocks))

    zeros = tuple(jnp.zeros((SUBLANES, LRU_BLOCK_DIM), F32) for _ in blocks)
    ones = tuple(jnp.ones((SUBLANES, LRU_BLOCK_DIM), F32) for _ in blocks)
    h_end, p_end = lax.fori_loop(0, seg, local, (zeros, ones), unroll=unroll)

    seg_in, carry_out = [], []
    for n in blocks:
        p, h = p_end[n], h_end[n]
        for d in (1, 2, 4):
            shift = SUBLANES - d if reverse else d
            ok = (sub < SUBLANES - d) if reverse else (sub >= d)
            h = h + p * jnp.where(ok, pltpu.roll(h, shift, axis=0), 0.0)
            p = p * jnp.where(ok, pltpu.roll(p, shift, axis=0), 1.0)
        seg_out = h + p * carry_in[n]
        first, last = (SUBLANES - 1, 0) if reverse else (0, SUBLANES - 1)
        shift = SUBLANES - 1 if reverse else 1
        seg_in.append(jnp.where(sub == first, carry_in[n], pltpu.roll(seg_out, shift, axis=0)))
        carry_out.append(seg_out[last:last + 1, :])

    if h_ref is not None:
        def final(jj, hs):
            r = rows(jj)
            new = tuple(a_ref[n, r, :] * hs[n] + u_ref[n, r, :] for n in blocks)
            for n in blocks:
                h_ref[n, r, :] = new[n]
            return new

        lax.fori_loop(0, seg, final, tuple(seg_in), unroll=unroll)
    return carry_out


def _lru_kernel(reverse, xr_ref, xp_ref, xn_ref, xm_ref, cw_ref, cb_ref, wg_ref, bg_ref, lam_ref,
                h_ref, carry_ref, x_scr, a_scr, u_scr, h_scr):
    step = pl.program_id(1)
    n_steps = pl.num_programs(1)
    t = (n_steps - 1 - step) if reverse else step
    args = (x_scr, cw_ref, cb_ref, wg_ref, bg_ref, lam_ref, a_scr, u_scr)
    lanes = lambda n: slice(n * LRU_BLOCK_DIM, (n + 1) * LRU_BLOCK_DIM)
    zero_row = lambda n: jnp.zeros((1, LRU_BLOCK_DIM), F32)

    if reverse:
        @pl.when(step == 0)
        def _zero_state():
            carry_ref[...] = jnp.zeros_like(carry_ref)
    else:
        @pl.when(step == 0)
        def _meta_state():
            _interleave_in(x_scr, lambda r0, nr, n: xm_ref[r0:r0 + nr, lanes(n)], N_META)
            _lru_gates(N_META, zero_row, zero_row, lambda n: xr_ref[0, 0:1, lanes(n)], *args)
            state = _lru_scan(N_META, False, [zero_row(n) for n in range(LRU_BLOCKS)], a_scr, u_scr, None)
            for n in range(LRU_BLOCKS):
                carry_ref[0:1, lanes(n)] = state[n]

    def before(row):
        return lambda n: jnp.where(t == 0, xm_ref[N_META - _HALO + row:N_META - _HALO + row + 1, lanes(n)],
                                   xp_ref[0, row:row + 1, lanes(n)])

    after = lambda n: jnp.where(t == n_steps - 1, 0.0, xn_ref[0, 0:1, lanes(n)])
    _interleave_in(x_scr, lambda r0, nr, n: xr_ref[0, r0:r0 + nr, lanes(n)], _TC)
    _lru_gates(_TC, before(_HALO - 2), before(_HALO - 1), after, *args)
    state = _lru_scan(_TC, reverse, [carry_ref[0:1, lanes(n)] for n in range(LRU_BLOCKS)], a_scr, u_scr, h_scr)
    for n in range(LRU_BLOCKS):
        carry_ref[0:1, lanes(n)] = state[n]

    def write(r0, nr, n, rows):
        h_ref[0, r0:r0 + nr, lanes(n)] = rows

    _interleave_out(write, h_scr, _TC)


def _lru(xr, xr_meta, conv_w, conv_b, wg_bf, bg, lam, reverse):
    batch, seq, _ = xr.shape
    n_steps = seq // _TC
    n_halo = seq // _HALO
    per_tile = _TC // _HALO
    tile = (lambda s: n_steps - 1 - s) if reverse else (lambda s: s)
    main = pl.BlockSpec((1, _TC, LRU_WIDTH), lambda b, s: (b, tile(s), 0))
    before = pl.BlockSpec((1, _HALO, LRU_WIDTH), lambda b, s: (b, jnp.maximum(tile(s) * per_tile - 1, 0), 0))
    after = pl.BlockSpec((1, _HALO, LRU_WIDTH),
                         lambda b, s: (b, jnp.minimum((tile(s) + 1) * per_tile, n_halo - 1), 0))
    full = lambda a: pl.BlockSpec(a.shape, lambda b, s: (0,) * a.ndim)
    return pl.pallas_call(
        functools.partial(_lru_kernel, reverse),
        grid=(batch, n_steps),
        in_specs=[main, before, after, full(xr_meta), full(conv_w), full(conv_b), full(wg_bf), full(bg),
                  full(lam)],
        out_specs=main,
        out_shape=jax.ShapeDtypeStruct((batch, seq, LRU_WIDTH), F32),
        scratch_shapes=[pltpu.VMEM((SUBLANES, LRU_WIDTH), F32)]
        + [pltpu.VMEM((LRU_BLOCKS, _TC, LRU_BLOCK_DIM), F32)] * 4,
        compiler_params=_params(2),
        name="lru_bwd" if reverse else "lru_fwd",
    )(xr, xr, xr, xr_meta, conv_w, conv_b, wg_bf, bg, lam)


_ROUTER_LANES = LANES


def _split_dot(a, b_hi, b_lo):
    a_hi = a.astype(BF16)
    a_lo = (a - a_hi.astype(F32)).astype(BF16)
    return (jnp.dot(a_hi, b_hi, preferred_element_type=F32)
            + (jnp.dot(a_lo, b_hi, preferred_element_type=F32) + jnp.dot(a_hi, b_lo, preferred_element_type=F32)))


def _merge_kernel(x_ref, hf_ref, hb_ref, gy_ref, ga_ref, gr_ref, wrec_ref, wout_ref, g_ref,
                  wr_hi_ref, wr_lo_ref, br_ref, h1_ref, n2_ref, gates_ref):
    rec_in = ((hf_ref[...] + hb_ref[...]) * gy_ref[...].astype(F32)).astype(BF16)
    rec = jnp.dot(rec_in, wrec_ref[...], preferred_element_type=F32)
    mix = (ga_ref[...].astype(F32) + gr_ref[...].astype(F32) * rec).astype(BF16)
    h1 = x_ref[...] + jnp.dot(mix, wout_ref[...], preferred_element_type=F32)
    h1_ref[...] = h1
    n2 = _rms_norm(h1, g_ref[...])
    n2_ref[...] = n2.astype(BF16)

    logits = _split_dot(n2, wr_hi_ref[...], wr_lo_ref[...]) + br_ref[...]
    lane_i = lax.broadcasted_iota(jnp.int32, logits.shape, 1)
    lane = lane_i.astype(F32)
    first = lambda mask: jnp.min(jnp.where(mask, lane, float(_ROUTER_LANES)), axis=-1, keepdims=True)
    lg = jnp.where(lane < N_GROUPS, logits, -jnp.inf)
    g_max = jnp.max(lg, axis=-1, keepdims=True)
    g_top_p = 1.0 / jnp.sum(jnp.exp(lg - g_max), axis=-1, keepdims=True)
    g_idx = first(lg == g_max)
    e = lane_i - N_GROUPS
    e_group = jnp.right_shift(e, int(math.log2(EXPERTS_PER_GROUP))).astype(F32)
    in_group = (e >= 0) & (e < N_EXPERTS) & (e_group == g_idx)
    le = jnp.where(in_group, logits, -jnp.inf)
    m1 = jnp.max(le, axis=-1, keepdims=True)
    i1 = first(le == m1)
    le2 = jnp.where(lane == i1, -jnp.inf, le)
    m2 = jnp.max(le2, axis=-1, keepdims=True)
    i2 = first(le2 == m2)
    e2 = jnp.exp(m2 - m1)
    w1 = g_top_p / (1.0 + e2)
    w2 = g_top_p * e2 / (1.0 + e2)
    gates_ref[...] = jnp.where(lane == i1, w1, 0.0) + jnp.where(lane == i2, w2, 0.0)


def _merge(x2, hf, hb, gy, ga, gr, wrec_bf, wout_bf, g, wr_hi, wr_lo, br, tm):
    n_rows = x2.shape[0]
    row = lambda w: pl.BlockSpec((tm, w), lambda i: (i, 0))
    full = lambda a: pl.BlockSpec(a.shape, lambda i: (0,) * a.ndim)
    return pl.pallas_call(
        _merge_kernel,
        grid=(n_rows // tm,),
        in_specs=[row(D_MODEL)] * 6 + [full(wrec_bf), full(wout_bf), full(g), full(wr_hi), full(wr_lo), full(br)],
        out_specs=(row(D_MODEL), row(D_MODEL), row(_ROUTER_LANES)),
        out_shape=(jax.ShapeDtypeStruct((n_rows, D_MODEL), F32),
                   jax.ShapeDtypeStruct((n_rows, D_MODEL), BF16),
                   jax.ShapeDtypeStruct((n_rows, _ROUTER_LANES), F32)),
        compiler_params=_params(1),
        name="merge",
    )(x2, hf, hb, gy, ga, gr, wrec_bf, wout_bf, g, wr_hi, wr_lo, br)


_TT = 512
_CHUNK = 16
_TM = 512
_SLOTS = 1280
_BIG = 1.0e6


def _moe_plan(gates, n_tiles):
    i32 = jnp.int32
    sel = gates[:, N_GROUPS:N_GROUPS + N_EXPERTS] > 0.0
    cnt = jnp.sum(sel.reshape(n_tiles, _TT, N_EXPERTS), axis=1, dtype=i32)
    padc = (cnt + _CHUNK - 1) // _CHUNK * _CHUNK
    lstart = jnp.cumsum(padc, axis=1) - padc
    tot = jnp.sum(padc, axis=0)
    ntile = (tot + _TM - 1) // _TM
    tile_end = jnp.cumsum(ntile)
    base = (tile_end - ntile) * _TM
    roff = base[None, :] + jnp.cumsum(padc, axis=0) - padc
    n_active = tile_end[-1]
    max_tiles = (2 * n_tiles * _TT + n_tiles * N_EXPERTS * (_CHUNK - 1)) // _TM + N_EXPERTS
    g = jnp.minimum(jnp.arange(max_tiles, dtype=i32), n_active - 1)
    tile_expert = jnp.sum(g[:, None] >= tile_end[None, :], axis=1, dtype=i32)
    lstart_vec = jnp.zeros((n_tiles, 1, _ROUTER_LANES), F32).at[:, 0, N_GROUPS:N_GROUPS + N_EXPERTS].set(
        lstart.astype(F32))
    plan = dict(
        nchunk=(padc // _CHUNK).reshape(-1), lstart=lstart.reshape(-1), roff=roff.reshape(-1),
        tile_chunks=jnp.sum(padc // _CHUNK, axis=1, dtype=i32),
        tail_start=base + tot, tail_chunks=(ntile * _TM - tot) // _CHUNK,
        tile_expert=tile_expert, tile_block=g, n_active=n_active.reshape(1), lstart_vec=lstart_vec)
    return plan, max_tiles


def _slot_positions(gates, lstart_vec):
    sel = gates > 0.0
    r = lax.broadcasted_iota(jnp.int32, (_TT, _TT), 0)
    c = lax.broadcasted_iota(jnp.int32, (_TT, _TT), 1)
    before = (c < r).astype(BF16)
    rank = jnp.dot(before, sel.astype(BF16), preferred_element_type=F32)
    return sel, rank + lstart_vec


def _dispatch_kernel(nchunk_ref, lstart_ref, roff_ref, tchunks_ref, tail_start_ref, tail_chunks_ref, n_active_ref,
                     n2_ref, gates_ref, lvec_ref, xs_ref, xloc_ref, zero_ref, sem, zsem):
    i = pl.program_id(0)
    last = pl.num_programs(0) - 1
    buf = i % 2

    def run_copy(b, src0, dst0, c):
        src = pl.multiple_of(src0 + c * _CHUNK, _CHUNK)
        dst = pl.multiple_of(dst0 + c * _CHUNK, _CHUNK)
        return pltpu.make_async_copy(xloc_ref.at[b, pl.ds(src, _CHUNK), :], xs_ref.at[pl.ds(dst, _CHUNK), :],
                                     sem.at[b])

    def wait_step(b, step):
        def wait(c, carry):
            run_copy(b, 0, 0, 0).wait()
            return carry

        lax.fori_loop(0, tchunks_ref[step], wait, 0)

    @pl.when(i >= 2)
    def _buffer_free():
        wait_step(buf, i - 2)

    sel, pos = _slot_positions(gates_ref[...], lvec_ref[0])
    lo = jnp.min(jnp.where(sel, pos, _BIG).T, axis=0, keepdims=True)
    hi = jnp.max(jnp.where(sel, pos, -1.0).T, axis=0, keepdims=True)
    slot = lax.broadcasted_iota(jnp.int32, (_SLOTS, _TT), 0).astype(F32)
    onehot = ((slot == lo) | (slot == hi)).astype(BF16)
    xloc_ref[buf] = jnp.dot(onehot, n2_ref[...], preferred_element_type=F32).astype(BF16)

    for e in range(N_EXPERTS):
        idx = i * N_EXPERTS + e
        src0, dst0 = lstart_ref[idx], roff_ref[idx]

        def start(c, carry, src0=src0, dst0=dst0):
            run_copy(buf, src0, dst0, c).start()
            return carry

        lax.fori_loop(0, nchunk_ref[idx], start, 0)

    @pl.when(i == last)
    def _drain():
        @pl.when(i >= 1)
        def _previous():
            wait_step(1 - buf, i - 1)

        wait_step(buf, i)

    @pl.when(i == last)
    def _zero_tails():
        zero_ref[...] = jnp.zeros_like(zero_ref)

        def tail_copy(dst0, c):
            dst = pl.multiple_of(dst0 + c * _CHUNK, _CHUNK)
            return pltpu.make_async_copy(zero_ref.at[pl.ds(0, _CHUNK), :], xs_ref.at[pl.ds(dst, _CHUNK), :], zsem)

        def tile_copy(t):
            dst = pl.multiple_of(t * _TM, _TM)
            return pltpu.make_async_copy(zero_ref, xs_ref.at[pl.ds(dst, _TM), :], zsem)

        n_tiles_total = xs_ref.shape[0] // _TM

        def tstart(t, carry):
            tile_copy(t).start()
            return carry

        def twait(t, carry):
            tile_copy(0).wait()
            return carry

        lax.fori_loop(n_active_ref[0], n_tiles_total, tstart, 0)
        lax.fori_loop(n_active_ref[0], n_tiles_total, twait, 0)

        for e in range(N_EXPERTS):
            dst0 = tail_start_ref[e]

            def zstart(c, carry, dst0=dst0):
                tail_copy(dst0, c).start()
                return carry

            def zwait(c, carry):
                tail_copy(0, 0).wait()
                return carry

            lax.fori_loop(0, tail_chunks_ref[e], zstart, 0)
            lax.fori_loop(0, tail_chunks_ref[e], zwait, 0)


def _dispatch(plan, n2, gates, n_tiles, n_sorted):
    row = lambda w: pl.BlockSpec((_TT, w), lambda i, *_: (i, 0))
    return pl.pallas_call(
        _dispatch_kernel,
        grid_spec=pltpu.PrefetchScalarGridSpec(
            num_scalar_prefetch=7,
            grid=(n_tiles,),
            in_specs=[row(D_MODEL), row(_ROUTER_LANES),
                      pl.BlockSpec((1, 1, _ROUTER_LANES), lambda i, *_: (i, 0, 0))],
            out_specs=pl.BlockSpec(memory_space=pl.ANY),
            scratch_shapes=[pltpu.VMEM((2, _SLOTS, D_MODEL), BF16), pltpu.VMEM((_TM, D_MODEL), BF16),
                            pltpu.SemaphoreType.DMA((2,)), pltpu.SemaphoreType.DMA],
        ),
        out_shape=jax.ShapeDtypeStruct((n_sorted, D_MODEL), BF16),
        compiler_params=_params(1),
        name="moe_dispatch",
    )(plan["nchunk"], plan["lstart"], plan["roff"], plan["tile_chunks"], plan["tail_start"], plan["tail_chunks"],
      plan["n_active"], n2, gates, plan["lstart_vec"])


def _experts_kernel(tile_expert_ref, tile_block_ref, n_active_ref, xs_ref, wg_ref, wu_ref, wd_ref, ys_ref):
    active = pl.program_id(0) < n_active_ref[0]

    @pl.when(active)
    def _ffn():
        xs = xs_ref[...]
        gate = jnp.dot(xs, wg_ref[0], preferred_element_type=F32)
        up = jnp.dot(xs, wu_ref[0], preferred_element_type=F32)
        hidden = (gate * _sigmoid(gate) * up).astype(BF16)
        ys_ref[...] = jnp.dot(hidden, wd_ref[0], preferred_element_type=F32).astype(BF16)

    @pl.when(jnp.logical_not(active))
    def _unused_tile():
        ys_ref[...] = jnp.zeros_like(ys_ref)


def _experts(plan, xs, wg_bf, wu_bf, wd_bf, max_tiles):
    rows_in = pl.BlockSpec((_TM, D_MODEL), lambda g, te, tb, na: (tb[g], 0))
    rows_out = pl.BlockSpec((_TM, D_MODEL), lambda g, te, tb, na: (g, 0))
    expert = lambda a: pl.BlockSpec((1,) + a.shape[1:], lambda g, te, tb, na: (te[g], 0, 0))
    return pl.pallas_call(
        _experts_kernel,
        grid_spec=pltpu.PrefetchScalarGridSpec(
            num_scalar_prefetch=3,
            grid=(max_tiles,),
            in_specs=[rows_in, expert(wg_bf), expert(wu_bf), expert(wd_bf)],
            out_specs=rows_out,
        ),
        out_shape=jax.ShapeDtypeStruct(xs.shape, BF16),
        compiler_params=_params(1),
        name="moe_experts",
    )(plan["tile_expert"], plan["tile_block"], plan["n_active"], xs, wg_bf, wu_bf, wd_bf)


def _combine_kernel(nchunk_ref, lstart_ref, roff_ref, tchunks_ref,
                    gates_ref, lvec_ref, h1_ref, g_ref, ys_ref, o_ref, yloc_ref, sem):
    i = pl.program_id(0)
    buf = i % 2

    def run_copy(b, src0, dst0, c):
        src = pl.multiple_of(src0 + c * _CHUNK, _CHUNK)
        dst = pl.multiple_of(dst0 + c * _CHUNK, _CHUNK)
        return pltpu.make_async_copy(ys_ref.at[pl.ds(src, _CHUNK), :], yloc_ref.at[b, pl.ds(dst, _CHUNK), :],
                                     sem.at[b])

    def fetch(b, step):
        for e in range(N_EXPERTS):
            idx = step * N_EXPERTS + e
            src0, dst0 = roff_ref[idx], lstart_ref[idx]

            def start(c, carry, src0=src0, dst0=dst0):
                run_copy(b, src0, dst0, c).start()
                return carry

            lax.fori_loop(0, nchunk_ref[idx], start, 0)

    @pl.when(i == 0)
    def _first():
        yloc_ref[...] = jnp.zeros_like(yloc_ref)
        fetch(buf, i)

    @pl.when(i + 1 < pl.num_programs(0))
    def _prefetch():
        fetch(1 - buf, i + 1)

    gates = gates_ref[...]
    sel, pos = _slot_positions(gates, lvec_ref[0])
    pos_lo = jnp.where(sel, pos, _BIG)
    pos_hi = jnp.where(sel, pos, -1.0)
    lo = jnp.min(pos_lo, axis=-1, keepdims=True)
    hi = jnp.max(pos_hi, axis=-1, keepdims=True)
    w_lo = jnp.sum(jnp.where(pos_lo == lo, gates, 0.0), axis=-1, keepdims=True)
    w_hi = jnp.where(hi == lo, 0.0, jnp.sum(jnp.where(pos_hi == hi, gates, 0.0), axis=-1, keepdims=True))
    slot = lax.broadcasted_iota(jnp.int32, (_TT, _SLOTS), 1).astype(F32)
    weights = (jnp.where(slot == lo, w_lo, 0.0) + jnp.where(slot == hi, w_hi, 0.0)).astype(BF16)

    def wait(c, carry):
        run_copy(buf, 0, 0, 0).wait()
        return carry

    lax.fori_loop(0, tchunks_ref[i], wait, 0)
    moe = jnp.dot(weights, yloc_ref[buf], preferred_element_type=F32)
    o_ref[...] = _rms_norm(h1_ref[...] + moe, g_ref[...])


def _combine(plan, gates, h1, g, ys, n_tiles):
    row = lambda w: pl.BlockSpec((_TT, w), lambda i, *_: (i, 0))
    return pl.pallas_call(
        _combine_kernel,
        grid_spec=pltpu.PrefetchScalarGridSpec(
            num_scalar_prefetch=4,
            grid=(n_tiles,),
            in_specs=[row(_ROUTER_LANES), pl.BlockSpec((1, 1, _ROUTER_LANES), lambda i, *_: (i, 0, 0)),
                      row(D_MODEL), pl.BlockSpec(g.shape, lambda i, *_: (0, 0)),
                      pl.BlockSpec(memory_space=pl.ANY)],
            out_specs=row(D_MODEL),
            scratch_shapes=[pltpu.VMEM((2, _SLOTS, D_MODEL), BF16), pltpu.SemaphoreType.DMA((2,))],
        ),
        out_shape=jax.ShapeDtypeStruct(h1.shape, F32),
        compiler_params=_params(1),
        name="moe_combine",
    )(plan["nchunk"], plan["lstart"], plan["roff"], plan["tile_chunks"], gates, plan["lstart_vec"], h1, g, ys)


def _moe(n2, gates, h1, wg_bf, wu_bf, wd_bf, g):
    n_rows = n2.shape[0]
    assert n_rows % _TT == 0 and _SLOTS >= 2 * _TT + N_EXPERTS * (_CHUNK - 1)
    n_tiles = n_rows // _TT
    plan, max_tiles = _moe_plan(gates, n_tiles)
    xs = _dispatch(plan, n2, gates, n_tiles, max_tiles * _TM)
    ys = _experts(plan, xs, wg_bf, wu_bf, wd_bf, max_tiles)
    return _combine(plan, gates, h1, g, ys, n_tiles)


def kernel(x, meta_tokens, norm_mix_g, w_in, conv_w, conv_b, lru_w_a, lru_b_a, lru_w_x, lru_b_x, lru_lambda, attn_sink, w_attn_branch, w_rec_branch, w_out, norm_ffn_g, w_group, b_group, w_router, b_router, moe_w_gate, moe_w_up, moe_w_down, final_norm_g):
    batch, seq, _ = x.shape
    assert norm_mix_g.shape[0] == 1, "single-layer block"
    assert seq % _TQ == 0 and seq % _TC == 0
    n_rows = batch * seq
    x2 = x.reshape(n_rows, D_MODEL)
    row = lambda a: a.reshape(1, -1).astype(F32)

    w_in_bf = w_in[0].astype(BF16)
    g_mix = row(norm_mix_g[0])
    q, k, v, xr, gy, ga, gr = _in_proj(x2, g_mix, w_in_bf, 512)
    _, k_meta, v_meta, xr_meta, _, _, _ = _in_proj(meta_tokens.astype(F32), g_mix, w_in_bf, N_META)

    sink_rows = jnp.broadcast_to((attn_sink[0].astype(F32) * HEAD_DIM ** 0.5)[:, None, None],
                                 (N_HEADS, BLOCK, BLOCK)).reshape(N_HEADS * BLOCK, BLOCK)
    shape3 = lambda a: a.reshape(batch, seq, a.shape[-1])
    pad_keys = lambda a: jnp.pad(a, ((0, BLOCK - N_META), (0, 0)))
    attn = _attention(shape3(q), shape3(k), shape3(v), pad_keys(k_meta), pad_keys(v_meta), sink_rows, shape3(ga),
                      w_attn_branch[0].astype(BF16))

    h_dirs = []
    for d, reverse in enumerate((False, True)):
        wg = jnp.concatenate([lru_w_a[0, d], lru_w_x[0, d]], axis=-1).astype(BF16)
        bg_half = 0.5 * jnp.stack([lru_b_a[0, d], lru_b_x[0, d]]).astype(F32)
        h_dirs.append(_lru(shape3(xr), xr_meta, 0.5 * conv_w[0].astype(F32), 0.5 * row(conv_b[0]), wg, bg_half,
                           row(lru_lambda[0, d]), reverse))

    w_route = jnp.concatenate([w_group[0], w_router[0]], axis=1).astype(F32)
    w_route = jnp.pad(w_route, ((0, 0), (0, _ROUTER_LANES - w_route.shape[1])))
    wr_hi = w_route.astype(BF16)
    wr_lo = (w_route - wr_hi.astype(F32)).astype(BF16)
    b_route = jnp.pad(jnp.concatenate([b_group[0], b_router[0]]).astype(F32),
                      (0, _ROUTER_LANES - N_GROUPS - N_EXPERTS)).reshape(1, _ROUTER_LANES)
    h1, n2, gates = _merge(x2, h_dirs[0].reshape(n_rows, LRU_WIDTH), h_dirs[1].reshape(n_rows, LRU_WIDTH),
                           gy, attn.reshape(n_rows, D_MODEL), gr,
                           w_rec_branch[0].astype(BF16), w_out[0].astype(BF16), row(norm_ffn_g[0]),
                           wr_hi, wr_lo, b_route, 512)

    out = _moe(n2, gates, h1, moe_w_gate[0].astype(BF16), moe_w_up[0].astype(BF16), moe_w_down[0].astype(BF16),
               row(final_norm_g))
    return out.reshape(batch, seq, D_MODEL)
```

```python
import functools
import math

import jax
import jax.numpy as jnp
from jax import lax
from jax.experimental import pallas as pl
from jax.experimental.pallas import tpu as pltpu

D_MODEL = 1024
N_META = 16
N_HEADS = 8
N_KV_HEADS = 2
HEAD_DIM = 128
Q_PER_KV = N_HEADS // N_KV_HEADS
ATTN_WIDTH = N_HEADS * HEAD_DIM
KV_WIDTH = N_KV_HEADS * HEAD_DIM
WINDOW = 128
BLOCK = 128
LRU_WIDTH = D_MODEL
LRU_BLOCKS = 8
LRU_BLOCK_DIM = LRU_WIDTH // LRU_BLOCKS
CONV_WIDTH = 4
LRU_C = 8.0
N_GROUPS = 4
EXPERTS_PER_GROUP = 4
N_EXPERTS = N_GROUPS * EXPERTS_PER_GROUP
EXPERT_FF = 512
IN_WIDTH = ATTN_WIDTH + 2 * KV_WIDTH + 2 * LRU_WIDTH + 2 * D_MODEL
EPS = 1e-6
NEG_INF = -1e30

LANES = 128
SUBLANES = 8
VMEM_LIMIT = 56 * 1024 * 1024

BF16 = jnp.bfloat16
F32 = jnp.float32


def _params(n_grid_dims):
    return pltpu.CompilerParams(
        dimension_semantics=("arbitrary",) * n_grid_dims,
        vmem_limit_bytes=VMEM_LIMIT,
    )


def _sigmoid(x):
    return 0.5 * jnp.tanh(0.5 * x) + 0.5


def _gelu_tanh(x):
    c = math.sqrt(2.0 / math.pi)
    return 0.5 * x * (1.0 + jnp.tanh(c * (x + 0.044715 * (x * x * x))))


def _rms_norm(xf, g):
    ms = jnp.mean(xf * xf, axis=-1, keepdims=True)
    return xf * lax.rsqrt(ms + EPS) * g


_IN_CHUNK = 512


def _in_proj_kernel(x_ref, g_ref, w_ref, q_ref, k_ref, v_ref, xr_ref, gy_ref, ga_ref, gr_ref):
    n = _rms_norm(x_ref[...], g_ref[...]).astype(BF16)

    def proj(c0, width):
        return jnp.dot(n, w_ref[:, c0:c0 + width], preferred_element_type=F32)

    c = 0
    for j in range(ATTN_WIDTH // _IN_CHUNK):
        q_ref[:, j * _IN_CHUNK:(j + 1) * _IN_CHUNK] = proj(c, _IN_CHUNK).astype(BF16)
        c += _IN_CHUNK
    kv = proj(c, 2 * KV_WIDTH)
    k_ref[...] = kv[:, :KV_WIDTH].astype(BF16)
    v_ref[...] = kv[:, KV_WIDTH:].astype(BF16)
    c += 2 * KV_WIDTH
    for j in range(LRU_WIDTH // _IN_CHUNK):
        xr_ref[:, j * _IN_CHUNK:(j + 1) * _IN_CHUNK] = proj(c, _IN_CHUNK)
        c += _IN_CHUNK
    for j in range(LRU_WIDTH // _IN_CHUNK):
        gy_ref[:, j * _IN_CHUNK:(j + 1) * _IN_CHUNK] = _gelu_tanh(proj(c, _IN_CHUNK)).astype(BF16)
        c += _IN_CHUNK
    for ref in (ga_ref, gr_ref):
        for j in range(D_MODEL // _IN_CHUNK):
            ref[:, j * _IN_CHUNK:(j + 1) * _IN_CHUNK] = _sigmoid(proj(c, _IN_CHUNK)).astype(BF16)
            c += _IN_CHUNK


def _in_proj(x2, g, w_bf, tm):
    n_rows = x2.shape[0]
    row = lambda w: pl.BlockSpec((tm, w), lambda i: (i, 0))
    full = lambda a: pl.BlockSpec(a.shape, lambda i: (0,) * a.ndim)
    out_shapes = (
        jax.ShapeDtypeStruct((n_rows, ATTN_WIDTH), BF16),
        jax.ShapeDtypeStruct((n_rows, KV_WIDTH), BF16),
        jax.ShapeDtypeStruct((n_rows, KV_WIDTH), BF16),
        jax.ShapeDtypeStruct((n_rows, LRU_WIDTH), F32),
        jax.ShapeDtypeStruct((n_rows, LRU_WIDTH), BF16),
        jax.ShapeDtypeStruct((n_rows, D_MODEL), BF16),
        jax.ShapeDtypeStruct((n_rows, D_MODEL), BF16),
    )
    return pl.pallas_call(
        _in_proj_kernel,
        grid=(n_rows // tm,),
        in_specs=[row(D_MODEL), full(g), full(w_bf)],
        out_specs=tuple(row(s.shape[1]) for s in out_shapes),
        out_shape=out_shapes,
        compiler_params=_params(1),
        name="in_proj",
    )(x2, g, w_bf)


_TQ = 512
_SUB = _TQ // BLOCK
_GROUP_ROWS = Q_PER_KV * BLOCK


_KEYS = 4 * BLOCK
_SM_ROWS = 32


def _attn_kernel(q_ref, kp_ref, kc_ref, kn_ref, vp_ref, vc_ref, vn_ref, km_ref, vm_ref,
                 sink_ref, ga_ref, w_ref, o_ref, bias_ref, attn_ref, s_ref, p_ref, m_ref):
    i = pl.program_id(1)
    n_i = pl.num_programs(1)
    scale = HEAD_DIM ** -0.5
    exp_scale = scale * math.log2(math.e)

    @pl.when((pl.program_id(0) == 0) & (i == 0))
    def _init_bias():
        r = lax.broadcasted_iota(jnp.int32, (BLOCK, BLOCK), 0)
        c = lax.broadcasted_iota(jnp.int32, (BLOCK, BLOCK), 1)
        d_prev = (r + BLOCK - c).astype(F32)
        d_cur = jnp.abs(r - c).astype(F32)
        d_next = (c + BLOCK - r).astype(F32)
        for h in range(N_HEADS):
            slope = 2.0 ** (-8.0 * (h + 1.0) / N_HEADS) / scale
            rows = slice(h * BLOCK, (h + 1) * BLOCK)
            bias_ref[rows, 0:BLOCK] = jnp.where(c >= r, -slope * d_prev, NEG_INF / scale)
            bias_ref[rows, BLOCK:2 * BLOCK] = -slope * d_cur
            bias_ref[rows, 2 * BLOCK:3 * BLOCK] = jnp.where(c <= r, -slope * d_next, NEG_INF / scale)
            bias_ref[rows, 3 * BLOCK:4 * BLOCK] = jnp.where(c < N_META, 0.0, NEG_INF / scale)

    nt = (((1,), (1,)), ((), ()))
    for j in range(_SUB):
        rows = slice(j * BLOCK, (j + 1) * BLOCK)
        q = q_ref[0, rows, :]
        if j == 0:
            k3 = [kp_ref[0], kc_ref[0, 0:2 * BLOCK, :]]
            v3 = [vp_ref[0], vc_ref[0, 0:2 * BLOCK, :]]
        elif j == _SUB - 1:
            k3 = [kc_ref[0, (j - 1) * BLOCK:(j + 1) * BLOCK, :], kn_ref[0]]
            v3 = [vc_ref[0, (j - 1) * BLOCK:(j + 1) * BLOCK, :], vn_ref[0]]
        else:
            k3 = [kc_ref[0, (j - 1) * BLOCK:(j + 2) * BLOCK, :]]
            v3 = [vc_ref[0, (j - 1) * BLOCK:(j + 2) * BLOCK, :]]
        k_cat = jnp.concatenate(k3 + [km_ref[...]], axis=0)
        v_cat = jnp.concatenate(v3 + [vm_ref[...]], axis=0)
        masked = []
        if j == 0:
            masked.append((slice(0, BLOCK), i == 0))
        if j == _SUB - 1:
            masked.append((slice(2 * BLOCK, 3 * BLOCK), i == n_i - 1))
        for g in range(N_KV_HEADS):
            cols = slice(g * HEAD_DIM, (g + 1) * HEAD_DIM)
            row0 = g * _GROUP_ROWS
            qg = jnp.concatenate(
                [q[:, (g * Q_PER_KV + h) * HEAD_DIM:(g * Q_PER_KV + h + 1) * HEAD_DIM] for h in range(Q_PER_KV)],
                axis=0)
            pair = j * N_KV_HEADS + g
            s_ref[pair] = lax.dot_general(qg, k_cat[:, cols], nt, preferred_element_type=F32)
            for mask_cols, mask_on in masked:
                s_ref[pair, :, mask_cols] = jnp.where(mask_on, NEG_INF / scale, s_ref[pair, :, mask_cols])

            chunks = [(slice(c * _SM_ROWS, (c + 1) * _SM_ROWS), slice(row0 + c * _SM_ROWS, row0 + (c + 1) * _SM_ROWS))
                      for c in range(_GROUP_ROWS // _SM_ROWS)]
            wide = lambda col: jnp.broadcast_to(col, (_SM_ROWS, BLOCK))
            tiled = lambda stat: jnp.concatenate([stat] * (_KEYS // BLOCK), axis=1)
            for r, rb in chunks:
                z = s_ref[pair, r, :] + bias_ref[rb, :]
                m_ref[pair, r, :] = jnp.maximum(wide(jnp.max(z, axis=-1, keepdims=True)), sink_ref[rb, :])
            for r, rb in chunks:
                m = m_ref[pair, r, :]
                p = jnp.exp2((s_ref[pair, r, :] + bias_ref[rb, :] - tiled(m)) * exp_scale)
                denom = wide(jnp.sum(p, axis=-1, keepdims=True)) + jnp.exp2((sink_ref[rb, :] - m) * exp_scale)
                p_ref[pair, r, :] = p.astype(BF16)
                m_ref[pair, r, :] = 1.0 / denom
            o = jnp.dot(p_ref[pair], v_cat[:, cols], preferred_element_type=F32)
            o = (o * m_ref[pair]).astype(BF16)
            for h in range(Q_PER_KV):
                head = g * Q_PER_KV + h
                attn_ref[rows, head * HEAD_DIM:(head + 1) * HEAD_DIM] = o[h * BLOCK:(h + 1) * BLOCK, :]

    proj = jnp.dot(attn_ref[...], w_ref[...], preferred_element_type=F32)
    o_ref[0] = (ga_ref[0].astype(F32) * proj).astype(BF16)


def _attention(q, k, v, k_meta, v_meta, sink_rows, g_attn, w_bf):
    batch, seq, _ = q.shape
    n_blk = seq // BLOCK
    main = lambda w: pl.BlockSpec((1, _TQ, w), lambda b, i: (b, i, 0))
    prev = pl.BlockSpec((1, BLOCK, KV_WIDTH), lambda b, i: (b, jnp.maximum(i * _SUB - 1, 0), 0))
    nxt = pl.BlockSpec((1, BLOCK, KV_WIDTH), lambda b, i: (b, jnp.minimum((i + 1) * _SUB, n_blk - 1), 0))
    full = lambda a: pl.BlockSpec(a.shape, lambda b, i: (0,) * a.ndim)
    return pl.pallas_call(
        _attn_kernel,
        grid=(batch, seq // _TQ),
        in_specs=[main(ATTN_WIDTH), prev, main(KV_WIDTH), nxt, prev, main(KV_WIDTH), nxt,
                  full(k_meta), full(v_meta), full(sink_rows), main(D_MODEL), full(w_bf)],
        out_specs=main(D_MODEL),
        out_shape=jax.ShapeDtypeStruct((batch, seq, D_MODEL), BF16),
        scratch_shapes=[pltpu.VMEM((N_HEADS * BLOCK, _KEYS), F32),
                        pltpu.VMEM((_TQ, ATTN_WIDTH), BF16),
                        pltpu.VMEM((_SUB * N_KV_HEADS, _GROUP_ROWS, _KEYS), F32),
                        pltpu.VMEM((_SUB * N_KV_HEADS, _GROUP_ROWS, _KEYS), BF16),
                        pltpu.VMEM((_SUB * N_KV_HEADS, _GROUP_ROWS, BLOCK), F32)],
        compiler_params=_params(2),
        name="attention",
    )(q, k, k, k, v, v, v, k_meta, v_meta, sink_rows, g_attn, w_bf)


_TC = 512
_HALO = SUBLANES


def _interleave_in(dst_ref, src, n_rows):
    seg = n_rows // SUBLANES
    for n in range(LRU_BLOCKS):
        for s in range(SUBLANES):
            dst_ref[n, pl.ds(s, seg, stride=SUBLANES), :] = src(s * seg, seg, n)


def _interleave_out(write, src_ref, n_rows):
    seg = n_rows // SUBLANES
    for n in range(LRU_BLOCKS):
        for s in range(SUBLANES):
            write(s * seg, seg, n, src_ref[n, pl.ds(s, seg, stride=SUBLANES), :])


def _lru_gates(n_rows, prev2, prev1, next0, x_ref, cw_ref, cb_ref, wg_ref, bg_ref, lam_ref, a_ref, u_ref):
    seg = n_rows // SUBLANES
    sub = lax.broadcasted_iota(jnp.int32, (SUBLANES, LRU_BLOCK_DIM), 0)
    lam = lam_ref[...]
    decay_scale = (-0.5 * LRU_C * math.log2(math.e)) * (
        jnp.maximum(-lam, 0.0) + jnp.log(1.0 + jnp.exp(-jnp.abs(lam))))
    for n in range(LRU_BLOCKS):
        cols = slice(n * LRU_BLOCK_DIM, (n + 1) * LRU_BLOCK_DIM)
        x = x_ref[n, 0:n_rows, :]
        group = lambda j: x[j * SUBLANES:(j + 1) * SUBLANES, :]
        e0 = jnp.where(sub == 0, prev2(n), pltpu.roll(group(seg - 2), 1, axis=0))
        e1 = jnp.where(sub == 0, prev1(n), pltpu.roll(group(seg - 1), 1, axis=0))
        e_next = jnp.where(sub == SUBLANES - 1, next0(n), pltpu.roll(group(0), SUBLANES - 1, axis=0))
        ext = jnp.concatenate([e0, e1, x, e_next], axis=0)
        xh = cb_ref[:, cols] + sum(
            cw_ref[t:t + 1, cols] * ext[t * SUBLANES:t * SUBLANES + n_rows, :] for t in range(CONV_WIDTH))
        pre = jnp.dot(xh.astype(BF16), wg_ref[n], preferred_element_type=F32)
        t_a = jnp.tanh(pre[:, :LRU_BLOCK_DIM] + bg_ref[0:1, cols])
        t_x = jnp.tanh(pre[:, LRU_BLOCK_DIM:] + bg_ref[1:2, cols])
        scale = decay_scale[:, cols]
        a = jnp.exp2(t_a * scale + scale)
        y = 1.0 - a * a
        a_ref[n, 0:n_rows, :] = a
        u_ref[n, 0:n_rows, :] = (y * lax.rsqrt(jnp.maximum(y, 1e-30))) * ((t_x + 1.0) * xh)


def _lru_scan(n_rows, reverse, carry_in, a_ref, u_ref, h_ref):
    seg = n_rows // SUBLANES
    unroll = min(16, seg)
    sub = lax.broadcasted_iota(jnp.int32, (SUBLANES, LRU_BLOCK_DIM), 0)
    blocks = range(LRU_BLOCKS)

    def rows(jj):
        j = (seg - 1 - jj) if reverse else jj
        return pl.ds(pl.multiple_of(j * SUBLANES, SUBLANES), SUBLANES)

    def local(jj, state):
        hs, ps = state
        r = rows(jj)
        a = [a_ref[n, r, :] for n in blocks]
        return (tuple(a[n] * hs[n] + u_ref[n, r, :] for n in blocks), tuple(a[n] * ps[n] for n in blocks))

    zeros = tuple(jnp.zeros((SUBLANES, LRU_BLOCK_DIM), F32) for _ in blocks)
    ones = tuple(jnp.ones((SUBLANES, LRU_BLOCK_DIM), F32) for _ in blocks)
    h_end, p_end = lax.fori_loop(0, seg, local, (zeros, ones), unroll=unroll)

    seg_in, carry_out = [], []
    for n in blocks:
        p, h = p_end[n], h_end[n]
        for d in (1, 2, 4):
            shift = SUBLANES - d if reverse else d
            ok = (sub < SUBLANES - d) if reverse else (sub >= d)
            h = h + p * jnp.where(ok, pltpu.roll(h, shift, axis=0), 0.0)
            p = p * jnp.where(ok, pltpu.roll(p, shift, axis=0), 1.0)
        seg_out = h + p * carry_in[n]
        first, last = (SUBLANES - 1, 0) if reverse else (0, SUBLANES - 1)
        shift = SUBLANES - 1 if reverse else 1
        seg_in.append(jnp.where(sub == first, carry_in[n], pltpu.roll(seg_out, shift, axis=0)))
        carry_out.append(seg_out[last:last + 1, :])

    if h_ref is not None:
        def final(jj, hs):
            r = rows(jj)
            new = tuple(a_ref[n, r, :] * hs[n] + u_ref[n, r, :] for n in blocks)
            for n in blocks:
                h_ref[n, r, :] = new[n]
            return new

        lax.fori_loop(0, seg, final, tuple(seg_in), unroll=unroll)
    return carry_out


def _lru_kernel(reverse, xr_ref, xp_ref, xn_ref, xm_ref, cw_ref, cb_ref, wg_ref, bg_ref, lam_ref,
                h_ref, carry_ref, x_scr, a_scr, u_scr, h_scr):
    step = pl.program_id(1)
    n_steps = pl.num_programs(1)
    t = (n_steps - 1 - step) if reverse else step
    args = (x_scr, cw_ref, cb_ref, wg_ref, bg_ref, lam_ref, a_scr, u_scr)
    lanes = lambda n: slice(n * LRU_BLOCK_DIM, (n + 1) * LRU_BLOCK_DIM)
    zero_row = lambda n: jnp.zeros((1, LRU_BLOCK_DIM), F32)

    if reverse:
        @pl.when(step == 0)
        def _zero_state():
            carry_ref[...] = jnp.zeros_like(carry_ref)
    else:
        @pl.when(step == 0)
        def _meta_state():
            _interleave_in(x_scr, lambda r0, nr, n: xm_ref[r0:r0 + nr, lanes(n)], N_META)
            _lru_gates(N_META, zero_row, zero_row, lambda n: xr_ref[0, 0:1, lanes(n)], *args)
            state = _lru_scan(N_META, False, [zero_row(n) for n in range(LRU_BLOCKS)], a_scr, u_scr, None)
            for n in range(LRU_BLOCKS):
                carry_ref[0:1, lanes(n)] = state[n]

    def before(row):
        return lambda n: jnp.where(t == 0, xm_ref[N_META - _HALO + row:N_META - _HALO + row + 1, lanes(n)],
                                   xp_ref[0, row:row + 1, lanes(n)])

    after = lambda n: jnp.where(t == n_steps - 1, 0.0, xn_ref[0, 0:1, lanes(n)])
    _interleave_in(x_scr, lambda r0, nr, n: xr_ref[0, r0:r0 + nr, lanes(n)], _TC)
    _lru_gates(_TC, before(_HALO - 2), before(_HALO - 1), after, *args)
    state = _lru_scan(_TC, reverse, [carry_ref[0:1, lanes(n)] for n in range(LRU_BLOCKS)], a_scr, u_scr, h_scr)
    for n in range(LRU_BLOCKS):
        carry_ref[0:1, lanes(n)] = state[n]

    def write(r0, nr, n, rows):
        h_ref[0, r0:r0 + nr, lanes(n)] = rows.astype(h_ref.dtype)

    _interleave_out(write, h_scr, _TC)


def _lru(xr, xr_meta, conv_w, conv_b, wg_bf, bg, lam, reverse):
    batch, seq, _ = xr.shape
    n_steps = seq // _TC
    n_halo = seq // _HALO
    per_tile = _TC // _HALO
    tile = (lambda s: n_steps - 1 - s) if reverse else (lambda s: s)
    main = pl.BlockSpec((1, _TC, LRU_WIDTH), lambda b, s: (b, tile(s), 0))
    before = pl.BlockSpec((1, _HALO, LRU_WIDTH), lambda b, s: (b, jnp.maximum(tile(s) * per_tile - 1, 0), 0))
    after = pl.BlockSpec((1, _HALO, LRU_WIDTH),
                         lambda b, s: (b, jnp.minimum((tile(s) + 1) * per_tile, n_halo - 1), 0))
    full = lambda a: pl.BlockSpec(a.shape, lambda b, s: (0,) * a.ndim)
    return pl.pallas_call(
        functools.partial(_lru_kernel, reverse),
        grid=(batch, n_steps),
        in_specs=[main, before, after, full(xr_meta), full(conv_w), full(conv_b), full(wg_bf), full(bg),
                  full(lam)],
        out_specs=main,
        out_shape=jax.ShapeDtypeStruct((batch, seq, LRU_WIDTH), BF16),
        scratch_shapes=[pltpu.VMEM((SUBLANES, LRU_WIDTH), F32)]
        + [pltpu.VMEM((LRU_BLOCKS, _TC, LRU_BLOCK_DIM), F32)] * 4,
        compiler_params=_params(2),
        name="lru_bwd" if reverse else "lru_fwd",
    )(xr, xr, xr, xr_meta, conv_w, conv_b, wg_bf, bg, lam)


_ROUTER_LANES = LANES


def _split_dot(a, b_hi, b_lo):
    a_hi = a.astype(BF16)
    a_lo = (a - a_hi.astype(F32)).astype(BF16)
    return (jnp.dot(a_hi, b_hi, preferred_element_type=F32)
            + (jnp.dot(a_lo, b_hi, preferred_element_type=F32) + jnp.dot(a_hi, b_lo, preferred_element_type=F32)))


def _merge_kernel(x_ref, hf_ref, hb_ref, gy_ref, ga_ref, gr_ref, wrec_ref, wout_ref, g_ref,
                  wr_hi_ref, wr_lo_ref, br_ref, h1_ref, n2_ref, gates_ref):
    rec_in = (hf_ref[...] + hb_ref[...]) * gy_ref[...]
    rec = jnp.dot(rec_in, wrec_ref[...], preferred_element_type=F32)
    mix = (ga_ref[...].astype(F32) + gr_ref[...].astype(F32) * rec).astype(BF16)
    h1 = x_ref[...] + jnp.dot(mix, wout_ref[...], preferred_element_type=F32)
    h1_ref[...] = h1
    n2 = _rms_norm(h1, g_ref[...])
    n2_ref[...] = n2.astype(BF16)

    logits = _split_dot(n2, wr_hi_ref[...], wr_lo_ref[...]) + br_ref[...]
    lane_i = lax.broadcasted_iota(jnp.int32, logits.shape, 1)
    lane = lane_i.astype(F32)
    first = lambda mask: jnp.min(jnp.where(mask, lane, float(_ROUTER_LANES)), axis=-1, keepdims=True)
    lg = jnp.where(lane < N_GROUPS, logits, -jnp.inf)
    g_max = jnp.max(lg, axis=-1, keepdims=True)
    g_top_p = 1.0 / jnp.sum(jnp.exp(lg - g_max), axis=-1, keepdims=True)
    g_idx = first(lg == g_max)
    e = lane_i - N_GROUPS
    e_group = jnp.right_shift(e, int(math.log2(EXPERTS_PER_GROUP))).astype(F32)
    in_group = (e >= 0) & (e < N_EXPERTS) & (e_group == g_idx)
    le = jnp.where(in_group, logits, -jnp.inf)
    m1 = jnp.max(le, axis=-1, keepdims=True)
    i1 = first(le == m1)
    le2 = jnp.where(lane == i1, -jnp.inf, le)
    m2 = jnp.max(le2, axis=-1, keepdims=True)
    i2 = first(le2 == m2)
    e2 = jnp.exp(m2 - m1)
    w1 = g_top_p / (1.0 + e2)
    w2 = g_top_p * e2 / (1.0 + e2)
    gates_ref[...] = jnp.where(lane == i1, w1, 0.0) + jnp.where(lane == i2, w2, 0.0)


def _merge(x2, hf, hb, gy, ga, gr, wrec_bf, wout_bf, g, wr_hi, wr_lo, br, tm):
    n_rows = x2.shape[0]
    row = lambda w: pl.BlockSpec((tm, w), lambda i: (i, 0))
    full = lambda a: pl.BlockSpec(a.shape, lambda i: (0,) * a.ndim)
    return pl.pallas_call(
        _merge_kernel,
        grid=(n_rows // tm,),
        in_specs=[row(D_MODEL)] * 6 + [full(wrec_bf), full(wout_bf), full(g), full(wr_hi), full(wr_lo), full(br)],
        out_specs=(row(D_MODEL), row(D_MODEL), row(_ROUTER_LANES)),
        out_shape=(jax.ShapeDtypeStruct((n_rows, D_MODEL), F32),
                   jax.ShapeDtypeStruct((n_rows, D_MODEL), BF16),
                   jax.ShapeDtypeStruct((n_rows, _ROUTER_LANES), F32)),
        compiler_params=_params(1),
        name="merge",
    )(x2, hf, hb, gy, ga, gr, wrec_bf, wout_bf, g, wr_hi, wr_lo, br)


_TT = 512
_CHUNK = 16
_TM = 512
_SLOTS = 1280
_BIG = 1.0e6


def _moe_plan(gates, n_tiles):
    i32 = jnp.int32
    sel = gates[:, N_GROUPS:N_GROUPS + N_EXPERTS] > 0.0
    cnt = jnp.sum(sel.reshape(n_tiles, _TT, N_EXPERTS), axis=1, dtype=i32)
    padc = (cnt + _CHUNK - 1) // _CHUNK * _CHUNK
    lstart = jnp.cumsum(padc, axis=1) - padc
    tot = jnp.sum(padc, axis=0)
    ntile = (tot + _TM - 1) // _TM
    tile_end = jnp.cumsum(ntile)
    base = (tile_end - ntile) * _TM
    roff = base[None, :] + jnp.cumsum(padc, axis=0) - padc
    n_active = tile_end[-1]
    max_tiles = (2 * n_tiles * _TT + n_tiles * N_EXPERTS * (_CHUNK - 1)) // _TM + N_EXPERTS
    g = jnp.minimum(jnp.arange(max_tiles, dtype=i32), n_active - 1)
    tile_expert = jnp.sum(g[:, None] >= tile_end[None, :], axis=1, dtype=i32)
    lstart_vec = jnp.zeros((n_tiles, 1, _ROUTER_LANES), F32).at[:, 0, N_GROUPS:N_GROUPS + N_EXPERTS].set(
        lstart.astype(F32))
    plan = dict(
        nchunk=(padc // _CHUNK).reshape(-1), lstart=lstart.reshape(-1), roff=roff.reshape(-1),
        tile_chunks=jnp.sum(padc // _CHUNK, axis=1, dtype=i32),
        tail_start=base + tot, tail_chunks=(ntile * _TM - tot) // _CHUNK,
        tile_expert=tile_expert, tile_block=g, n_active=n_active.reshape(1), lstart_vec=lstart_vec)
    return plan, max_tiles


def _slot_positions(gates, lstart_vec):
    sel = gates > 0.0
    r = lax.broadcasted_iota(jnp.int32, (_TT, _TT), 0)
    c = lax.broadcasted_iota(jnp.int32, (_TT, _TT), 1)
    before = (c < r).astype(BF16)
    rank = jnp.dot(before, sel.astype(BF16), preferred_element_type=F32)
    return sel, rank + lstart_vec


def _dispatch_kernel(nchunk_ref, lstart_ref, roff_ref, tchunks_ref, tail_start_ref, tail_chunks_ref, n_active_ref,
                     n2_ref, gates_ref, lvec_ref, xs_ref, xloc_ref, zero_ref, sem, zsem):
    i = pl.program_id(0)
    last = pl.num_programs(0) - 1
    buf = i % 2

    def run_copy(b, src0, dst0, c):
        src = pl.multiple_of(src0 + c * _CHUNK, _CHUNK)
        dst = pl.multiple_of(dst0 + c * _CHUNK, _CHUNK)
        return pltpu.make_async_copy(xloc_ref.at[b, pl.ds(src, _CHUNK), :], xs_ref.at[pl.ds(dst, _CHUNK), :],
                                     sem.at[b])

    def wait_step(b, step):
        def wait(c, carry):
            run_copy(b, 0, 0, 0).wait()
            return carry

        lax.fori_loop(0, tchunks_ref[step], wait, 0)

    @pl.when(i >= 2)
    def _buffer_free():
        wait_step(buf, i - 2)

    sel, pos = _slot_positions(gates_ref[...], lvec_ref[0])
    lo = jnp.min(jnp.where(sel, pos, _BIG).T, axis=0, keepdims=True)
    hi = jnp.max(jnp.where(sel, pos, -1.0).T, axis=0, keepdims=True)
    slot = lax.broadcasted_iota(jnp.int32, (_SLOTS, _TT), 0).astype(F32)
    onehot = ((slot == lo) | (slot == hi)).astype(BF16)
    xloc_ref[buf] = jnp.dot(onehot, n2_ref[...], preferred_element_type=F32).astype(BF16)

    for e in range(N_EXPERTS):
        idx = i * N_EXPERTS + e
        src0, dst0 = lstart_ref[idx], roff_ref[idx]

        def start(c, carry, src0=src0, dst0=dst0):
            run_copy(buf, src0, dst0, c).start()
            return carry

        lax.fori_loop(0, nchunk_ref[idx], start, 0)

    @pl.when(i == last)
    def _drain():
        @pl.when(i >= 1)
        def _previous():
            wait_step(1 - buf, i - 1)

        wait_step(buf, i)

    @pl.when(i == last)
    def _zero_tails():
        zero_ref[...] = jnp.zeros_like(zero_ref)

        def tail_copy(dst0, c):
            dst = pl.multiple_of(dst0 + c * _CHUNK, _CHUNK)
            return pltpu.make_async_copy(zero_ref.at[pl.ds(0, _CHUNK), :], xs_ref.at[pl.ds(dst, _CHUNK), :], zsem)

        def tile_copy(t):
            dst = pl.multiple_of(t * _TM, _TM)
            return pltpu.make_async_copy(zero_ref, xs_ref.at[pl.ds(dst, _TM), :], zsem)

        n_tiles_total = xs_ref.shape[0] // _TM

        def tstart(t, carry):
            tile_copy(t).start()
            return carry

        def twait(t, carry):
            tile_copy(0).wait()
            return carry

        lax.fori_loop(n_active_ref[0], n_tiles_total, tstart, 0)
        lax.fori_loop(n_active_ref[0], n_tiles_total, twait, 0)

        for e in range(N_EXPERTS):
            dst0 = tail_start_ref[e]

            def zstart(c, carry, dst0=dst0):
                tail_copy(dst0, c).start()
                return carry

            def zwait(c, carry):
                tail_copy(0, 0).wait()
                return carry

            lax.fori_loop(0, tail_chunks_ref[e], zstart, 0)
            lax.fori_loop(0, tail_chunks_ref[e], zwait, 0)


def _dispatch(plan, n2, gates, n_tiles, n_sorted):
    row = lambda w: pl.BlockSpec((_TT, w), lambda i, *_: (i, 0))
    return pl.pallas_call(
        _dispatch_kernel,
        grid_spec=pltpu.PrefetchScalarGridSpec(
            num_scalar_prefetch=7,
            grid=(n_tiles,),
            in_specs=[row(D_MODEL), row(_ROUTER_LANES),
                      pl.BlockSpec((1, 1, _ROUTER_LANES), lambda i, *_: (i, 0, 0))],
            out_specs=pl.BlockSpec(memory_space=pl.ANY),
            scratch_shapes=[pltpu.VMEM((2, _SLOTS, D_MODEL), BF16), pltpu.VMEM((_TM, D_MODEL), BF16),
                            pltpu.SemaphoreType.DMA((2,)), pltpu.SemaphoreType.DMA],
        ),
        out_shape=jax.ShapeDtypeStruct((n_sorted, D_MODEL), BF16),
        compiler_params=_params(1),
        name="moe_dispatch",
    )(plan["nchunk"], plan["lstart"], plan["roff"], plan["tile_chunks"], plan["tail_start"], plan["tail_chunks"],
      plan["n_active"], n2, gates, plan["lstart_vec"])


def _experts_kernel(tile_expert_ref, tile_block_ref, n_active_ref, xs_ref, wg_ref, wu_ref, wd_ref, ys_ref,
                    wg_bf, wu_bf, wd_bf):
    g = pl.program_id(0)
    active = g < n_active_ref[0]
    new_expert = (g == 0) | (tile_expert_ref[g] != tile_expert_ref[jnp.maximum(g - 1, 0)])

    @pl.when(active & new_expert)
    def _round_weights():
        wg_bf[...] = wg_ref[0].astype(BF16)
        wu_bf[...] = wu_ref[0].astype(BF16)
        wd_bf[...] = wd_ref[0].astype(BF16)

    @pl.when(active)
    def _ffn():
        xs = xs_ref[...]
        gate = jnp.dot(xs, wg_bf[...], preferred_element_type=F32)
        up = jnp.dot(xs, wu_bf[...], preferred_element_type=F32)
        hidden = (gate * _sigmoid(gate) * up).astype(BF16)
        ys_ref[...] = jnp.dot(hidden, wd_bf[...], preferred_element_type=F32).astype(BF16)

    @pl.when(jnp.logical_not(active))
    def _unused_tile():
        ys_ref[...] = jnp.zeros_like(ys_ref)


def _experts(plan, xs, w_gate, w_up, w_down, max_tiles):
    rows_in = pl.BlockSpec((_TM, D_MODEL), lambda g, te, tb, na: (tb[g], 0))
    rows_out = pl.BlockSpec((_TM, D_MODEL), lambda g, te, tb, na: (g, 0))
    expert = lambda a: pl.BlockSpec((1,) + a.shape[1:], lambda g, te, tb, na: (te[g], 0, 0))
    return pl.pallas_call(
        _experts_kernel,
        grid_spec=pltpu.PrefetchScalarGridSpec(
            num_scalar_prefetch=3,
            grid=(max_tiles,),
            in_specs=[rows_in, expert(w_gate), expert(w_up), expert(w_down)],
            out_specs=rows_out,
            scratch_shapes=[pltpu.VMEM(w.shape[1:], BF16) for w in (w_gate, w_up, w_down)],
        ),
        out_shape=jax.ShapeDtypeStruct(xs.shape, BF16),
        compiler_params=_params(1),
        name="moe_experts",
    )(plan["tile_expert"], plan["tile_block"], plan["n_active"], xs, w_gate, w_up, w_down)


def _combine_kernel(nchunk_ref, lstart_ref, roff_ref, tchunks_ref,
                    gates_ref, lvec_ref, h1_ref, g_ref, ys_ref, o_ref, yloc_ref, sem):
    i = pl.program_id(0)
    buf = i % 2

    def run_copy(b, src0, dst0, c):
        src = pl.multiple_of(src0 + c * _CHUNK, _CHUNK)
        dst = pl.multiple_of(dst0 + c * _CHUNK, _CHUNK)
        return pltpu.make_async_copy(ys_ref.at[pl.ds(src, _CHUNK), :], yloc_ref.at[b, pl.ds(dst, _CHUNK), :],
                                     sem.at[b])

    def fetch(b, step):
        for e in range(N_EXPERTS):
            idx = step * N_EXPERTS + e
            src0, dst0 = roff_ref[idx], lstart_ref[idx]

            def start(c, carry, src0=src0, dst0=dst0):
                run_copy(b, src0, dst0, c).start()
                return carry

            lax.fori_loop(0, nchunk_ref[idx], start, 0)

    @pl.when(i == 0)
    def _first():
        yloc_ref[...] = jnp.zeros_like(yloc_ref)
        fetch(buf, i)

    @pl.when(i + 1 < pl.num_programs(0))
    def _prefetch():
        fetch(1 - buf, i + 1)

    gates = gates_ref[...]
    sel, pos = _slot_positions(gates, lvec_ref[0])
    pos_lo = jnp.where(sel, pos, _BIG)
    pos_hi = jnp.where(sel, pos, -1.0)
    lo = jnp.min(pos_lo, axis=-1, keepdims=True)
    hi = jnp.max(pos_hi, axis=-1, keepdims=True)
    w_lo = jnp.sum(jnp.where(pos_lo == lo, gates, 0.0), axis=-1, keepdims=True)
    w_hi = jnp.where(hi == lo, 0.0, jnp.sum(jnp.where(pos_hi == hi, gates, 0.0), axis=-1, keepdims=True))
    slot = lax.broadcasted_iota(jnp.int32, (_TT, _SLOTS), 1).astype(F32)
    weights = (jnp.where(slot == lo, w_lo, 0.0) + jnp.where(slot == hi, w_hi, 0.0)).astype(BF16)

    def wait(c, carry):
        run_copy(buf, 0, 0, 0).wait()
        return carry

    lax.fori_loop(0, tchunks_ref[i], wait, 0)
    moe = jnp.dot(weights, yloc_ref[buf], preferred_element_type=F32)
    o_ref[...] = _rms_norm(h1_ref[...] + moe, g_ref[...])


def _combine(plan, gates, h1, g, ys, n_tiles):
    row = lambda w: pl.BlockSpec((_TT, w), lambda i, *_: (i, 0))
    return pl.pallas_call(
        _combine_kernel,
        grid_spec=pltpu.PrefetchScalarGridSpec(
            num_scalar_prefetch=4,
            grid=(n_tiles,),
            in_specs=[row(_ROUTER_LANES), pl.BlockSpec((1, 1, _ROUTER_LANES), lambda i, *_: (i, 0, 0)),
                      row(D_MODEL), pl.BlockSpec(g.shape, lambda i, *_: (0, 0)),
                      pl.BlockSpec(memory_space=pl.ANY)],
            out_specs=row(D_MODEL),
            scratch_shapes=[pltpu.VMEM((2, _SLOTS, D_MODEL), BF16), pltpu.SemaphoreType.DMA((2,))],
        ),
        out_shape=jax.ShapeDtypeStruct(h1.shape, F32),
        compiler_params=_params(1),
        name="moe_combine",
    )(plan["nchunk"], plan["lstart"], plan["roff"], plan["tile_chunks"], gates, plan["lstart_vec"], h1, g, ys)


def _moe(n2, gates, h1, w_gate, w_up, w_down, g):
    n_rows = n2.shape[0]
    assert n_rows % _TT == 0 and _SLOTS >= 2 * _TT + N_EXPERTS * (_CHUNK - 1)
    n_tiles = n_rows // _TT
    plan, max_tiles = _moe_plan(gates, n_tiles)
    xs = _dispatch(plan, n2, gates, n_tiles, max_tiles * _TM)
    ys = _experts(plan, xs, w_gate, w_up, w_down, max_tiles)
    return _combine(plan, gates, h1, g, ys, n_tiles)


def kernel(x, meta_tokens, norm_mix_g, w_in, conv_w, conv_b, lru_w_a, lru_b_a, lru_w_x, lru_b_x, lru_lambda, attn_sink, w_attn_branch, w_rec_branch, w_out, norm_ffn_g, w_group, b_group, w_router, b_router, moe_w_gate, moe_w_up, moe_w_down, final_norm_g):
    batch, seq, _ = x.shape
    assert norm_mix_g.shape[0] == 1, "single-layer block"
    assert seq % _TQ == 0 and seq % _TC == 0
    n_rows = batch * seq
    x2 = x.reshape(n_rows, D_MODEL)
    row = lambda a: a.reshape(1, -1).astype(F32)

    w_in_bf = w_in[0].astype(BF16)
    g_mix = row(norm_mix_g[0])
    q, k, v, xr, gy, ga, gr = _in_proj(x2, g_mix, w_in_bf, 512)
    _, k_meta, v_meta, xr_meta, _, _, _ = _in_proj(meta_tokens.astype(F32), g_mix, w_in_bf, N_META)

    sink_rows = jnp.broadcast_to((attn_sink[0].astype(F32) * HEAD_DIM ** 0.5)[:, None, None],
                                 (N_HEADS, BLOCK, BLOCK)).reshape(N_HEADS * BLOCK, BLOCK)
    shape3 = lambda a: a.reshape(batch, seq, a.shape[-1])
    pad_keys = lambda a: jnp.pad(a, ((0, BLOCK - N_META), (0, 0)))
    attn = _attention(shape3(q), shape3(k), shape3(v), pad_keys(k_meta), pad_keys(v_meta), sink_rows, shape3(ga),
                      w_attn_branch[0].astype(BF16))

    h_dirs = []
    for d, reverse in enumerate((False, True)):
        wg = jnp.concatenate([lru_w_a[0, d], lru_w_x[0, d]], axis=-1).astype(BF16)
        bg_half = 0.5 * jnp.stack([lru_b_a[0, d], lru_b_x[0, d]]).astype(F32)
        h_dirs.append(_lru(shape3(xr), xr_meta, 0.5 * conv_w[0].astype(F32), 0.5 * row(conv_b[0]), wg, bg_half,
                           row(lru_lambda[0, d]), reverse))

    w_route = jnp.concatenate([w_group[0], w_router[0]], axis=1).astype(F32)
    w_route = jnp.pad(w_route, ((0, 0), (0, _ROUTER_LANES - w_route.shape[1])))
    wr_hi = w_route.astype(BF16)
    wr_lo = (w_route - wr_hi.astype(F32)).astype(BF16)
    b_route = jnp.pad(jnp.concatenate([b_group[0], b_router[0]]).astype(F32),
                      (0, _ROUTER_LANES - N_GROUPS - N_EXPERTS)).reshape(1, _ROUTER_LANES)
    h1, n2, gates = _merge(x2, h_dirs[0].reshape(n_rows, LRU_WIDTH), h_dirs[1].reshape(n_rows, LRU_WIDTH),
                           gy, attn.reshape(n_rows, D_MODEL), gr,
                           w_rec_branch[0].astype(BF16), w_out[0].astype(BF16), row(norm_ffn_g[0]),
                           wr_hi, wr_lo, b_route, 512)

    out = _moe(n2, gates, h1, moe_w_gate[0].astype(F32), moe_w_up[0].astype(F32), moe_w_down[0].astype(F32),
               row(final_norm_g))
    return out.reshape(batch, seq, D_MODEL)
```

```python
import functools
import math

import jax
import jax.numpy as jnp
from jax import lax
from jax.experimental import pallas as pl
from jax.experimental.pallas import tpu as pltpu

D_MODEL = 1024
N_META = 16
N_HEADS = 8
N_KV_HEADS = 2
HEAD_DIM = 128
Q_PER_KV = N_HEADS // N_KV_HEADS
ATTN_WIDTH = N_HEADS * HEAD_DIM
KV_WIDTH = N_KV_HEADS * HEAD_DIM
WINDOW = 128
BLOCK = 128
LRU_WIDTH = D_MODEL
LRU_BLOCKS = 8
LRU_BLOCK_DIM = LRU_WIDTH // LRU_BLOCKS
CONV_WIDTH = 4
LRU_C = 8.0
N_GROUPS = 4
EXPERTS_PER_GROUP = 4
N_EXPERTS = N_GROUPS * EXPERTS_PER_GROUP
EXPERT_FF = 512
IN_WIDTH = ATTN_WIDTH + 2 * KV_WIDTH + 2 * LRU_WIDTH + 2 * D_MODEL
EPS = 1e-6
NEG_INF = -1e30

LANES = 128
SUBLANES = 8
VMEM_LIMIT = 56 * 1024 * 1024

BF16 = jnp.bfloat16
F32 = jnp.float32


def _params(n_grid_dims):
    return pltpu.CompilerParams(
        dimension_semantics=("arbitrary",) * n_grid_dims,
        vmem_limit_bytes=VMEM_LIMIT,
    )


def _sigmoid(x):
    return 0.5 * jnp.tanh(0.5 * x) + 0.5


def _gelu_tanh(x):
    c = math.sqrt(2.0 / math.pi)
    return 0.5 * x * (1.0 + jnp.tanh(c * (x + 0.044715 * (x * x * x))))


def _rms_norm(xf, g):
    ms = jnp.mean(xf * xf, axis=-1, keepdims=True)
    return xf * lax.rsqrt(ms + EPS) * g


_IN_CHUNK = 512


def _in_proj_kernel(x_ref, g_ref, w_ref, q_ref, k_ref, v_ref, xr_ref, gy_ref, ga_ref, gr_ref):
    n = _rms_norm(x_ref[...], g_ref[...]).astype(BF16)

    def proj(c0, width):
        return jnp.dot(n, w_ref[:, c0:c0 + width], preferred_element_type=F32)

    c = 0
    for j in range(ATTN_WIDTH // _IN_CHUNK):
        q_ref[:, j * _IN_CHUNK:(j + 1) * _IN_CHUNK] = proj(c, _IN_CHUNK).astype(BF16)
        c += _IN_CHUNK
    kv = proj(c, 2 * KV_WIDTH)
    k_ref[...] = kv[:, :KV_WIDTH].astype(BF16)
    v_ref[...] = kv[:, KV_WIDTH:].astype(BF16)
    c += 2 * KV_WIDTH
    for j in range(LRU_WIDTH // _IN_CHUNK):
        xr_ref[:, j * _IN_CHUNK:(j + 1) * _IN_CHUNK] = proj(c, _IN_CHUNK)
        c += _IN_CHUNK
    for j in range(LRU_WIDTH // _IN_CHUNK):
        gy_ref[:, j * _IN_CHUNK:(j + 1) * _IN_CHUNK] = _gelu_tanh(proj(c, _IN_CHUNK)).astype(BF16)
        c += _IN_CHUNK
    for ref in (ga_ref, gr_ref):
        for j in range(D_MODEL // _IN_CHUNK):
            ref[:, j * _IN_CHUNK:(j + 1) * _IN_CHUNK] = _sigmoid(proj(c, _IN_CHUNK)).astype(BF16)
            c += _IN_CHUNK


def _in_proj(x2, g, w_bf, tm):
    n_rows = x2.shape[0]
    row = lambda w: pl.BlockSpec((tm, w), lambda i: (i, 0))
    full = lambda a: pl.BlockSpec(a.shape, lambda i: (0,) * a.ndim)
    out_shapes = (
        jax.ShapeDtypeStruct((n_rows, ATTN_WIDTH), BF16),
        jax.ShapeDtypeStruct((n_rows, KV_WIDTH), BF16),
        jax.ShapeDtypeStruct((n_rows, KV_WIDTH), BF16),
        jax.ShapeDtypeStruct((n_rows, LRU_WIDTH), F32),
        jax.ShapeDtypeStruct((n_rows, LRU_WIDTH), BF16),
        jax.ShapeDtypeStruct((n_rows, D_MODEL), BF16),
        jax.ShapeDtypeStruct((n_rows, D_MODEL), BF16),
    )
    return pl.pallas_call(
        _in_proj_kernel,
        grid=(n_rows // tm,),
        in_specs=[row(D_MODEL), full(g), full(w_bf)],
        out_specs=tuple(row(s.shape[1]) for s in out_shapes),
        out_shape=out_shapes,
        compiler_params=_params(1),
        name="in_proj",
    )(x2, g, w_bf)


_TQ = 512
_SUB = _TQ // BLOCK
_GROUP_ROWS = Q_PER_KV * BLOCK


_KEYS = 4 * BLOCK
_SM_ROWS = 32


def _attn_kernel(q_ref, kp_ref, kc_ref, kn_ref, vp_ref, vc_ref, vn_ref, km_ref, vm_ref,
                 sink_ref, ga_ref, w_ref, o_ref, bias_ref, attn_ref, s_ref, p_ref, m_ref):
    i = pl.program_id(1)
    n_i = pl.num_programs(1)
    scale = HEAD_DIM ** -0.5
    exp_scale = scale * math.log2(math.e)

    @pl.when((pl.program_id(0) == 0) & (i == 0))
    def _init_bias():
        r = lax.broadcasted_iota(jnp.int32, (BLOCK, BLOCK), 0)
        c = lax.broadcasted_iota(jnp.int32, (BLOCK, BLOCK), 1)
        d_prev = (r + BLOCK - c).astype(F32)
        d_cur = jnp.abs(r - c).astype(F32)
        d_next = (c + BLOCK - r).astype(F32)
        for h in range(N_HEADS):
            slope = 2.0 ** (-8.0 * (h + 1.0) / N_HEADS) / scale
            rows = slice(h * BLOCK, (h + 1) * BLOCK)
            bias_ref[rows, 0:BLOCK] = jnp.where(c >= r, -slope * d_prev, NEG_INF / scale)
            bias_ref[rows, BLOCK:2 * BLOCK] = -slope * d_cur
            bias_ref[rows, 2 * BLOCK:3 * BLOCK] = jnp.where(c <= r, -slope * d_next, NEG_INF / scale)
            bias_ref[rows, 3 * BLOCK:4 * BLOCK] = jnp.where(c < N_META, 0.0, NEG_INF / scale)

    nt = (((1,), (1,)), ((), ()))
    for j in range(_SUB):
        rows = slice(j * BLOCK, (j + 1) * BLOCK)
        q = q_ref[0, rows, :]
        if j == 0:
            k3 = [kp_ref[0], kc_ref[0, 0:2 * BLOCK, :]]
            v3 = [vp_ref[0], vc_ref[0, 0:2 * BLOCK, :]]
        elif j == _SUB - 1:
            k3 = [kc_ref[0, (j - 1) * BLOCK:(j + 1) * BLOCK, :], kn_ref[0]]
            v3 = [vc_ref[0, (j - 1) * BLOCK:(j + 1) * BLOCK, :], vn_ref[0]]
        else:
            k3 = [kc_ref[0, (j - 1) * BLOCK:(j + 2) * BLOCK, :]]
            v3 = [vc_ref[0, (j - 1) * BLOCK:(j + 2) * BLOCK, :]]
        k_cat = jnp.concatenate(k3 + [km_ref[...]], axis=0)
        v_cat = jnp.concatenate(v3 + [vm_ref[...]], axis=0)
        masked = []
        if j == 0:
            masked.append((slice(0, BLOCK), i == 0))
        if j == _SUB - 1:
            masked.append((slice(2 * BLOCK, 3 * BLOCK), i == n_i - 1))
        for g in range(N_KV_HEADS):
            cols = slice(g * HEAD_DIM, (g + 1) * HEAD_DIM)
            row0 = g * _GROUP_ROWS
            qg = jnp.concatenate(
                [q[:, (g * Q_PER_KV + h) * HEAD_DIM:(g * Q_PER_KV + h + 1) * HEAD_DIM] for h in range(Q_PER_KV)],
                axis=0)
            pair = j * N_KV_HEADS + g
            s_ref[pair] = lax.dot_general(qg, k_cat[:, cols], nt, preferred_element_type=F32)
            for mask_cols, mask_on in masked:
                s_ref[pair, :, mask_cols] = jnp.where(mask_on, NEG_INF / scale, s_ref[pair, :, mask_cols])

            chunks = [(slice(c * _SM_ROWS, (c + 1) * _SM_ROWS), slice(row0 + c * _SM_ROWS, row0 + (c + 1) * _SM_ROWS))
                      for c in range(_GROUP_ROWS // _SM_ROWS)]
            wide = lambda col: jnp.broadcast_to(col, (_SM_ROWS, BLOCK))
            tiled = lambda stat: jnp.concatenate([stat] * (_KEYS // BLOCK), axis=1)
            for r, rb in chunks:
                z = s_ref[pair, r, :] + bias_ref[rb, :]
                m_ref[pair, r, :] = jnp.maximum(wide(jnp.max(z, axis=-1, keepdims=True)), sink_ref[rb, :])
            for r, rb in chunks:
                m = m_ref[pair, r, :]
                p = jnp.exp2((s_ref[pair, r, :] + bias_ref[rb, :] - tiled(m)) * exp_scale)
                denom = wide(jnp.sum(p, axis=-1, keepdims=True)) + jnp.exp2((sink_ref[rb, :] - m) * exp_scale)
                p_ref[pair, r, :] = p.astype(BF16)
                m_ref[pair, r, :] = 1.0 / denom
            o = jnp.dot(p_ref[pair], v_cat[:, cols], preferred_element_type=F32)
            o = (o * m_ref[pair]).astype(BF16)
            for h in range(Q_PER_KV):
                head = g * Q_PER_KV + h
                attn_ref[rows, head * HEAD_DIM:(head + 1) * HEAD_DIM] = o[h * BLOCK:(h + 1) * BLOCK, :]

    proj = jnp.dot(attn_ref[...], w_ref[...], preferred_element_type=F32)
    o_ref[0] = (ga_ref[0].astype(F32) * proj).astype(BF16)


def _attention(q, k, v, k_meta, v_meta, sink_rows, g_attn, w_bf):
    batch, seq, _ = q.shape
    n_blk = seq // BLOCK
    main = lambda w: pl.BlockSpec((1, _TQ, w), lambda b, i: (b, i, 0))
    prev = pl.BlockSpec((1, BLOCK, KV_WIDTH), lambda b, i: (b, jnp.maximum(i * _SUB - 1, 0), 0))
    nxt = pl.BlockSpec((1, BLOCK, KV_WIDTH), lambda b, i: (b, jnp.minimum((i + 1) * _SUB, n_blk - 1), 0))
    full = lambda a: pl.BlockSpec(a.shape, lambda b, i: (0,) * a.ndim)
    return pl.pallas_call(
        _attn_kernel,
        grid=(batch, seq // _TQ),
        in_specs=[main(ATTN_WIDTH), prev, main(KV_WIDTH), nxt, prev, main(KV_WIDTH), nxt,
                  full(k_meta), full(v_meta), full(sink_rows), main(D_MODEL), full(w_bf)],
        out_specs=main(D_MODEL),
        out_shape=jax.ShapeDtypeStruct((batch, seq, D_MODEL), BF16),
        scratch_shapes=[pltpu.VMEM((N_HEADS * BLOCK, _KEYS), F32),
                        pltpu.VMEM((_TQ, ATTN_WIDTH), BF16),
                        pltpu.VMEM((_SUB * N_KV_HEADS, _GROUP_ROWS, _KEYS), F32),
                        pltpu.VMEM((_SUB * N_KV_HEADS, _GROUP_ROWS, _KEYS), BF16),
                        pltpu.VMEM((_SUB * N_KV_HEADS, _GROUP_ROWS, BLOCK), F32)],
        compiler_params=_params(2),
        name="attention",
    )(q, k, k, k, v, v, v, k_meta, v_meta, sink_rows, g_attn, w_bf)


_TC = 512
_HALO = SUBLANES


def _interleave_in(dst_ref, src, n_rows):
    seg = n_rows // SUBLANES
    for n in range(LRU_BLOCKS):
        for s in range(SUBLANES):
            dst_ref[n, pl.ds(s, seg, stride=SUBLANES), :] = src(s * seg, seg, n)


def _interleave_out(write, src_ref, n_rows):
    seg = n_rows // SUBLANES
    for n in range(LRU_BLOCKS):
        for s in range(SUBLANES):
            write(s * seg, seg, n, src_ref[n, pl.ds(s, seg, stride=SUBLANES), :])


def _lru_gates(n_rows, prev2, prev1, next0, x_ref, cw_ref, cb_ref, wg_ref, bg_ref, lam_ref, a_ref, u_ref):
    seg = n_rows // SUBLANES
    sub = lax.broadcasted_iota(jnp.int32, (SUBLANES, LRU_BLOCK_DIM), 0)
    lam = lam_ref[...]
    decay_scale = (-0.5 * LRU_C * math.log2(math.e)) * (
        jnp.maximum(-lam, 0.0) + jnp.log(1.0 + jnp.exp(-jnp.abs(lam))))
    for n in range(LRU_BLOCKS):
        cols = slice(n * LRU_BLOCK_DIM, (n + 1) * LRU_BLOCK_DIM)
        x = x_ref[n, 0:n_rows, :]
        group = lambda j: x[j * SUBLANES:(j + 1) * SUBLANES, :]
        e0 = jnp.where(sub == 0, prev2(n), pltpu.roll(group(seg - 2), 1, axis=0))
        e1 = jnp.where(sub == 0, prev1(n), pltpu.roll(group(seg - 1), 1, axis=0))
        e_next = jnp.where(sub == SUBLANES - 1, next0(n), pltpu.roll(group(0), SUBLANES - 1, axis=0))
        ext = jnp.concatenate([e0, e1, x, e_next], axis=0)
        xh = cb_ref[:, cols] + sum(
            cw_ref[t:t + 1, cols] * ext[t * SUBLANES:t * SUBLANES + n_rows, :] for t in range(CONV_WIDTH))
        pre = jnp.dot(xh.astype(BF16), wg_ref[n], preferred_element_type=F32)
        t_a = jnp.tanh(pre[:, :LRU_BLOCK_DIM] + bg_ref[0:1, cols])
        t_x = jnp.tanh(pre[:, LRU_BLOCK_DIM:] + bg_ref[1:2, cols])
        scale = decay_scale[:, cols]
        a = jnp.exp2(t_a * scale + scale)
        y = 1.0 - a * a
        a_ref[n, 0:n_rows, :] = a
        u_ref[n, 0:n_rows, :] = (y * lax.rsqrt(jnp.maximum(y, 1e-30))) * ((t_x + 1.0) * xh)


def _lru_scan(n_rows, reverse, carry_in, a_ref, u_ref, h_ref):
    seg = n_rows // SUBLANES
    unroll = min(16, seg)
    sub = lax.broadcasted_iota(jnp.int32, (SUBLANES, LRU_BLOCK_DIM), 0)
    blocks = range(LRU_BLOCKS)

    def rows(jj):
        j = (seg - 1 - jj) if reverse else jj
        return pl.ds(pl.multiple_of(j * SUBLANES, SUBLANES), SUBLANES)

    def local(jj, state):
        hs, ps = state
        r = rows(jj)
        a = [a_ref[n, r, :] for n in blocks]
        return (tuple(a[n] * hs[n] + u_ref[n, r, :] for n in blocks), tuple(a[n] * ps[n] for n in blocks))

    zeros = tuple(jnp.zeros((SUBLANES, LRU_BLOCK_DIM), F32) for _ in blocks)
    ones = tuple(jnp.ones((SUBLANES, LRU_BLOCK_DIM), F32) for _ in blocks)
    h_end, p_end = lax.fori_loop(0, seg, local, (zeros, ones), unroll=unroll)

    seg_in, carry_out = [], []
    for n in blocks:
        p, h = p_end[n], h_end[n]
        for d in (1, 2, 4):
            shift = SUBLANES - d if reverse else d
            ok = (sub < SUBLANES - d) if reverse else (sub >= d)
            h = h + p * jnp.where(ok, pltpu.roll(h, shift, axis=0), 0.0)
            p = p * jnp.where(ok, pltpu.roll(p, shift, axis=0), 1.0)
        seg_out = h + p * carry_in[n]
        first, last = (SUBLANES - 1, 0) if reverse else (0, SUBLANES - 1)
        shift = SUBLANES - 1 if reverse else 1
        seg_in.append(jnp.where(sub == first, carry_in[n], pltpu.roll(seg_out, shift, axis=0)))
        carry_out.append(seg_out[last:last + 1, :])

    if h_ref is not None:
        def final(jj, hs):
            r = rows(jj)
            new = tuple(a_ref[n, r, :] * hs[n] + u_ref[n, r, :] for n in blocks)
            for n in blocks:
                h_ref[n, r, :] = new[n]
            return new

        lax.fori_loop(0, seg, final, tuple(seg_in), unroll=unroll)
    return carry_out


def _lru_kernel(reverse, xr_ref, xp_ref, xn_ref, xm_ref, cw_ref, cb_ref, wg_ref, bg_ref, lam_ref,
                h_ref, carry_ref, x_scr, a_scr, u_scr, h_scr):
    step = pl.program_id(1)
    n_steps = pl.num_programs(1)
    t = (n_steps - 1 - step) if reverse else step
    args = (x_scr, cw_ref, cb_ref, wg_ref, bg_ref, lam_ref, a_scr, u_scr)
    lanes = lambda n: slice(n * LRU_BLOCK_DIM, (n + 1) * LRU_BLOCK_DIM)
    zero_row = lambda n: jnp.zeros((1, LRU_BLOCK_DIM), F32)

    if reverse:
        @pl.when(step == 0)
        def _zero_state():
            carry_ref[...] = jnp.zeros_like(carry_ref)
    else:
        @pl.when(step == 0)
        def _meta_state():
            _interleave_in(x_scr, lambda r0, nr, n: xm_ref[r0:r0 + nr, lanes(n)], N_META)
            _lru_gates(N_META, zero_row, zero_row, lambda n: xr_ref[0, 0:1, lanes(n)], *args)
            state = _lru_scan(N_META, False, [zero_row(n) for n in range(LRU_BLOCKS)], a_scr, u_scr, None)
            for n in range(LRU_BLOCKS):
                carry_ref[0:1, lanes(n)] = state[n]

    def before(row):
        return lambda n: jnp.where(t == 0, xm_ref[N_META - _HALO + row:N_META - _HALO + row + 1, lanes(n)],
                                   xp_ref[0, row:row + 1, lanes(n)])

    after = lambda n: jnp.where(t == n_steps - 1, 0.0, xn_ref[0, 0:1, lanes(n)])
    _interleave_in(x_scr, lambda r0, nr, n: xr_ref[0, r0:r0 + nr, lanes(n)], _TC)
    _lru_gates(_TC, before(_HALO - 2), before(_HALO - 1), after, *args)
    state = _lru_scan(_TC, reverse, [carry_ref[0:1, lanes(n)] for n in range(LRU_BLOCKS)], a_scr, u_scr, h_scr)
    for n in range(LRU_BLOCKS):
        carry_ref[0:1, lanes(n)] = state[n]

    def write(r0, nr, n, rows):
        h_ref[0, r0:r0 + nr, lanes(n)] = rows.astype(h_ref.dtype)

    _interleave_out(write, h_scr, _TC)


def _lru(xr, xr_meta, conv_w, conv_b, wg_bf, bg, lam, reverse):
    batch, seq, _ = xr.shape
    n_steps = seq // _TC
    n_halo = seq // _HALO
    per_tile = _TC // _HALO
    tile = (lambda s: n_steps - 1 - s) if reverse else (lambda s: s)
    main = pl.BlockSpec((1, _TC, LRU_WIDTH), lambda b, s: (b, tile(s), 0))
    before = pl.BlockSpec((1, _HALO, LRU_WIDTH), lambda b, s: (b, jnp.maximum(tile(s) * per_tile - 1, 0), 0))
    after = pl.BlockSpec((1, _HALO, LRU_WIDTH),
                         lambda b, s: (b, jnp.minimum((tile(s) + 1) * per_tile, n_halo - 1), 0))
    full = lambda a: pl.BlockSpec(a.shape, lambda b, s: (0,) * a.ndim)
    return pl.pallas_call(
        functools.partial(_lru_kernel, reverse),
        grid=(batch, n_steps),
        in_specs=[main, before, after, full(xr_meta), full(conv_w), full(conv_b), full(wg_bf), full(bg),
                  full(lam)],
        out_specs=main,
        out_shape=jax.ShapeDtypeStruct((batch, seq, LRU_WIDTH), BF16),
        scratch_shapes=[pltpu.VMEM((SUBLANES, LRU_WIDTH), F32)]
        + [pltpu.VMEM((LRU_BLOCKS, _TC, LRU_BLOCK_DIM), F32)] * 4,
        compiler_params=_params(2),
        name="lru_bwd" if reverse else "lru_fwd",
    )(xr, xr, xr, xr_meta, conv_w, conv_b, wg_bf, bg, lam)


_ROUTER_LANES = LANES
_MERGE_ROWS = 256


def _split_dot(a, b_hi, b_lo):
    a_hi = a.astype(BF16)
    a_lo = (a - a_hi.astype(F32)).astype(BF16)
    return (jnp.dot(a_hi, b_hi, preferred_element_type=F32)
            + (jnp.dot(a_lo, b_hi, preferred_element_type=F32) + jnp.dot(a_hi, b_lo, preferred_element_type=F32)))


def _merge_kernel(x_ref, hf_ref, hb_ref, gy_ref, ga_ref, gr_ref, wrec_ref, wout_ref, g_ref,
                  wr_hi_ref, wr_lo_ref, br_ref, h1_ref, n2_ref, gates_ref):
    subs = [slice(r0, r0 + _MERGE_ROWS) for r0 in range(0, x_ref.shape[0], _MERGE_ROWS)]
    rec, n2 = {}, {}
    for k, rows in enumerate(subs):
        rec_in = (hf_ref[rows, :] + hb_ref[rows, :]) * gy_ref[rows, :]
        rec[k] = jnp.dot(rec_in, wrec_ref[...], preferred_element_type=F32)
    for k, rows in enumerate(subs):
        mix = (ga_ref[rows, :].astype(F32) + gr_ref[rows, :].astype(F32) * rec[k]).astype(BF16)
        h1 = x_ref[rows, :] + jnp.dot(mix, wout_ref[...], preferred_element_type=F32)
        h1_ref[rows, :] = h1
        n2[k] = _rms_norm(h1, g_ref[...])
        n2_ref[rows, :] = n2[k].astype(BF16)
    for k, rows in enumerate(subs):
        _route(rows, n2[k], wr_hi_ref, wr_lo_ref, br_ref, gates_ref)


def _route(rows, n2, wr_hi_ref, wr_lo_ref, br_ref, gates_ref):
    logits = _split_dot(n2, wr_hi_ref[...], wr_lo_ref[...]) + br_ref[...]
    lane_i = lax.broadcasted_iota(jnp.int32, logits.shape, 1)
    lane = lane_i.astype(F32)
    first = lambda mask: jnp.min(jnp.where(mask, lane, float(_ROUTER_LANES)), axis=-1, keepdims=True)
    lg = jnp.where(lane < N_GROUPS, logits, -jnp.inf)
    g_max = jnp.max(lg, axis=-1, keepdims=True)
    g_top_p = 1.0 / jnp.sum(jnp.exp(lg - g_max), axis=-1, keepdims=True)
    g_idx = first(lg == g_max)
    e = lane_i - N_GROUPS
    e_group = jnp.right_shift(e, int(math.log2(EXPERTS_PER_GROUP))).astype(F32)
    in_group = (e >= 0) & (e < N_EXPERTS) & (e_group == g_idx)
    le = jnp.where(in_group, logits, -jnp.inf)
    m1 = jnp.max(le, axis=-1, keepdims=True)
    i1 = first(le == m1)
    le2 = jnp.where(lane == i1, -jnp.inf, le)
    m2 = jnp.max(le2, axis=-1, keepdims=True)
    i2 = first(le2 == m2)
    e2 = jnp.exp(m2 - m1)
    w1 = g_top_p / (1.0 + e2)
    w2 = g_top_p * e2 / (1.0 + e2)
    gates_ref[rows, :] = jnp.where(lane == i1, w1, 0.0) + jnp.where(lane == i2, w2, 0.0)


def _merge(x2, hf, hb, gy, ga, gr, wrec_bf, wout_bf, g, wr_hi, wr_lo, br, tm):
    n_rows = x2.shape[0]
    row = lambda w: pl.BlockSpec((tm, w), lambda i: (i, 0))
    full = lambda a: pl.BlockSpec(a.shape, lambda i: (0,) * a.ndim)
    return pl.pallas_call(
        _merge_kernel,
        grid=(n_rows // tm,),
        in_specs=[row(D_MODEL)] * 6 + [full(wrec_bf), full(wout_bf), full(g), full(wr_hi), full(wr_lo), full(br)],
        out_specs=(row(D_MODEL), row(D_MODEL), row(_ROUTER_LANES)),
        out_shape=(jax.ShapeDtypeStruct((n_rows, D_MODEL), F32),
                   jax.ShapeDtypeStruct((n_rows, D_MODEL), BF16),
                   jax.ShapeDtypeStruct((n_rows, _ROUTER_LANES), F32)),
        compiler_params=_params(1),
        name="merge",
    )(x2, hf, hb, gy, ga, gr, wrec_bf, wout_bf, g, wr_hi, wr_lo, br)


_TT = 512
_CHUNK = 16
_TM = 512
_SLOTS = 1280
_SLOT_CHUNK = 256
_BIG = 1.0e6


def _moe_plan(gates, n_tiles):
    i32 = jnp.int32
    sel = gates[:, N_GROUPS:N_GROUPS + N_EXPERTS] > 0.0
    cnt = jnp.sum(sel.reshape(n_tiles, _TT, N_EXPERTS), axis=1, dtype=i32)
    padc = (cnt + _CHUNK - 1) // _CHUNK * _CHUNK
    lstart = jnp.cumsum(padc, axis=1) - padc
    tot = jnp.sum(padc, axis=0)
    ntile = (tot + _TM - 1) // _TM
    tile_end = jnp.cumsum(ntile)
    base = (tile_end - ntile) * _TM
    roff = base[None, :] + jnp.cumsum(padc, axis=0) - padc
    n_active = tile_end[-1]
    max_tiles = (2 * n_tiles * _TT + n_tiles * N_EXPERTS * (_CHUNK - 1)) // _TM + N_EXPERTS
    g = jnp.minimum(jnp.arange(max_tiles, dtype=i32), n_active - 1)
    tile_expert = jnp.sum(g[:, None] >= tile_end[None, :], axis=1, dtype=i32)
    lstart_vec = jnp.zeros((n_tiles, 1, _ROUTER_LANES), F32).at[:, 0, N_GROUPS:N_GROUPS + N_EXPERTS].set(
        lstart.astype(F32))
    plan = dict(
        nchunk=(padc // _CHUNK).reshape(-1), lstart=lstart.reshape(-1), roff=roff.reshape(-1),
        tile_chunks=jnp.sum(padc // _CHUNK, axis=1, dtype=i32),
        tail_start=base + tot, tail_chunks=(ntile * _TM - tot) // _CHUNK,
        tile_expert=tile_expert, tile_block=g, n_active=n_active.reshape(1), lstart_vec=lstart_vec)
    return plan, max_tiles


def _slot_positions(gates, lstart_vec):
    sel = gates > 0.0
    r = lax.broadcasted_iota(jnp.int32, (_TT, _TT), 0)
    c = lax.broadcasted_iota(jnp.int32, (_TT, _TT), 1)
    before = (c < r).astype(BF16)
    rank = jnp.dot(before, sel.astype(BF16), preferred_element_type=F32)
    return sel, rank + lstart_vec


def _dispatch_kernel(nchunk_ref, lstart_ref, roff_ref, tchunks_ref, tail_start_ref, tail_chunks_ref, n_active_ref,
                     n2_ref, gates_ref, lvec_ref, xs_ref, xloc_ref, zero_ref, sem, zsem):
    i = pl.program_id(0)
    last = pl.num_programs(0) - 1
    buf = i % 2

    def run_copy(b, src0, dst0, c):
        src = pl.multiple_of(src0 + c * _CHUNK, _CHUNK)
        dst = pl.multiple_of(dst0 + c * _CHUNK, _CHUNK)
        return pltpu.make_async_copy(xloc_ref.at[b, pl.ds(src, _CHUNK), :], xs_ref.at[pl.ds(dst, _CHUNK), :],
                                     sem.at[b])

    def wait_step(b, step):
        def wait(c, carry):
            run_copy(b, 0, 0, 0).wait()
            return carry

        lax.fori_loop(0, tchunks_ref[step], wait, 0)

    @pl.when(i >= 2)
    def _buffer_free():
        wait_step(buf, i - 2)

    sel, pos = _slot_positions(gates_ref[...], lvec_ref[0])
    lo = jnp.min(jnp.where(sel, pos, _BIG).T, axis=0, keepdims=True)
    hi = jnp.max(jnp.where(sel, pos, -1.0).T, axis=0, keepdims=True)
    n2 = n2_ref[...]
    for s0 in range(0, _SLOTS, _SLOT_CHUNK):
        slot = (lax.broadcasted_iota(jnp.int32, (_SLOT_CHUNK, _TT), 0) + s0).astype(F32)
        onehot = ((slot == lo) | (slot == hi)).astype(BF16)
        xloc_ref[buf, s0:s0 + _SLOT_CHUNK, :] = jnp.dot(onehot, n2, preferred_element_type=F32).astype(BF16)

    for e in range(N_EXPERTS):
        idx = i * N_EXPERTS + e
        src0, dst0 = lstart_ref[idx], roff_ref[idx]

        def start(c, carry, src0=src0, dst0=dst0):
            run_copy(buf, src0, dst0, c).start()
            return carry

        lax.fori_loop(0, nchunk_ref[idx], start, 0)

    @pl.when(i == last)
    def _drain():
        @pl.when(i >= 1)
        def _previous():
            wait_step(1 - buf, i - 1)

        wait_step(buf, i)

    @pl.when(i == last)
    def _zero_tails():
        zero_ref[...] = jnp.zeros_like(zero_ref)

        def tail_copy(dst0, c):
            dst = pl.multiple_of(dst0 + c * _CHUNK, _CHUNK)
            return pltpu.make_async_copy(zero_ref.at[pl.ds(0, _CHUNK), :], xs_ref.at[pl.ds(dst, _CHUNK), :], zsem)

        def tile_copy(t):
            dst = pl.multiple_of(t * _TM, _TM)
            return pltpu.make_async_copy(zero_ref, xs_ref.at[pl.ds(dst, _TM), :], zsem)

        n_tiles_total = xs_ref.shape[0] // _TM

        def tstart(t, carry):
            tile_copy(t).start()
            return carry

        def twait(t, carry):
            tile_copy(0).wait()
            return carry

        lax.fori_loop(n_active_ref[0], n_tiles_total, tstart, 0)
        lax.fori_loop(n_active_ref[0], n_tiles_total, twait, 0)

        for e in range(N_EXPERTS):
            dst0 = tail_start_ref[e]

            def zstart(c, carry, dst0=dst0):
                tail_copy(dst0, c).start()
                return carry

            def zwait(c, carry):
                tail_copy(0, 0).wait()
                return carry

            lax.fori_loop(0, tail_chunks_ref[e], zstart, 0)
            lax.fori_loop(0, tail_chunks_ref[e], zwait, 0)


def _dispatch(plan, n2, gates, n_tiles, n_sorted):
    row = lambda w: pl.BlockSpec((_TT, w), lambda i, *_: (i, 0))
    return pl.pallas_call(
        _dispatch_kernel,
        grid_spec=pltpu.PrefetchScalarGridSpec(
            num_scalar_prefetch=7,
            grid=(n_tiles,),
            in_specs=[row(D_MODEL), row(_ROUTER_LANES),
                      pl.BlockSpec((1, 1, _ROUTER_LANES), lambda i, *_: (i, 0, 0))],
            out_specs=pl.BlockSpec(memory_space=pl.ANY),
            scratch_shapes=[pltpu.VMEM((2, _SLOTS, D_MODEL), BF16), pltpu.VMEM((_TM, D_MODEL), BF16),
                            pltpu.SemaphoreType.DMA((2,)), pltpu.SemaphoreType.DMA],
        ),
        out_shape=jax.ShapeDtypeStruct((n_sorted, D_MODEL), BF16),
        compiler_params=_params(1),
        name="moe_dispatch",
    )(plan["nchunk"], plan["lstart"], plan["roff"], plan["tile_chunks"], plan["tail_start"], plan["tail_chunks"],
      plan["n_active"], n2, gates, plan["lstart_vec"])


def _experts_kernel(tile_expert_ref, tile_block_ref, n_active_ref, xs_ref, wg_ref, wu_ref, wd_ref, ys_ref,
                    wg_bf, wu_bf, wd_bf):
    g = pl.program_id(0)
    active = g < n_active_ref[0]
    new_expert = (g == 0) | (tile_expert_ref[g] != tile_expert_ref[jnp.maximum(g - 1, 0)])

    @pl.when(active & new_expert)
    def _round_weights():
        wg_bf[...] = wg_ref[0].astype(BF16)
        wu_bf[...] = wu_ref[0].astype(BF16)
        wd_bf[...] = wd_ref[0].astype(BF16)

    @pl.when(active)
    def _ffn():
        xs = xs_ref[...]
        gate = jnp.dot(xs, wg_bf[...], preferred_element_type=F32)
        up = jnp.dot(xs, wu_bf[...], preferred_element_type=F32)
        hidden = (gate * _sigmoid(gate) * up).astype(BF16)
        ys_ref[...] = jnp.dot(hidden, wd_bf[...], preferred_element_type=F32).astype(BF16)

    @pl.when(jnp.logical_not(active))
    def _unused_tile():
        ys_ref[...] = jnp.zeros_like(ys_ref)


def _experts(plan, xs, w_gate, w_up, w_down, max_tiles):
    rows_in = pl.BlockSpec((_TM, D_MODEL), lambda g, te, tb, na: (tb[g], 0))
    rows_out = pl.BlockSpec((_TM, D_MODEL), lambda g, te, tb, na: (g, 0))
    expert = lambda a: pl.BlockSpec((1,) + a.shape[1:], lambda g, te, tb, na: (te[g], 0, 0))
    return pl.pallas_call(
        _experts_kernel,
        grid_spec=pltpu.PrefetchScalarGridSpec(
            num_scalar_prefetch=3,
            grid=(max_tiles,),
            in_specs=[rows_in, expert(w_gate), expert(w_up), expert(w_down)],
            out_specs=rows_out,
            scratch_shapes=[pltpu.VMEM(w.shape[1:], BF16) for w in (w_gate, w_up, w_down)],
        ),
        out_shape=jax.ShapeDtypeStruct(xs.shape, BF16),
        compiler_params=_params(1),
        name="moe_experts",
    )(plan["tile_expert"], plan["tile_block"], plan["n_active"], xs, w_gate, w_up, w_down)


def _combine_kernel(nchunk_ref, lstart_ref, roff_ref, tchunks_ref,
                    gates_ref, lvec_ref, h1_ref, g_ref, ys_ref, o_ref, yloc_ref, sem):
    i = pl.program_id(0)
    buf = i % 2

    def run_copy(b, src0, dst0, c):
        src = pl.multiple_of(src0 + c * _CHUNK, _CHUNK)
        dst = pl.multiple_of(dst0 + c * _CHUNK, _CHUNK)
        return pltpu.make_async_copy(ys_ref.at[pl.ds(src, _CHUNK), :], yloc_ref.at[b, pl.ds(dst, _CHUNK), :],
                                     sem.at[b])

    def fetch(b, step):
        for e in range(N_EXPERTS):
            idx = step * N_EXPERTS + e
            src0, dst0 = roff_ref[idx], lstart_ref[idx]

            def start(c, carry, src0=src0, dst0=dst0):
                run_copy(b, src0, dst0, c).start()
                return carry

            lax.fori_loop(0, nchunk_ref[idx], start, 0)

    @pl.when(i == 0)
    def _first():
        yloc_ref[...] = jnp.zeros_like(yloc_ref)
        fetch(buf, i)

    @pl.when(i + 1 < pl.num_programs(0))
    def _prefetch():
        fetch(1 - buf, i + 1)

    def wait(c, carry):
        run_copy(buf, 0, 0, 0).wait()
        return carry

    lax.fori_loop(0, tchunks_ref[i], wait, 0)

    gates = gates_ref[...]
    sel, pos = _slot_positions(gates, lvec_ref[0])
    pos_lo = jnp.where(sel, pos, _BIG)
    pos_hi = jnp.where(sel, pos, -1.0)
    lo = jnp.min(pos_lo, axis=-1, keepdims=True)
    hi = jnp.max(pos_hi, axis=-1, keepdims=True)
    w_lo = jnp.sum(jnp.where(pos_lo == lo, gates, 0.0), axis=-1, keepdims=True)
    w_hi = jnp.where(hi == lo, 0.0, jnp.sum(jnp.where(pos_hi == hi, gates, 0.0), axis=-1, keepdims=True))
    moe = None
    for s0 in range(0, _SLOTS, _SLOT_CHUNK):
        slot = (lax.broadcasted_iota(jnp.int32, (_TT, _SLOT_CHUNK), 1) + s0).astype(F32)
        weights = (jnp.where(slot == lo, w_lo, 0.0) + jnp.where(slot == hi, w_hi, 0.0)).astype(BF16)
        part = jnp.dot(weights, yloc_ref[buf, s0:s0 + _SLOT_CHUNK, :], preferred_element_type=F32)
        moe = part if moe is None else moe + part
    o_ref[...] = _rms_norm(h1_ref[...] + moe, g_ref[...])


def _combine(plan, gates, h1, g, ys, n_tiles):
    row = lambda w: pl.BlockSpec((_TT, w), lambda i, *_: (i, 0))
    return pl.pallas_call(
        _combine_kernel,
        grid_spec=pltpu.PrefetchScalarGridSpec(
            num_scalar_prefetch=4,
            grid=(n_tiles,),
            in_specs=[row(_ROUTER_LANES), pl.BlockSpec((1, 1, _ROUTER_LANES), lambda i, *_: (i, 0, 0)),
                      row(D_MODEL), pl.BlockSpec(g.shape, lambda i, *_: (0, 0)),
                      pl.BlockSpec(memory_space=pl.ANY)],
            out_specs=row(D_MODEL),
            scratch_shapes=[pltpu.VMEM((2, _SLOTS, D_MODEL), BF16), pltpu.SemaphoreType.DMA((2,))],
        ),
        out_shape=jax.ShapeDtypeStruct(h1.shape, F32),
        compiler_params=_params(1),
        name="moe_combine",
    )(plan["nchunk"], plan["lstart"], plan["roff"], plan["tile_chunks"], gates, plan["lstart_vec"], h1, g, ys)


def _moe(n2, gates, h1, w_gate, w_up, w_down, g):
    n_rows = n2.shape[0]
    assert n_rows % _TT == 0 and _SLOTS >= 2 * _TT + N_EXPERTS * (_CHUNK - 1)
    n_tiles = n_rows // _TT
    plan, max_tiles = _moe_plan(gates, n_tiles)
    xs = _dispatch(plan, n2, gates, n_tiles, max_tiles * _TM)
    ys = _experts(plan, xs, w_gate, w_up, w_down, max_tiles)
    return _combine(plan, gates, h1, g, ys, n_tiles)


def kernel(x, meta_tokens, norm_mix_g, w_in, conv_w, conv_b, lru_w_a, lru_b_a, lru_w_x, lru_b_x, lru_lambda, attn_sink, w_attn_branch, w_rec_branch, w_out, norm_ffn_g, w_group, b_group, w_router, b_router, moe_w_gate, moe_w_up, moe_w_down, final_norm_g):
    batch, seq, _ = x.shape
    assert norm_mix_g.shape[0] == 1, "single-layer block"
    assert seq % _TQ == 0 and seq % _TC == 0
    n_rows = batch * seq
    x2 = x.reshape(n_rows, D_MODEL)
    row = lambda a: a.reshape(1, -1).astype(F32)

    w_in_bf = w_in[0].astype(BF16)
    g_mix = row(norm_mix_g[0])
    q, k, v, xr, gy, ga, gr = _in_proj(x2, g_mix, w_in_bf, 512)
    _, k_meta, v_meta, xr_meta, _, _, _ = _in_proj(meta_tokens.astype(F32), g_mix, w_in_bf, N_META)

    sink_rows = jnp.broadcast_to((attn_sink[0].astype(F32) * HEAD_DIM ** 0.5)[:, None, None],
                                 (N_HEADS, BLOCK, BLOCK)).reshape(N_HEADS * BLOCK, BLOCK)
    shape3 = lambda a: a.reshape(batch, seq, a.shape[-1])
    pad_keys = lambda a: jnp.pad(a, ((0, BLOCK - N_META), (0, 0)))
    attn = _attention(shape3(q), shape3(k), shape3(v), pad_keys(k_meta), pad_keys(v_meta), sink_rows, shape3(ga),
                      w_attn_branch[0].astype(BF16))

    h_dirs = []
    for d, reverse in enumerate((False, True)):
        wg = jnp.concatenate([lru_w_a[0, d], lru_w_x[0, d]], axis=-1).astype(BF16)
        bg_half = 0.5 * jnp.stack([lru_b_a[0, d], lru_b_x[0, d]]).astype(F32)
        h_dirs.append(_lru(shape3(xr), xr_meta, 0.5 * conv_w[0].astype(F32), 0.5 * row(conv_b[0]), wg, bg_half,
                           row(lru_lambda[0, d]), reverse))

    w_route = jnp.concatenate([w_group[0], w_router[0]], axis=1).astype(F32)
    w_route = jnp.pad(w_route, ((0, 0), (0, _ROUTER_LANES - w_route.shape[1])))
    wr_hi = w_route.astype(BF16)
    wr_lo = (w_route - wr_hi.astype(F32)).astype(BF16)
    b_route = jnp.pad(jnp.concatenate([b_group[0], b_router[0]]).astype(F32),
                      (0, _ROUTER_LANES - N_GROUPS - N_EXPERTS)).reshape(1, _ROUTER_LANES)
    h1, n2, gates = _merge(x2, h_dirs[0].reshape(n_rows, LRU_WIDTH), h_dirs[1].reshape(n_rows, LRU_WIDTH),
                           gy, attn.reshape(n_rows, D_MODEL), gr,
                           w_rec_branch[0].astype(BF16), w_out[0].astype(BF16), row(norm_ffn_g[0]),
                           wr_hi, wr_lo, b_route, 512)

    out = _moe(n2, gates, h1, moe_w_gate[0].astype(F32), moe_w_up[0].astype(F32), moe_w_down[0].astype(F32),
               row(final_norm_g))
    return out.reshape(batch, seq, D_MODEL)
```

```python
import functools
import math

import jax
import jax.numpy as jnp
from jax import lax
from jax.experimental import pallas as pl
from jax.experimental.pallas import tpu as pltpu

D_MODEL = 1024
N_META = 16
N_HEADS = 8
N_KV_HEADS = 2
HEAD_DIM = 128
Q_PER_KV = N_HEADS // N_KV_HEADS
ATTN_WIDTH = N_HEADS * HEAD_DIM
KV_WIDTH = N_KV_HEADS * HEAD_DIM
WINDOW = 128
BLOCK = 128
LRU_WIDTH = D_MODEL
LRU_BLOCKS = 8
LRU_BLOCK_DIM = LRU_WIDTH // LRU_BLOCKS
CONV_WIDTH = 4
LRU_C = 8.0
N_GROUPS = 4
EXPERTS_PER_GROUP = 4
N_EXPERTS = N_GROUPS * EXPERTS_PER_GROUP
EXPERT_FF = 512
IN_WIDTH = ATTN_WIDTH + 2 * KV_WIDTH + 2 * LRU_WIDTH + 2 * D_MODEL
EPS = 1e-6
NEG_INF = -1e30

LANES = 128
SUBLANES = 8
VMEM_LIMIT = 56 * 1024 * 1024

BF16 = jnp.bfloat16
F32 = jnp.float32


def _params(n_grid_dims):
    return pltpu.CompilerParams(
        dimension_semantics=("arbitrary",) * n_grid_dims,
        vmem_limit_bytes=VMEM_LIMIT,
    )


def _sigmoid(x):
    return 0.5 * jnp.tanh(0.5 * x) + 0.5


def _gelu_tanh(x):
    c = math.sqrt(2.0 / math.pi)
    return 0.5 * x * (1.0 + jnp.tanh(c * (x + 0.044715 * (x * x * x))))


def _rms_norm(xf, g):
    ms = jnp.mean(xf * xf, axis=-1, keepdims=True)
    return xf * lax.rsqrt(ms + EPS) * g


_IN_CHUNK = 512


def _in_proj_kernel(x_ref, g_ref, w_ref, q_ref, k_ref, v_ref, xr_ref, gy_ref, ga_ref, gr_ref):
    n = _rms_norm(x_ref[...], g_ref[...]).astype(BF16)

    def proj(c0, width):
        return jnp.dot(n, w_ref[:, c0:c0 + width], preferred_element_type=F32)

    c = 0
    for j in range(ATTN_WIDTH // _IN_CHUNK):
        q_ref[:, j * _IN_CHUNK:(j + 1) * _IN_CHUNK] = proj(c, _IN_CHUNK).astype(BF16)
        c += _IN_CHUNK
    kv = proj(c, 2 * KV_WIDTH)
    k_ref[...] = kv[:, :KV_WIDTH].astype(BF16)
    v_ref[...] = kv[:, KV_WIDTH:].astype(BF16)
    c += 2 * KV_WIDTH
    for j in range(LRU_WIDTH // _IN_CHUNK):
        xr_ref[:, j * _IN_CHUNK:(j + 1) * _IN_CHUNK] = proj(c, _IN_CHUNK)
        c += _IN_CHUNK
    for j in range(LRU_WIDTH // _IN_CHUNK):
        gy_ref[:, j * _IN_CHUNK:(j + 1) * _IN_CHUNK] = _gelu_tanh(proj(c, _IN_CHUNK)).astype(BF16)
        c += _IN_CHUNK
    for ref in (ga_ref, gr_ref):
        for j in range(D_MODEL // _IN_CHUNK):
            ref[:, j * _IN_CHUNK:(j + 1) * _IN_CHUNK] = _sigmoid(proj(c, _IN_CHUNK)).astype(BF16)
            c += _IN_CHUNK


def _in_proj(x2, g, w_bf, tm):
    n_rows = x2.shape[0]
    row = lambda w: pl.BlockSpec((tm, w), lambda i: (i, 0))
    full = lambda a: pl.BlockSpec(a.shape, lambda i: (0,) * a.ndim)
    out_shapes = (
        jax.ShapeDtypeStruct((n_rows, ATTN_WIDTH), BF16),
        jax.ShapeDtypeStruct((n_rows, KV_WIDTH), BF16),
        jax.ShapeDtypeStruct((n_rows, KV_WIDTH), BF16),
        jax.ShapeDtypeStruct((n_rows, LRU_WIDTH), F32),
        jax.ShapeDtypeStruct((n_rows, LRU_WIDTH), BF16),
        jax.ShapeDtypeStruct((n_rows, D_MODEL), BF16),
        jax.ShapeDtypeStruct((n_rows, D_MODEL), BF16),
    )
    return pl.pallas_call(
        _in_proj_kernel,
        grid=(n_rows // tm,),
        in_specs=[row(D_MODEL), full(g), full(w_bf)],
        out_specs=tuple(row(s.shape[1]) for s in out_shapes),
        out_shape=out_shapes,
        compiler_params=_params(1),
        name="in_proj",
    )(x2, g, w_bf)


_TQ = 512
_SUB = _TQ // BLOCK
_GROUP_ROWS = Q_PER_KV * BLOCK


_KEYS = 4 * BLOCK
_SM_ROWS = 32


def _attn_kernel(q_ref, kp_ref, kc_ref, kn_ref, vp_ref, vc_ref, vn_ref, km_ref, vm_ref,
                 sink_ref, ga_ref, w_ref, o_ref, bias_ref, attn_ref, s_ref, p_ref, m_ref):
    i = pl.program_id(1)
    n_i = pl.num_programs(1)
    scale = HEAD_DIM ** -0.5
    exp_scale = scale * math.log2(math.e)

    @pl.when((pl.program_id(0) == 0) & (i == 0))
    def _init_bias():
        r = lax.broadcasted_iota(jnp.int32, (BLOCK, BLOCK), 0)
        c = lax.broadcasted_iota(jnp.int32, (BLOCK, BLOCK), 1)
        d_prev = (r + BLOCK - c).astype(F32)
        d_cur = jnp.abs(r - c).astype(F32)
        d_next = (c + BLOCK - r).astype(F32)
        for h in range(N_HEADS):
            slope = 2.0 ** (-8.0 * (h + 1.0) / N_HEADS) / scale
            rows = slice(h * BLOCK, (h + 1) * BLOCK)
            bias_ref[rows, 0:BLOCK] = jnp.where(c >= r, -slope * d_prev, NEG_INF / scale)
            bias_ref[rows, BLOCK:2 * BLOCK] = -slope * d_cur
            bias_ref[rows, 2 * BLOCK:3 * BLOCK] = jnp.where(c <= r, -slope * d_next, NEG_INF / scale)
            bias_ref[rows, 3 * BLOCK:4 * BLOCK] = jnp.where(c < N_META, 0.0, NEG_INF / scale)

    nt = (((1,), (1,)), ((), ()))
    for j in range(_SUB):
        rows = slice(j * BLOCK, (j + 1) * BLOCK)
        q = q_ref[0, rows, :]
        if j == 0:
            k3 = [kp_ref[0], kc_ref[0, 0:2 * BLOCK, :]]
            v3 = [vp_ref[0], vc_ref[0, 0:2 * BLOCK, :]]
        elif j == _SUB - 1:
            k3 = [kc_ref[0, (j - 1) * BLOCK:(j + 1) * BLOCK, :], kn_ref[0]]
            v3 = [vc_ref[0, (j - 1) * BLOCK:(j + 1) * BLOCK, :], vn_ref[0]]
        else:
            k3 = [kc_ref[0, (j - 1) * BLOCK:(j + 2) * BLOCK, :]]
            v3 = [vc_ref[0, (j - 1) * BLOCK:(j + 2) * BLOCK, :]]
        k_cat = jnp.concatenate(k3 + [km_ref[...]], axis=0)
        v_cat = jnp.concatenate(v3 + [vm_ref[...]], axis=0)
        masked = []
        if j == 0:
            masked.append((slice(0, BLOCK), i == 0))
        if j == _SUB - 1:
            masked.append((slice(2 * BLOCK, 3 * BLOCK), i == n_i - 1))
        for g in range(N_KV_HEADS):
            cols = slice(g * HEAD_DIM, (g + 1) * HEAD_DIM)
            row0 = g * _GROUP_ROWS
            qg = jnp.concatenate(
                [q[:, (g * Q_PER_KV + h) * HEAD_DIM:(g * Q_PER_KV + h + 1) * HEAD_DIM] for h in range(Q_PER_KV)],
                axis=0)
            pair = j * N_KV_HEADS + g
            s_ref[pair] = lax.dot_general(qg, k_cat[:, cols], nt, preferred_element_type=F32)
            for mask_cols, mask_on in masked:
                s_ref[pair, :, mask_cols] = jnp.where(mask_on, NEG_INF / scale, s_ref[pair, :, mask_cols])

            chunks = [(slice(c * _SM_ROWS, (c + 1) * _SM_ROWS), slice(row0 + c * _SM_ROWS, row0 + (c + 1) * _SM_ROWS))
                      for c in range(_GROUP_ROWS // _SM_ROWS)]
            wide = lambda col: jnp.broadcast_to(col, (_SM_ROWS, BLOCK))
            tiled = lambda stat: jnp.concatenate([stat] * (_KEYS // BLOCK), axis=1)
            for r, rb in chunks:
                z = s_ref[pair, r, :] + bias_ref[rb, :]
                m_ref[pair, r, :] = jnp.maximum(wide(jnp.max(z, axis=-1, keepdims=True)), sink_ref[rb, :])
            for r, rb in chunks:
                m = m_ref[pair, r, :]
                p = jnp.exp2((s_ref[pair, r, :] + bias_ref[rb, :] - tiled(m)) * exp_scale)
                denom = wide(jnp.sum(p, axis=-1, keepdims=True)) + jnp.exp2((sink_ref[rb, :] - m) * exp_scale)
                p_ref[pair, r, :] = p.astype(BF16)
                m_ref[pair, r, :] = 1.0 / denom
            o = jnp.dot(p_ref[pair], v_cat[:, cols], preferred_element_type=F32)
            o = (o * m_ref[pair]).astype(BF16)
            for h in range(Q_PER_KV):
                head = g * Q_PER_KV + h
                attn_ref[rows, head * HEAD_DIM:(head + 1) * HEAD_DIM] = o[h * BLOCK:(h + 1) * BLOCK, :]

    proj = jnp.dot(attn_ref[...], w_ref[...], preferred_element_type=F32)
    o_ref[0] = (ga_ref[0].astype(F32) * proj).astype(BF16)


def _attention(q, k, v, k_meta, v_meta, sink_rows, g_attn, w_bf):
    batch, seq, _ = q.shape
    n_blk = seq // BLOCK
    main = lambda w: pl.BlockSpec((1, _TQ, w), lambda b, i: (b, i, 0))
    prev = pl.BlockSpec((1, BLOCK, KV_WIDTH), lambda b, i: (b, jnp.maximum(i * _SUB - 1, 0), 0))
    nxt = pl.BlockSpec((1, BLOCK, KV_WIDTH), lambda b, i: (b, jnp.minimum((i + 1) * _SUB, n_blk - 1), 0))
    full = lambda a: pl.BlockSpec(a.shape, lambda b, i: (0,) * a.ndim)
    return pl.pallas_call(
        _attn_kernel,
        grid=(batch, seq // _TQ),
        in_specs=[main(ATTN_WIDTH), prev, main(KV_WIDTH), nxt, prev, main(KV_WIDTH), nxt,
                  full(k_meta), full(v_meta), full(sink_rows), main(D_MODEL), full(w_bf)],
        out_specs=main(D_MODEL),
        out_shape=jax.ShapeDtypeStruct((batch, seq, D_MODEL), BF16),
        scratch_shapes=[pltpu.VMEM((N_HEADS * BLOCK, _KEYS), F32),
                        pltpu.VMEM((_TQ, ATTN_WIDTH), BF16),
                        pltpu.VMEM((_SUB * N_KV_HEADS, _GROUP_ROWS, _KEYS), F32),
                        pltpu.VMEM((_SUB * N_KV_HEADS, _GROUP_ROWS, _KEYS), BF16),
                        pltpu.VMEM((_SUB * N_KV_HEADS, _GROUP_ROWS, BLOCK), F32)],
        compiler_params=_params(2),
        name="attention",
    )(q, k, k, k, v, v, v, k_meta, v_meta, sink_rows, g_attn, w_bf)


_TC = 512
_HALO = SUBLANES


def _interleave_in(dst_ref, src, n_rows):
    seg = n_rows // SUBLANES
    for n in range(LRU_BLOCKS):
        for s in range(SUBLANES):
            dst_ref[n, pl.ds(s, seg, stride=SUBLANES), :] = src(s * seg, seg, n)


def _interleave_out(write, src_ref, n_rows):
    seg = n_rows // SUBLANES
    for n in range(LRU_BLOCKS):
        for s in range(SUBLANES):
            write(s * seg, seg, n, src_ref[n, pl.ds(s, seg, stride=SUBLANES), :])


def _lru_gates(n_rows, prev2, prev1, next0, x_ref, cw_ref, cb_ref, wg_ref, bg_ref, lam_ref, a_ref, u_ref):
    seg = n_rows // SUBLANES
    sub = lax.broadcasted_iota(jnp.int32, (SUBLANES, LRU_BLOCK_DIM), 0)
    lam = lam_ref[...]
    decay_scale = (-0.5 * LRU_C * math.log2(math.e)) * (
        jnp.maximum(-lam, 0.0) + jnp.log(1.0 + jnp.exp(-jnp.abs(lam))))
    for n in range(LRU_BLOCKS):
        cols = slice(n * LRU_BLOCK_DIM, (n + 1) * LRU_BLOCK_DIM)
        x = x_ref[n, 0:n_rows, :]
        group = lambda j: x[j * SUBLANES:(j + 1) * SUBLANES, :]
        e0 = jnp.where(sub == 0, prev2(n), pltpu.roll(group(seg - 2), 1, axis=0))
        e1 = jnp.where(sub == 0, prev1(n), pltpu.roll(group(seg - 1), 1, axis=0))
        e_next = jnp.where(sub == SUBLANES - 1, next0(n), pltpu.roll(group(0), SUBLANES - 1, axis=0))
        ext = jnp.concatenate([e0, e1, x, e_next], axis=0)
        xh = cb_ref[:, cols] + sum(
            cw_ref[t:t + 1, cols] * ext[t * SUBLANES:t * SUBLANES + n_rows, :] for t in range(CONV_WIDTH))
        pre = jnp.dot(xh.astype(BF16), wg_ref[n], preferred_element_type=F32)
        t_a = jnp.tanh(pre[:, :LRU_BLOCK_DIM] + bg_ref[0:1, cols])
        t_x = jnp.tanh(pre[:, LRU_BLOCK_DIM:] + bg_ref[1:2, cols])
        scale = decay_scale[:, cols]
        a = jnp.exp2(t_a * scale + scale)
        y = 1.0 - a * a
        a_ref[n, 0:n_rows, :] = a
        u_ref[n, 0:n_rows, :] = (y * lax.rsqrt(jnp.maximum(y, 1e-30))) * ((t_x + 1.0) * xh)


def _lru_scan(n_rows, reverse, carry_in, a_ref, u_ref, h_ref):
    seg = n_rows // SUBLANES
    unroll = min(16, seg)
    sub = lax.broadcasted_iota(jnp.int32, (SUBLANES, LRU_BLOCK_DIM), 0)
    blocks = range(LRU_BLOCKS)

    def rows(jj):
        j = (seg - 1 - jj) if reverse else jj
        return pl.ds(pl.multiple_of(j * SUBLANES, SUBLANES), SUBLANES)

    def local(jj, state):
        hs, ps = state
        r = rows(jj)
        a = [a_ref[n, r, :] for n in blocks]
        return (tuple(a[n] * hs[n] + u_ref[n, r, :] for n in blocks), tuple(a[n] * ps[n] for n in blocks))

    zeros = tuple(jnp.zeros((SUBLANES, LRU_BLOCK_DIM), F32) for _ in blocks)
    ones = tuple(jnp.ones((SUBLANES, LRU_BLOCK_DIM), F32) for _ in blocks)
    h_end, p_end = lax.fori_loop(0, seg, local, (zeros, ones), unroll=unroll)

    seg_in, carry_out = [], []
    for n in blocks:
        p, h = p_end[n], h_end[n]
        for d in (1, 2, 4):
            shift = SUBLANES - d if reverse else d
            ok = (sub < SUBLANES - d) if reverse else (sub >= d)
            h = h + p * jnp.where(ok, pltpu.roll(h, shift, axis=0), 0.0)
            p = p * jnp.where(ok, pltpu.roll(p, shift, axis=0), 1.0)
        seg_out = h + p * carry_in[n]
        first, last = (SUBLANES - 1, 0) if reverse else (0, SUBLANES - 1)
        shift = SUBLANES - 1 if reverse else 1
        seg_in.append(jnp.where(sub == first, carry_in[n], pltpu.roll(seg_out, shift, axis=0)))
        carry_out.append(seg_out[last:last + 1, :])

    if h_ref is not None:
        def final(jj, hs):
            r = rows(jj)
            new = tuple(a_ref[n, r, :] * hs[n] + u_ref[n, r, :] for n in blocks)
            for n in blocks:
                h_ref[n, r, :] = new[n]
            return new

        lax.fori_loop(0, seg, final, tuple(seg_in), unroll=unroll)
    return carry_out


def _lru_kernel(reverse, xr_ref, xp_ref, xn_ref, xm_ref, cw_ref, cb_ref, wg_ref, bg_ref, lam_ref,
                h_ref, carry_ref, x_scr, a_scr, u_scr, h_scr):
    step = pl.program_id(1)
    n_steps = pl.num_programs(1)
    t = (n_steps - 1 - step) if reverse else step
    args = (x_scr, cw_ref, cb_ref, wg_ref, bg_ref, lam_ref, a_scr, u_scr)
    lanes = lambda n: slice(n * LRU_BLOCK_DIM, (n + 1) * LRU_BLOCK_DIM)
    zero_row = lambda n: jnp.zeros((1, LRU_BLOCK_DIM), F32)

    if reverse:
        @pl.when(step == 0)
        def _zero_state():
            carry_ref[...] = jnp.zeros_like(carry_ref)
    else:
        @pl.when(step == 0)
        def _meta_state():
            _interleave_in(x_scr, lambda r0, nr, n: xm_ref[r0:r0 + nr, lanes(n)], N_META)
            _lru_gates(N_META, zero_row, zero_row, lambda n: xr_ref[0, 0:1, lanes(n)], *args)
            state = _lru_scan(N_META, False, [zero_row(n) for n in range(LRU_BLOCKS)], a_scr, u_scr, None)
            for n in range(LRU_BLOCKS):
                carry_ref[0:1, lanes(n)] = state[n]

    def before(row):
        return lambda n: jnp.where(t == 0, xm_ref[N_META - _HALO + row:N_META - _HALO + row + 1, lanes(n)],
                                   xp_ref[0, row:row + 1, lanes(n)])

    after = lambda n: jnp.where(t == n_steps - 1, 0.0, xn_ref[0, 0:1, lanes(n)])
    _interleave_in(x_scr, lambda r0, nr, n: xr_ref[0, r0:r0 + nr, lanes(n)], _TC)
    _lru_gates(_TC, before(_HALO - 2), before(_HALO - 1), after, *args)
    state = _lru_scan(_TC, reverse, [carry_ref[0:1, lanes(n)] for n in range(LRU_BLOCKS)], a_scr, u_scr, h_scr)
    for n in range(LRU_BLOCKS):
        carry_ref[0:1, lanes(n)] = state[n]

    def write(r0, nr, n, rows):
        h_ref[0, r0:r0 + nr, lanes(n)] = rows.astype(h_ref.dtype)

    _interleave_out(write, h_scr, _TC)


def _lru(xr, xr_meta, conv_w, conv_b, wg_bf, bg, lam, reverse):
    batch, seq, _ = xr.shape
    n_steps = seq // _TC
    n_halo = seq // _HALO
    per_tile = _TC // _HALO
    tile = (lambda s: n_steps - 1 - s) if reverse else (lambda s: s)
    main = pl.BlockSpec((1, _TC, LRU_WIDTH), lambda b, s: (b, tile(s), 0))
    before = pl.BlockSpec((1, _HALO, LRU_WIDTH), lambda b, s: (b, jnp.maximum(tile(s) * per_tile - 1, 0), 0))
    after = pl.BlockSpec((1, _HALO, LRU_WIDTH),
                         lambda b, s: (b, jnp.minimum((tile(s) + 1) * per_tile, n_halo - 1), 0))
    full = lambda a: pl.BlockSpec(a.shape, lambda b, s: (0,) * a.ndim)
    return pl.pallas_call(
        functools.partial(_lru_kernel, reverse),
        grid=(batch, n_steps),
        in_specs=[main, before, after, full(xr_meta), full(conv_w), full(conv_b), full(wg_bf), full(bg),
                  full(lam)],
        out_specs=main,
        out_shape=jax.ShapeDtypeStruct((batch, seq, LRU_WIDTH), BF16),
        scratch_shapes=[pltpu.VMEM((SUBLANES, LRU_WIDTH), F32)]
        + [pltpu.VMEM((LRU_BLOCKS, _TC, LRU_BLOCK_DIM), F32)] * 4,
        compiler_params=_params(2),
        name="lru_bwd" if reverse else "lru_fwd",
    )(xr, xr, xr, xr_meta, conv_w, conv_b, wg_bf, bg, lam)


_ROUTER_LANES = LANES
_MERGE_ROWS = 256


def _split_dot(a, b_hi, b_lo):
    a_hi = a.astype(BF16)
    a_lo = (a - a_hi.astype(F32)).astype(BF16)
    both = jnp.dot(a_hi, jnp.concatenate([b_hi, b_lo], axis=1), preferred_element_type=F32)
    return (both[:, :_ROUTER_LANES]
            + (jnp.dot(a_lo, b_hi, preferred_element_type=F32) + both[:, _ROUTER_LANES:]))


def _merge_kernel(x_ref, hf_ref, hb_ref, gy_ref, ga_ref, gr_ref, wrec_ref, wout_ref, g_ref,
                  wr_hi_ref, wr_lo_ref, br_ref, h1_ref, n2_ref, gates_ref):
    subs = [slice(r0, r0 + _MERGE_ROWS) for r0 in range(0, x_ref.shape[0], _MERGE_ROWS)]
    rec, n2 = {}, {}
    for k, rows in enumerate(subs):
        rec_in = (hf_ref[rows, :] + hb_ref[rows, :]) * gy_ref[rows, :]
        rec[k] = jnp.dot(rec_in, wrec_ref[...], preferred_element_type=F32)
    for k, rows in enumerate(subs):
        mix = (ga_ref[rows, :].astype(F32) + gr_ref[rows, :].astype(F32) * rec[k]).astype(BF16)
        h1 = x_ref[rows, :] + jnp.dot(mix, wout_ref[...], preferred_element_type=F32)
        h1_ref[rows, :] = h1
        n2[k] = _rms_norm(h1, g_ref[...])
        n2_ref[rows, :] = n2[k].astype(BF16)
    for k, rows in enumerate(subs):
        _route(rows, n2[k], wr_hi_ref, wr_lo_ref, br_ref, gates_ref)


def _route(rows, n2, wr_hi_ref, wr_lo_ref, br_ref, gates_ref):
    logits = _split_dot(n2, wr_hi_ref[...], wr_lo_ref[...]) + br_ref[...]
    lane_i = lax.broadcasted_iota(jnp.int32, logits.shape, 1)
    lane = lane_i.astype(F32)
    first = lambda mask: jnp.min(jnp.where(mask, lane, float(_ROUTER_LANES)), axis=-1, keepdims=True)
    lg = jnp.where(lane < N_GROUPS, logits, -jnp.inf)
    g_max = jnp.max(lg, axis=-1, keepdims=True)
    g_top_p = 1.0 / jnp.sum(jnp.exp(lg - g_max), axis=-1, keepdims=True)
    g_idx = first(lg == g_max)
    e = lane_i - N_GROUPS
    e_group = jnp.right_shift(e, int(math.log2(EXPERTS_PER_GROUP))).astype(F32)
    in_group = (e >= 0) & (e < N_EXPERTS) & (e_group == g_idx)
    le = jnp.where(in_group, logits, -jnp.inf)
    m1 = jnp.max(le, axis=-1, keepdims=True)
    i1 = first(le == m1)
    le2 = jnp.where(lane == i1, -jnp.inf, le)
    m2 = jnp.max(le2, axis=-1, keepdims=True)
    i2 = first(le2 == m2)
    e2 = jnp.exp(m2 - m1)
    w1 = g_top_p / (1.0 + e2)
    w2 = g_top_p * e2 / (1.0 + e2)
    gates_ref[rows, :] = jnp.where(lane == i1, w1, 0.0) + jnp.where(lane == i2, w2, 0.0)


def _merge(x2, hf, hb, gy, ga, gr, wrec_bf, wout_bf, g, wr_hi, wr_lo, br, tm):
    n_rows = x2.shape[0]
    row = lambda w: pl.BlockSpec((tm, w), lambda i: (i, 0))
    full = lambda a: pl.BlockSpec(a.shape, lambda i: (0,) * a.ndim)
    return pl.pallas_call(
        _merge_kernel,
        grid=(n_rows // tm,),
        in_specs=[row(D_MODEL)] * 6 + [full(wrec_bf), full(wout_bf), full(g), full(wr_hi), full(wr_lo), full(br)],
        out_specs=(row(D_MODEL), row(D_MODEL), row(_ROUTER_LANES)),
        out_shape=(jax.ShapeDtypeStruct((n_rows, D_MODEL), F32),
                   jax.ShapeDtypeStruct((n_rows, D_MODEL), BF16),
                   jax.ShapeDtypeStruct((n_rows, _ROUTER_LANES), F32)),
        compiler_params=_params(1),
        name="merge",
    )(x2, hf, hb, gy, ga, gr, wrec_bf, wout_bf, g, wr_hi, wr_lo, br)


_TT = 512
_CHUNK = 16
_TM = 512
_SLOTS = 1280
_SLOT_CHUNK = 256
_BIG = 1.0e6


def _moe_plan(gates, n_tiles):
    i32 = jnp.int32
    sel = gates[:, N_GROUPS:N_GROUPS + N_EXPERTS] > 0.0
    cnt = jnp.sum(sel.reshape(n_tiles, _TT, N_EXPERTS), axis=1, dtype=i32)
    padc = (cnt + _CHUNK - 1) // _CHUNK * _CHUNK
    lstart = jnp.cumsum(padc, axis=1) - padc
    tot = jnp.sum(padc, axis=0)
    ntile = (tot + _TM - 1) // _TM
    tile_end = jnp.cumsum(ntile)
    base = (tile_end - ntile) * _TM
    roff = base[None, :] + jnp.cumsum(padc, axis=0) - padc
    n_active = tile_end[-1]
    max_tiles = (2 * n_tiles * _TT + n_tiles * N_EXPERTS * (_CHUNK - 1)) // _TM + N_EXPERTS
    g = jnp.minimum(jnp.arange(max_tiles, dtype=i32), n_active - 1)
    tile_expert = jnp.sum(g[:, None] >= tile_end[None, :], axis=1, dtype=i32)
    lstart_vec = jnp.zeros((n_tiles, 1, _ROUTER_LANES), F32).at[:, 0, N_GROUPS:N_GROUPS + N_EXPERTS].set(
        lstart.astype(F32))
    has_tiles = ntile > 0
    ids = jnp.arange(N_EXPERTS, dtype=i32)
    later = (ids[None, :] > ids[:, None]) & has_tiles[None, :]
    expert_next = jnp.min(jnp.where(later, ids[None, :], N_EXPERTS), axis=1)
    plan = dict(
        nchunk=(padc // _CHUNK).reshape(-1), lstart=lstart.reshape(-1), roff=roff.reshape(-1),
        tile_chunks=jnp.sum(padc // _CHUNK, axis=1, dtype=i32),
        tail_start=base + tot, tail_chunks=(ntile * _TM - tot) // _CHUNK,
        tile_expert=tile_expert, tile_block=g, n_active=n_active.reshape(1), lstart_vec=lstart_vec,
        expert_slot=(jnp.cumsum(has_tiles.astype(i32)) - 1) % 2,
        expert_next=jnp.where(expert_next == N_EXPERTS, -1, expert_next).astype(i32))
    return plan, max_tiles


def _slot_positions(gates, lstart_vec):
    sel = gates > 0.0
    r = lax.broadcasted_iota(jnp.int32, (_TT, _TT), 0)
    c = lax.broadcasted_iota(jnp.int32, (_TT, _TT), 1)
    before = (c < r).astype(BF16)
    rank = jnp.dot(before, sel.astype(BF16), preferred_element_type=F32)
    return sel, rank + lstart_vec


def _dispatch_kernel(nchunk_ref, lstart_ref, roff_ref, tchunks_ref, tail_start_ref, tail_chunks_ref, n_active_ref,
                     n2_ref, gates_ref, lvec_ref, xs_ref, xloc_ref, zero_ref, sem, zsem):
    i = pl.program_id(0)
    last = pl.num_programs(0) - 1
    buf = i % 2

    def run_copy(b, src0, dst0, c):
        src = pl.multiple_of(src0 + c * _CHUNK, _CHUNK)
        dst = pl.multiple_of(dst0 + c * _CHUNK, _CHUNK)
        return pltpu.make_async_copy(xloc_ref.at[b, pl.ds(src, _CHUNK), :], xs_ref.at[pl.ds(dst, _CHUNK), :],
                                     sem.at[b])

    def wait_step(b, step):
        def wait(c, carry):
            run_copy(b, 0, 0, 0).wait()
            return carry

        lax.fori_loop(0, tchunks_ref[step], wait, 0)

    @pl.when(i >= 2)
    def _buffer_free():
        wait_step(buf, i - 2)

    sel, pos = _slot_positions(gates_ref[...], lvec_ref[0])
    lo = jnp.min(jnp.where(sel, pos, _BIG).T, axis=0, keepdims=True)
    hi = jnp.max(jnp.where(sel, pos, -1.0).T, axis=0, keepdims=True)
    slot = lax.broadcasted_iota(jnp.int32, (_SLOTS, _TT), 0).astype(F32)
    onehot = ((slot == lo) | (slot == hi)).astype(BF16)
    xloc_ref[buf] = jnp.dot(onehot, n2_ref[...], preferred_element_type=F32).astype(BF16)

    for e in range(N_EXPERTS):
        idx = i * N_EXPERTS + e
        src0, dst0 = lstart_ref[idx], roff_ref[idx]

        def start(c, carry, src0=src0, dst0=dst0):
            run_copy(buf, src0, dst0, c).start()
            return carry

        lax.fori_loop(0, nchunk_ref[idx], start, 0)

    @pl.when(i == last)
    def _drain():
        @pl.when(i >= 1)
        def _previous():
            wait_step(1 - buf, i - 1)

        wait_step(buf, i)

    @pl.when(i == last)
    def _zero_tails():
        zero_ref[...] = jnp.zeros_like(zero_ref)

        def tail_copy(dst0, c):
            dst = pl.multiple_of(dst0 + c * _CHUNK, _CHUNK)
            return pltpu.make_async_copy(zero_ref.at[pl.ds(0, _CHUNK), :], xs_ref.at[pl.ds(dst, _CHUNK), :], zsem)

        def tile_copy(t):
            dst = pl.multiple_of(t * _TM, _TM)
            return pltpu.make_async_copy(zero_ref, xs_ref.at[pl.ds(dst, _TM), :], zsem)

        n_tiles_total = xs_ref.shape[0] // _TM

        def tstart(t, carry):
            tile_copy(t).start()
            return carry

        def twait(t, carry):
            tile_copy(0).wait()
            return carry

        lax.fori_loop(n_active_ref[0], n_tiles_total, tstart, 0)
        lax.fori_loop(n_active_ref[0], n_tiles_total, twait, 0)

        for e in range(N_EXPERTS):
            dst0 = tail_start_ref[e]

            def zstart(c, carry, dst0=dst0):
                tail_copy(dst0, c).start()
                return carry

            def zwait(c, carry):
                tail_copy(0, 0).wait()
                return carry

            lax.fori_loop(0, tail_chunks_ref[e], zstart, 0)
            lax.fori_loop(0, tail_chunks_ref[e], zwait, 0)


def _dispatch(plan, n2, gates, n_tiles, n_sorted):
    row = lambda w: pl.BlockSpec((_TT, w), lambda i, *_: (i, 0))
    return pl.pallas_call(
        _dispatch_kernel,
        grid_spec=pltpu.PrefetchScalarGridSpec(
            num_scalar_prefetch=7,
            grid=(n_tiles,),
            in_specs=[row(D_MODEL), row(_ROUTER_LANES),
                      pl.BlockSpec((1, 1, _ROUTER_LANES), lambda i, *_: (i, 0, 0))],
            out_specs=pl.BlockSpec(memory_space=pl.ANY),
            scratch_shapes=[pltpu.VMEM((2, _SLOTS, D_MODEL), BF16), pltpu.VMEM((_TM, D_MODEL), BF16),
                            pltpu.SemaphoreType.DMA((2,)), pltpu.SemaphoreType.DMA],
        ),
        out_shape=jax.ShapeDtypeStruct((n_sorted, D_MODEL), BF16),
        compiler_params=_params(1),
        name="moe_dispatch",
    )(plan["nchunk"], plan["lstart"], plan["roff"], plan["tile_chunks"], plan["tail_start"], plan["tail_chunks"],
      plan["n_active"], n2, gates, plan["lstart_vec"])


def _experts_kernel(tile_expert_ref, tile_block_ref, n_active_ref, slot_ref, next_ref,
                    xs_ref, wg_hbm, wu_hbm, wd_hbm, ys_ref, wg_f32, wu_f32, wd_f32, wg_bf, wu_bf, wd_bf, sem):
    g = pl.program_id(0)
    active = g < n_active_ref[0]
    expert = tile_expert_ref[g]
    new_expert = (g == 0) | (expert != tile_expert_ref[jnp.maximum(g - 1, 0)])

    def weight_copies(e, slot):
        return [pltpu.make_async_copy(hbm.at[e], buf.at[slot], sem.at[slot])
                for hbm, buf in ((wg_hbm, wg_f32), (wu_hbm, wu_f32), (wd_hbm, wd_f32))]

    @pl.when(active & new_expert)
    def _switch_expert():
        slot = slot_ref[expert]

        @pl.when(g == 0)
        def _first():
            for copy in weight_copies(expert, slot):
                copy.start()

        for copy in weight_copies(expert, slot):
            copy.wait()
        wg_bf[...] = wg_f32[slot].astype(BF16)
        wu_bf[...] = wu_f32[slot].astype(BF16)
        wd_bf[...] = wd_f32[slot].astype(BF16)

        @pl.when(next_ref[expert] >= 0)
        def _prefetch():
            for copy in weight_copies(next_ref[expert], 1 - slot):
                copy.start()

    @pl.when(active)
    def _ffn():
        xs = xs_ref[...]
        gate = jnp.dot(xs, wg_bf[...], preferred_element_type=F32)
        up = jnp.dot(xs, wu_bf[...], preferred_element_type=F32)
        hidden = (gate * _sigmoid(gate) * up).astype(BF16)
        ys_ref[...] = jnp.dot(hidden, wd_bf[...], preferred_element_type=F32).astype(BF16)

    @pl.when(jnp.logical_not(active))
    def _unused_tile():
        ys_ref[...] = jnp.zeros_like(ys_ref)


def _experts(plan, xs, w_gate, w_up, w_down, max_tiles):
    rows_in = pl.BlockSpec((_TM, D_MODEL), lambda g, te, tb, *_: (tb[g], 0))
    rows_out = pl.BlockSpec((_TM, D_MODEL), lambda g, *_: (g, 0))
    weights = (w_gate, w_up, w_down)
    return pl.pallas_call(
        _experts_kernel,
        grid_spec=pltpu.PrefetchScalarGridSpec(
            num_scalar_prefetch=5,
            grid=(max_tiles,),
            in_specs=[rows_in] + [pl.BlockSpec(memory_space=pl.ANY)] * len(weights),
            out_specs=rows_out,
            scratch_shapes=[pltpu.VMEM((2,) + w.shape[1:], F32) for w in weights]
            + [pltpu.VMEM(w.shape[1:], BF16) for w in weights] + [pltpu.SemaphoreType.DMA((2,))],
        ),
        out_shape=jax.ShapeDtypeStruct(xs.shape, BF16),
        compiler_params=_params(1),
        name="moe_experts",
    )(plan["tile_expert"], plan["tile_block"], plan["n_active"], plan["expert_slot"], plan["expert_next"],
      xs, w_gate, w_up, w_down)


def _combine_kernel(nchunk_ref, lstart_ref, roff_ref, tchunks_ref,
                    gates_ref, lvec_ref, h1_ref, g_ref, ys_ref, o_ref, yloc_ref, sem):
    i = pl.program_id(0)
    buf = i % 2

    def run_copy(b, src0, dst0, c):
        src = pl.multiple_of(src0 + c * _CHUNK, _CHUNK)
        dst = pl.multiple_of(dst0 + c * _CHUNK, _CHUNK)
        return pltpu.make_async_copy(ys_ref.at[pl.ds(src, _CHUNK), :], yloc_ref.at[b, pl.ds(dst, _CHUNK), :],
                                     sem.at[b])

    def fetch(b, step):
        for e in range(N_EXPERTS):
            idx = step * N_EXPERTS + e
            src0, dst0 = roff_ref[idx], lstart_ref[idx]

            def start(c, carry, src0=src0, dst0=dst0):
                run_copy(b, src0, dst0, c).start()
                return carry

            lax.fori_loop(0, nchunk_ref[idx], start, 0)

    @pl.when(i == 0)
    def _first():
        yloc_ref[...] = jnp.zeros_like(yloc_ref)
        fetch(buf, i)

    @pl.when(i + 1 < pl.num_programs(0))
    def _prefetch():
        fetch(1 - buf, i + 1)

    def wait(c, carry):
        run_copy(buf, 0, 0, 0).wait()
        return carry

    lax.fori_loop(0, tchunks_ref[i], wait, 0)

    gates = gates_ref[...]
    sel, pos = _slot_positions(gates, lvec_ref[0])
    pos_lo = jnp.where(sel, pos, _BIG)
    pos_hi = jnp.where(sel, pos, -1.0)
    lo = jnp.min(pos_lo, axis=-1, keepdims=True)
    hi = jnp.max(pos_hi, axis=-1, keepdims=True)
    w_lo = jnp.sum(jnp.where(pos_lo == lo, gates, 0.0), axis=-1, keepdims=True)
    w_hi = jnp.where(hi == lo, 0.0, jnp.sum(jnp.where(pos_hi == hi, gates, 0.0), axis=-1, keepdims=True))
    moe = None
    for s0 in range(0, _SLOTS, _SLOT_CHUNK):
        slot = (lax.broadcasted_iota(jnp.int32, (_TT, _SLOT_CHUNK), 1) + s0).astype(F32)
        weights = (jnp.where(slot == lo, w_lo, 0.0) + jnp.where(slot == hi, w_hi, 0.0)).astype(BF16)
        part = jnp.dot(weights, yloc_ref[buf, s0:s0 + _SLOT_CHUNK, :], preferred_element_type=F32)
        moe = part if moe is None else moe + part
    o_ref[...] = _rms_norm(h1_ref[...] + moe, g_ref[...])


def _combine(plan, gates, h1, g, ys, n_tiles):
    row = lambda w: pl.BlockSpec((_TT, w), lambda i, *_: (i, 0))
    return pl.pallas_call(
        _combine_kernel,
        grid_spec=pltpu.PrefetchScalarGridSpec(
            num_scalar_prefetch=4,
            grid=(n_tiles,),
            in_specs=[row(_ROUTER_LANES), pl.BlockSpec((1, 1, _ROUTER_LANES), lambda i, *_: (i, 0, 0)),
                      row(D_MODEL), pl.BlockSpec(g.shape, lambda i, *_: (0, 0)),
                      pl.BlockSpec(memory_space=pl.ANY)],
            out_specs=row(D_MODEL),
            scratch_shapes=[pltpu.VMEM((2, _SLOTS, D_MODEL), BF16), pltpu.SemaphoreType.DMA((2,))],
        ),
        out_shape=jax.ShapeDtypeStruct(h1.shape, F32),
        compiler_params=_params(1),
        name="moe_combine",
    )(plan["nchunk"], plan["lstart"], plan["roff"], plan["tile_chunks"], gates, plan["lstart_vec"], h1, g, ys)


def _moe(n2, gates, h1, w_gate, w_up, w_down, g):
    n_rows = n2.shape[0]
    assert n_rows % _TT == 0 and _SLOTS >= 2 * _TT + N_EXPERTS * (_CHUNK - 1)
    n_tiles = n_rows // _TT
    plan, max_tiles = _moe_plan(gates, n_tiles)
    xs = _dispatch(plan, n2, gates, n_tiles, max_tiles * _TM)
    ys = _experts(plan, xs, w_gate, w_up, w_down, max_tiles)
    return _combine(plan, gates, h1, g, ys, n_tiles)


def kernel(x, meta_tokens, norm_mix_g, w_in, conv_w, conv_b, lru_w_a, lru_b_a, lru_w_x, lru_b_x, lru_lambda, attn_sink, w_attn_branch, w_rec_branch, w_out, norm_ffn_g, w_group, b_group, w_router, b_router, moe_w_gate, moe_w_up, moe_w_down, final_norm_g):
    batch, seq, _ = x.shape
    assert norm_mix_g.shape[0] == 1, "single-layer block"
    assert seq % _TQ == 0 and seq % _TC == 0
    n_rows = batch * seq
    x2 = x.reshape(n_rows, D_MODEL)
    row = lambda a: a.reshape(1, -1).astype(F32)

    w_in_bf = w_in[0].astype(BF16)
    g_mix = row(norm_mix_g[0])
    q, k, v, xr, gy, ga, gr = _in_proj(x2, g_mix, w_in_bf, 512)
    _, k_meta, v_meta, xr_meta, _, _, _ = _in_proj(meta_tokens.astype(F32), g_mix, w_in_bf, N_META)

    sink_rows = jnp.broadcast_to((attn_sink[0].astype(F32) * HEAD_DIM ** 0.5)[:, None, None],
                                 (N_HEADS, BLOCK, BLOCK)).reshape(N_HEADS * BLOCK, BLOCK)
    shape3 = lambda a: a.reshape(batch, seq, a.shape[-1])
    pad_keys = lambda a: jnp.pad(a, ((0, BLOCK - N_META), (0, 0)))
    attn = _attention(shape3(q), shape3(k), shape3(v), pad_keys(k_meta), pad_keys(v_meta), sink_rows, shape3(ga),
                      w_attn_branch[0].astype(BF16))

    h_dirs = []
    for d, reverse in enumerate((False, True)):
        wg = jnp.concatenate([lru_w_a[0, d], lru_w_x[0, d]], axis=-1).astype(BF16)
        bg_half = 0.5 * jnp.stack([lru_b_a[0, d], lru_b_x[0, d]]).astype(F32)
        h_dirs.append(_lru(shape3(xr), xr_meta, 0.5 * conv_w[0].astype(F32), 0.5 * row(conv_b[0]), wg, bg_half,
                           row(lru_lambda[0, d]), reverse))

    w_route = jnp.concatenate([w_group[0], w_router[0]], axis=1).astype(F32)
    w_route = jnp.pad(w_route, ((0, 0), (0, _ROUTER_LANES - w_route.shape[1])))
    wr_hi = w_route.astype(BF16)
    wr_lo = (w_route - wr_hi.astype(F32)).astype(BF16)
    b_route = jnp.pad(jnp.concatenate([b_group[0], b_router[0]]).astype(F32),
                      (0, _ROUTER_LANES - N_GROUPS - N_EXPERTS)).reshape(1, _ROUTER_LANES)
    h1, n2, gates = _merge(x2, h_dirs[0].reshape(n_rows, LRU_WIDTH), h_dirs[1].reshape(n_rows, LRU_WIDTH),
                           gy, attn.reshape(n_rows, D_MODEL), gr,
                           w_rec_branch[0].astype(BF16), w_out[0].astype(BF16), row(norm_ffn_g[0]),
                           wr_hi, wr_lo, b_route, 512)

    out = _moe(n2, gates, h1, moe_w_gate[0].astype(F32), moe_w_up[0].astype(F32), moe_w_down[0].astype(F32),
               row(final_norm_g))
    return out.reshape(batch, seq, D_MODEL)
```

```python
import functools
import math

import jax
import jax.numpy as jnp
from jax import lax
from jax.experimental import pallas as pl
from jax.experimental.pallas import tpu as pltpu

D_MODEL = 1024
N_META = 16
N_HEADS = 8
N_KV_HEADS = 2
HEAD_DIM = 128
Q_PER_KV = N_HEADS // N_KV_HEADS
ATTN_WIDTH = N_HEADS * HEAD_DIM
KV_WIDTH = N_KV_HEADS * HEAD_DIM
WINDOW = 128
BLOCK = 128
LRU_WIDTH = D_MODEL
LRU_BLOCKS = 8
LRU_BLOCK_DIM = LRU_WIDTH // LRU_BLOCKS
CONV_WIDTH = 4
LRU_C = 8.0
N_GROUPS = 4
EXPERTS_PER_GROUP = 4
N_EXPERTS = N_GROUPS * EXPERTS_PER_GROUP
EXPERT_FF = 512
IN_WIDTH = ATTN_WIDTH + 2 * KV_WIDTH + 2 * LRU_WIDTH + 2 * D_MODEL
EPS = 1e-6
NEG_INF = -1e30

LANES = 128
SUBLANES = 8
VMEM_LIMIT = 56 * 1024 * 1024

BF16 = jnp.bfloat16
F32 = jnp.float32


def _params(n_grid_dims):
    return pltpu.CompilerParams(
        dimension_semantics=("arbitrary",) * n_grid_dims,
        vmem_limit_bytes=VMEM_LIMIT,
    )


def _sigmoid(x):
    return 0.5 * jnp.tanh(0.5 * x) + 0.5


def _gelu_tanh(x):
    c = math.sqrt(2.0 / math.pi)
    return 0.5 * x * (1.0 + jnp.tanh(c * (x + 0.044715 * (x * x * x))))


def _rms_norm(xf, g):
    ms = jnp.mean(xf * xf, axis=-1, keepdims=True)
    return xf * lax.rsqrt(ms + EPS) * g


_IN_CHUNK = 512


def _in_proj_kernel(x_ref, g_ref, w_ref, q_ref, k_ref, v_ref, xr_ref, gy_ref, ga_ref, gr_ref):
    n = _rms_norm(x_ref[...], g_ref[...]).astype(BF16)

    def proj(c0, width):
        return jnp.dot(n, w_ref[:, c0:c0 + width], preferred_element_type=F32)

    c = 0
    for j in range(ATTN_WIDTH // _IN_CHUNK):
        q_ref[:, j * _IN_CHUNK:(j + 1) * _IN_CHUNK] = proj(c, _IN_CHUNK).astype(BF16)
        c += _IN_CHUNK
    kv = proj(c, 2 * KV_WIDTH)
    k_ref[...] = kv[:, :KV_WIDTH].astype(BF16)
    v_ref[...] = kv[:, KV_WIDTH:].astype(BF16)
    c += 2 * KV_WIDTH
    for j in range(LRU_WIDTH // _IN_CHUNK):
        xr_ref[:, j * _IN_CHUNK:(j + 1) * _IN_CHUNK] = proj(c, _IN_CHUNK)
        c += _IN_CHUNK
    for j in range(LRU_WIDTH // _IN_CHUNK):
        gy_ref[:, j * _IN_CHUNK:(j + 1) * _IN_CHUNK] = _gelu_tanh(proj(c, _IN_CHUNK)).astype(BF16)
        c += _IN_CHUNK
    for ref in (ga_ref, gr_ref):
        for j in range(D_MODEL // _IN_CHUNK):
            ref[:, j * _IN_CHUNK:(j + 1) * _IN_CHUNK] = _sigmoid(proj(c, _IN_CHUNK)).astype(BF16)
            c += _IN_CHUNK


def _in_proj(x2, g, w_bf, tm):
    n_rows = x2.shape[0]
    row = lambda w: pl.BlockSpec((tm, w), lambda i: (i, 0))
    full = lambda a: pl.BlockSpec(a.shape, lambda i: (0,) * a.ndim)
    out_shapes = (
        jax.ShapeDtypeStruct((n_rows, ATTN_WIDTH), BF16),
        jax.ShapeDtypeStruct((n_rows, KV_WIDTH), BF16),
        jax.ShapeDtypeStruct((n_rows, KV_WIDTH), BF16),
        jax.ShapeDtypeStruct((n_rows, LRU_WIDTH), F32),
        jax.ShapeDtypeStruct((n_rows, LRU_WIDTH), BF16),
        jax.ShapeDtypeStruct((n_rows, D_MODEL), BF16),
        jax.ShapeDtypeStruct((n_rows, D_MODEL), BF16),
    )
    return pl.pallas_call(
        _in_proj_kernel,
        grid=(n_rows // tm,),
        in_specs=[row(D_MODEL), full(g), full(w_bf)],
        out_specs=tuple(row(s.shape[1]) for s in out_shapes),
        out_shape=out_shapes,
        compiler_params=_params(1),
        name="in_proj",
    )(x2, g, w_bf)


_TQ = 512
_SUB = _TQ // BLOCK
_GROUP_ROWS = Q_PER_KV * BLOCK


_KEYS = 4 * BLOCK
_SM_ROWS = 32


def _attn_kernel(q_ref, kp_ref, kc_ref, kn_ref, vp_ref, vc_ref, vn_ref, km_ref, vm_ref,
                 sink_ref, ga_ref, w_ref, o_ref, bias_ref, attn_ref, s_ref, p_ref, m_ref):
    i = pl.program_id(1)
    n_i = pl.num_programs(1)
    scale = HEAD_DIM ** -0.5
    exp_scale = scale * math.log2(math.e)

    @pl.when((pl.program_id(0) == 0) & (i == 0))
    def _init_bias():
        r = lax.broadcasted_iota(jnp.int32, (BLOCK, BLOCK), 0)
        c = lax.broadcasted_iota(jnp.int32, (BLOCK, BLOCK), 1)
        d_prev = (r + BLOCK - c).astype(F32)
        d_cur = jnp.abs(r - c).astype(F32)
        d_next = (c + BLOCK - r).astype(F32)
        for h in range(N_HEADS):
            slope = 2.0 ** (-8.0 * (h + 1.0) / N_HEADS) / scale
            rows = slice(h * BLOCK, (h + 1) * BLOCK)
            bias_ref[rows, 0:BLOCK] = jnp.where(c >= r, -slope * d_prev, NEG_INF / scale)
            bias_ref[rows, BLOCK:2 * BLOCK] = -slope * d_cur
            bias_ref[rows, 2 * BLOCK:3 * BLOCK] = jnp.where(c <= r, -slope * d_next, NEG_INF / scale)
            bias_ref[rows, 3 * BLOCK:4 * BLOCK] = jnp.where(c < N_META, 0.0, NEG_INF / scale)

    nt = (((1,), (1,)), ((), ()))
    for j in range(_SUB):
        rows = slice(j * BLOCK, (j + 1) * BLOCK)
        q = q_ref[0, rows, :]
        if j == 0:
            k3 = [kp_ref[0], kc_ref[0, 0:2 * BLOCK, :]]
            v3 = [vp_ref[0], vc_ref[0, 0:2 * BLOCK, :]]
        elif j == _SUB - 1:
            k3 = [kc_ref[0, (j - 1) * BLOCK:(j + 1) * BLOCK, :], kn_ref[0]]
            v3 = [vc_ref[0, (j - 1) * BLOCK:(j + 1) * BLOCK, :], vn_ref[0]]
        else:
            k3 = [kc_ref[0, (j - 1) * BLOCK:(j + 2) * BLOCK, :]]
            v3 = [vc_ref[0, (j - 1) * BLOCK:(j + 2) * BLOCK, :]]
        k_cat = jnp.concatenate(k3 + [km_ref[...]], axis=0)
        v_cat = jnp.concatenate(v3 + [vm_ref[...]], axis=0)
        masked = []
        if j == 0:
            masked.append((slice(0, BLOCK), i == 0))
        if j == _SUB - 1:
            masked.append((slice(2 * BLOCK, 3 * BLOCK), i == n_i - 1))
        for g in range(N_KV_HEADS):
            cols = slice(g * HEAD_DIM, (g + 1) * HEAD_DIM)
            row0 = g * _GROUP_ROWS
            qg = jnp.concatenate(
                [q[:, (g * Q_PER_KV + h) * HEAD_DIM:(g * Q_PER_KV + h + 1) * HEAD_DIM] for h in range(Q_PER_KV)],
                axis=0)
            pair = j * N_KV_HEADS + g
            s_ref[pair] = lax.dot_general(qg, k_cat[:, cols], nt, preferred_element_type=F32)
            for mask_cols, mask_on in masked:
                s_ref[pair, :, mask_cols] = jnp.where(mask_on, NEG_INF / scale, s_ref[pair, :, mask_cols])

            chunks = [(slice(c * _SM_ROWS, (c + 1) * _SM_ROWS), slice(row0 + c * _SM_ROWS, row0 + (c + 1) * _SM_ROWS))
                      for c in range(_GROUP_ROWS // _SM_ROWS)]
            wide = lambda col: jnp.broadcast_to(col, (_SM_ROWS, BLOCK))
            tiled = lambda stat: jnp.concatenate([stat] * (_KEYS // BLOCK), axis=1)
            for r, rb in chunks:
                z = s_ref[pair, r, :] + bias_ref[rb, :]
                m_ref[pair, r, :] = jnp.maximum(wide(jnp.max(z, axis=-1, keepdims=True)), sink_ref[rb, :])
            for r, rb in chunks:
                m = m_ref[pair, r, :]
                p = jnp.exp2((s_ref[pair, r, :] + bias_ref[rb, :] - tiled(m)) * exp_scale)
                denom = wide(jnp.sum(p, axis=-1, keepdims=True)) + jnp.exp2((sink_ref[rb, :] - m) * exp_scale)
                p_ref[pair, r, :] = p.astype(BF16)
                m_ref[pair, r, :] = 1.0 / denom
            o = jnp.dot(p_ref[pair], v_cat[:, cols], preferred_element_type=F32)
            o = (o * m_ref[pair]).astype(BF16)
            for h in range(Q_PER_KV):
                head = g * Q_PER_KV + h
                attn_ref[rows, head * HEAD_DIM:(head + 1) * HEAD_DIM] = o[h * BLOCK:(h + 1) * BLOCK, :]

    proj = jnp.dot(attn_ref[...], w_ref[...], preferred_element_type=F32)
    o_ref[0] = (ga_ref[0].astype(F32) * proj).astype(BF16)


def _attention(q, k, v, k_meta, v_meta, sink_rows, g_attn, w_bf):
    batch, seq, _ = q.shape
    n_blk = seq // BLOCK
    main = lambda w: pl.BlockSpec((1, _TQ, w), lambda b, i: (b, i, 0))
    prev = pl.BlockSpec((1, BLOCK, KV_WIDTH), lambda b, i: (b, jnp.maximum(i * _SUB - 1, 0), 0))
    nxt = pl.BlockSpec((1, BLOCK, KV_WIDTH), lambda b, i: (b, jnp.minimum((i + 1) * _SUB, n_blk - 1), 0))
    full = lambda a: pl.BlockSpec(a.shape, lambda b, i: (0,) * a.ndim)
    return pl.pallas_call(
        _attn_kernel,
        grid=(batch, seq // _TQ),
        in_specs=[main(ATTN_WIDTH), prev, main(KV_WIDTH), nxt, prev, main(KV_WIDTH), nxt,
                  full(k_meta), full(v_meta), full(sink_rows), main(D_MODEL), full(w_bf)],
        out_specs=main(D_MODEL),
        out_shape=jax.ShapeDtypeStruct((batch, seq, D_MODEL), BF16),
        scratch_shapes=[pltpu.VMEM((N_HEADS * BLOCK, _KEYS), F32),
                        pltpu.VMEM((_TQ, ATTN_WIDTH), BF16),
                        pltpu.VMEM((_SUB * N_KV_HEADS, _GROUP_ROWS, _KEYS), F32),
                        pltpu.VMEM((_SUB * N_KV_HEADS, _GROUP_ROWS, _KEYS), BF16),
                        pltpu.VMEM((_SUB * N_KV_HEADS, _GROUP_ROWS, BLOCK), F32)],
        compiler_params=_params(2),
        name="attention",
    )(q, k, k, k, v, v, v, k_meta, v_meta, sink_rows, g_attn, w_bf)


_TC = 512
_TS = 256
_HALO = SUBLANES


def _interleave_in(dst_ref, src, n_rows, row0=0):
    seg = n_rows // SUBLANES
    for n in range(LRU_BLOCKS):
        for s in range(SUBLANES):
            dst_ref[n, pl.ds(row0 + s, seg, stride=SUBLANES), :] = src(row0 + s * seg, seg, n)


def _interleave_out(write, src_ref, n_rows, row0=0):
    seg = n_rows // SUBLANES
    for n in range(LRU_BLOCKS):
        for s in range(SUBLANES):
            write(row0 + s * seg, seg, n, src_ref[n, pl.ds(row0 + s, seg, stride=SUBLANES), :])


def _lru_gates(n_rows, row0, prev2, prev1, next0, x_ref, cw_ref, cb_ref, wg_ref, bg_ref, lam_ref, a_ref, u_ref):
    seg = n_rows // SUBLANES
    sub = lax.broadcasted_iota(jnp.int32, (SUBLANES, LRU_BLOCK_DIM), 0)
    lam = lam_ref[...]
    decay_scale = (-0.5 * LRU_C * math.log2(math.e)) * (
        jnp.maximum(-lam, 0.0) + jnp.log(1.0 + jnp.exp(-jnp.abs(lam))))
    for n in range(LRU_BLOCKS):
        cols = slice(n * LRU_BLOCK_DIM, (n + 1) * LRU_BLOCK_DIM)
        x = x_ref[n, row0:row0 + n_rows, :]
        group = lambda j: x[j * SUBLANES:(j + 1) * SUBLANES, :]
        e0 = jnp.where(sub == 0, prev2(n), pltpu.roll(group(seg - 2), 1, axis=0))
        e1 = jnp.where(sub == 0, prev1(n), pltpu.roll(group(seg - 1), 1, axis=0))
        e_next = jnp.where(sub == SUBLANES - 1, next0(n), pltpu.roll(group(0), SUBLANES - 1, axis=0))
        ext = jnp.concatenate([e0, e1, x, e_next], axis=0)
        xh = cb_ref[:, cols] + sum(
            cw_ref[t:t + 1, cols] * ext[t * SUBLANES:t * SUBLANES + n_rows, :] for t in range(CONV_WIDTH))
        pre = jnp.dot(xh.astype(BF16), wg_ref[n], preferred_element_type=F32)
        t_a = jnp.tanh(pre[:, :LRU_BLOCK_DIM] + bg_ref[0:1, cols])
        t_x = jnp.tanh(pre[:, LRU_BLOCK_DIM:] + bg_ref[1:2, cols])
        scale = decay_scale[:, cols]
        a = jnp.exp2(t_a * scale + scale)
        y = 1.0 - a * a
        a_ref[n, row0:row0 + n_rows, :] = a
        u_ref[n, row0:row0 + n_rows, :] = (y * lax.rsqrt(jnp.maximum(y, 1e-30))) * ((t_x + 1.0) * xh)


def _lru_scan(n_rows, tiles, reverse, carry_in, a_ref, u_ref, h_ref):
    seg = n_rows // SUBLANES
    unroll = min(32, seg)
    sub = lax.broadcasted_iota(jnp.int32, (SUBLANES, LRU_BLOCK_DIM), 0)
    chains = [(row0, n) for row0 in tiles for n in range(LRU_BLOCKS)]

    def rows(jj, row0):
        j = (seg - 1 - jj) if reverse else jj
        return pl.ds(pl.multiple_of(row0 + j * SUBLANES, SUBLANES), SUBLANES)

    def local(jj, state):
        hs, ps = state
        a = [a_ref[n, rows(jj, row0), :] for row0, n in chains]
        return (tuple(a[c] * hs[c] + u_ref[n, rows(jj, row0), :] for c, (row0, n) in enumerate(chains)),
                tuple(a[c] * ps[c] for c in range(len(chains))))

    zeros = tuple(jnp.zeros((SUBLANES, LRU_BLOCK_DIM), F32) for _ in chains)
    ones = tuple(jnp.ones((SUBLANES, LRU_BLOCK_DIM), F32) for _ in chains)
    h_end, p_end = lax.fori_loop(0, seg, local, (zeros, ones), unroll=unroll)

    seg_in = [None] * len(chains)
    carry = list(carry_in)
    first, last = (SUBLANES - 1, 0) if reverse else (0, SUBLANES - 1)
    for row0 in (reversed(tiles) if reverse else tiles):
        for n in range(LRU_BLOCKS):
            c = chains.index((row0, n))
            p, h = p_end[c], h_end[c]
            for d in (1, 2, 4):
                shift = SUBLANES - d if reverse else d
                ok = (sub < SUBLANES - d) if reverse else (sub >= d)
                h = h + p * jnp.where(ok, pltpu.roll(h, shift, axis=0), 0.0)
                p = p * jnp.where(ok, pltpu.roll(p, shift, axis=0), 1.0)
            seg_out = h + p * carry[n]
            shift = SUBLANES - 1 if reverse else 1
            seg_in[c] = jnp.where(sub == first, carry[n], pltpu.roll(seg_out, shift, axis=0))
            carry[n] = seg_out[last:last + 1, :]

    if h_ref is not None:
        def final(jj, hs):
            new = tuple(a_ref[n, rows(jj, row0), :] * hs[c] + u_ref[n, rows(jj, row0), :]
                        for c, (row0, n) in enumerate(chains))
            for c, (row0, n) in enumerate(chains):
                h_ref[n, rows(jj, row0), :] = new[c]
            return new

        lax.fori_loop(0, seg, final, tuple(seg_in), unroll=unroll)
    return carry


def _lru_kernel(reverse, xr_ref, xp_ref, xn_ref, xm_ref, cw_ref, cb_ref, wg_ref, bg_ref, lam_ref,
                h_ref, carry_ref, x_scr, a_scr, u_scr, h_scr):
    step = pl.program_id(1)
    n_steps = pl.num_programs(1)
    t = (n_steps - 1 - step) if reverse else step
    args = (x_scr, cw_ref, cb_ref, wg_ref, bg_ref, lam_ref, a_scr, u_scr)
    lanes = lambda n: slice(n * LRU_BLOCK_DIM, (n + 1) * LRU_BLOCK_DIM)
    zero_row = lambda n: jnp.zeros((1, LRU_BLOCK_DIM), F32)

    if reverse:
        @pl.when(step == 0)
        def _zero_state():
            carry_ref[...] = jnp.zeros_like(carry_ref)
    else:
        @pl.when(step == 0)
        def _meta_state():
            _interleave_in(x_scr, lambda r0, nr, n: xm_ref[r0:r0 + nr, lanes(n)], N_META)
            _lru_gates(N_META, 0, zero_row, zero_row, lambda n: xr_ref[0, 0:1, lanes(n)], *args)
            state = _lru_scan(N_META, [0], False, [zero_row(n) for n in range(LRU_BLOCKS)], a_scr, u_scr, None)
            for n in range(LRU_BLOCKS):
                carry_ref[0:1, lanes(n)] = state[n]

    def before(row):
        return lambda n: jnp.where(t == 0, xm_ref[N_META - _HALO + row:N_META - _HALO + row + 1, lanes(n)],
                                   xp_ref[0, row:row + 1, lanes(n)])

    after = lambda n: jnp.where(t == n_steps - 1, 0.0, xn_ref[0, 0:1, lanes(n)])
    inside = lambda row: (lambda n: xr_ref[0, row:row + 1, lanes(n)])
    tiles = list(range(0, _TC, _TS))
    for row0 in tiles:
        _interleave_in(x_scr, lambda r0, nr, n: xr_ref[0, r0:r0 + nr, lanes(n)], _TS, row0)
        prev2, prev1 = (before(_HALO - 2), before(_HALO - 1)) if row0 == 0 else (inside(row0 - 2), inside(row0 - 1))
        next0 = after if row0 + _TS == _TC else inside(row0 + _TS)
        _lru_gates(_TS, row0, prev2, prev1, next0, *args)
    state = _lru_scan(_TS, tiles, reverse, [carry_ref[0:1, lanes(n)] for n in range(LRU_BLOCKS)],
                      a_scr, u_scr, h_scr)
    for n in range(LRU_BLOCKS):
        carry_ref[0:1, lanes(n)] = state[n]

    def write(r0, nr, n, rows):
        h_ref[0, r0:r0 + nr, lanes(n)] = rows.astype(h_ref.dtype)

    for row0 in tiles:
        _interleave_out(write, h_scr, _TS, row0)


def _lru(xr, xr_meta, conv_w, conv_b, wg_bf, bg, lam, reverse):
    batch, seq, _ = xr.shape
    n_steps = seq // _TC
    n_halo = seq // _HALO
    per_tile = _TC // _HALO
    tile = (lambda s: n_steps - 1 - s) if reverse else (lambda s: s)
    main = pl.BlockSpec((1, _TC, LRU_WIDTH), lambda b, s: (b, tile(s), 0))
    before = pl.BlockSpec((1, _HALO, LRU_WIDTH), lambda b, s: (b, jnp.maximum(tile(s) * per_tile - 1, 0), 0))
    after = pl.BlockSpec((1, _HALO, LRU_WIDTH),
                         lambda b, s: (b, jnp.minimum((tile(s) + 1) * per_tile, n_halo - 1), 0))
    full = lambda a: pl.BlockSpec(a.shape, lambda b, s: (0,) * a.ndim)
    return pl.pallas_call(
        functools.partial(_lru_kernel, reverse),
        grid=(batch, n_steps),
        in_specs=[main, before, after, full(xr_meta), full(conv_w), full(conv_b), full(wg_bf), full(bg),
                  full(lam)],
        out_specs=main,
        out_shape=jax.ShapeDtypeStruct((batch, seq, LRU_WIDTH), BF16),
        scratch_shapes=[pltpu.VMEM((SUBLANES, LRU_WIDTH), F32)]
        + [pltpu.VMEM((LRU_BLOCKS, _TC, LRU_BLOCK_DIM), F32)] * 4,
        compiler_params=_params(2),
        name="lru_bwd" if reverse else "lru_fwd",
    )(xr, xr, xr, xr_meta, conv_w, conv_b, wg_bf, bg, lam)


_ROUTER_LANES = LANES
_MERGE_ROWS = 256


def _split_dot(a, b_hi, b_lo):
    a_hi = a.astype(BF16)
    a_lo = (a - a_hi.astype(F32)).astype(BF16)
    both = jnp.dot(a_hi, jnp.concatenate([b_hi, b_lo], axis=1), preferred_element_type=F32)
    return (both[:, :_ROUTER_LANES]
            + (jnp.dot(a_lo, b_hi, preferred_element_type=F32) + both[:, _ROUTER_LANES:]))


def _merge_kernel(x_ref, hf_ref, hb_ref, gy_ref, ga_ref, gr_ref, wrec_ref, wout_ref, g_ref,
                  wr_hi_ref, wr_lo_ref, br_ref, h1_ref, n2_ref, gates_ref):
    subs = [slice(r0, r0 + _MERGE_ROWS) for r0 in range(0, x_ref.shape[0], _MERGE_ROWS)]
    rec, n2 = {}, {}
    for k, rows in enumerate(subs):
        rec_in = (hf_ref[rows, :] + hb_ref[rows, :]) * gy_ref[rows, :]
        rec[k] = jnp.dot(rec_in, wrec_ref[...], preferred_element_type=F32)
    for k, rows in enumerate(subs):
        mix = (ga_ref[rows, :].astype(F32) + gr_ref[rows, :].astype(F32) * rec[k]).astype(BF16)
        h1 = x_ref[rows, :] + jnp.dot(mix, wout_ref[...], preferred_element_type=F32)
        h1_ref[rows, :] = h1
        n2[k] = _rms_norm(h1, g_ref[...])
        n2_ref[rows, :] = n2[k].astype(BF16)
    for k, rows in enumerate(subs):
        _route(rows, n2[k], wr_hi_ref, wr_lo_ref, br_ref, gates_ref)


def _route(rows, n2, wr_hi_ref, wr_lo_ref, br_ref, gates_ref):
    logits = _split_dot(n2, wr_hi_ref[...], wr_lo_ref[...]) + br_ref[...]
    lane_i = lax.broadcasted_iota(jnp.int32, logits.shape, 1)
    lane = lane_i.astype(F32)
    first = lambda mask: jnp.min(jnp.where(mask, lane, float(_ROUTER_LANES)), axis=-1, keepdims=True)
    lg = jnp.where(lane < N_GROUPS, logits, -jnp.inf)
    g_max = jnp.max(lg, axis=-1, keepdims=True)
    g_top_p = 1.0 / jnp.sum(jnp.exp(lg - g_max), axis=-1, keepdims=True)
    g_idx = first(lg == g_max)
    e = lane_i - N_GROUPS
    e_group = jnp.right_shift(e, int(math.log2(EXPERTS_PER_GROUP))).astype(F32)
    in_group = (e >= 0) & (e < N_EXPERTS) & (e_group == g_idx)
    le = jnp.where(in_group, logits, -jnp.inf)
    m1 = jnp.max(le, axis=-1, keepdims=True)
    i1 = first(le == m1)
    le2 = jnp.where(lane == i1, -jnp.inf, le)
    m2 = jnp.max(le2, axis=-1, keepdims=True)
    i2 = first(le2 == m2)
    e2 = jnp.exp(m2 - m1)
    w1 = g_top_p / (1.0 + e2)
    w2 = g_top_p * e2 / (1.0 + e2)
    gates_ref[rows, :] = jnp.where(lane == i1, w1, 0.0) + jnp.where(lane == i2, w2, 0.0)


def _merge(x2, hf, hb, gy, ga, gr, wrec_bf, wout_bf, g, wr_hi, wr_lo, br, tm):
    n_rows = x2.shape[0]
    row = lambda w: pl.BlockSpec((tm, w), lambda i: (i, 0))
    full = lambda a: pl.BlockSpec(a.shape, lambda i: (0,) * a.ndim)
    return pl.pallas_call(
        _merge_kernel,
        grid=(n_rows // tm,),
        in_specs=[row(D_MODEL)] * 6 + [full(wrec_bf), full(wout_bf), full(g), full(wr_hi), full(wr_lo), full(br)],
        out_specs=(row(D_MODEL), row(D_MODEL), row(_ROUTER_LANES)),
        out_shape=(jax.ShapeDtypeStruct((n_rows, D_MODEL), F32),
                   jax.ShapeDtypeStruct((n_rows, D_MODEL), BF16),
                   jax.ShapeDtypeStruct((n_rows, _ROUTER_LANES), F32)),
        compiler_params=_params(1),
        name="merge",
    )(x2, hf, hb, gy, ga, gr, wrec_bf, wout_bf, g, wr_hi, wr_lo, br)


_TT = 512
_CHUNK = 16
_TM = 512
_SLOTS = 1280
_SLOT_CHUNK = 256
_BIG = 1.0e6


def _moe_plan(gates, n_tiles):
    i32 = jnp.int32
    sel = gates[:, N_GROUPS:N_GROUPS + N_EXPERTS] > 0.0
    cnt = jnp.sum(sel.reshape(n_tiles, _TT, N_EXPERTS), axis=1, dtype=i32)
    padc = (cnt + _CHUNK - 1) // _CHUNK * _CHUNK
    lstart = jnp.cumsum(padc, axis=1) - padc
    tot = jnp.sum(padc, axis=0)
    ntile = (tot + _TM - 1) // _TM
    tile_end = jnp.cumsum(ntile)
    base = (tile_end - ntile) * _TM
    roff = base[None, :] + jnp.cumsum(padc, axis=0) - padc
    n_active = tile_end[-1]
    max_tiles = (2 * n_tiles * _TT + n_tiles * N_EXPERTS * (_CHUNK - 1)) // _TM + N_EXPERTS
    g = jnp.minimum(jnp.arange(max_tiles, dtype=i32), n_active - 1)
    tile_expert = jnp.sum(g[:, None] >= tile_end[None, :], axis=1, dtype=i32)
    lstart_vec = jnp.zeros((n_tiles, 1, _ROUTER_LANES), F32).at[:, 0, N_GROUPS:N_GROUPS + N_EXPERTS].set(
        lstart.astype(F32))
    has_tiles = ntile > 0
    ids = jnp.arange(N_EXPERTS, dtype=i32)
    later = (ids[None, :] > ids[:, None]) & has_tiles[None, :]
    expert_next = jnp.min(jnp.where(later, ids[None, :], N_EXPERTS), axis=1)
    plan = dict(
        nchunk=(padc // _CHUNK).reshape(-1), lstart=lstart.reshape(-1), roff=roff.reshape(-1),
        tile_chunks=jnp.sum(padc // _CHUNK, axis=1, dtype=i32),
        tail_start=base + tot, tail_chunks=(ntile * _TM - tot) // _CHUNK,
        tile_expert=tile_expert, tile_block=g, n_active=n_active.reshape(1), lstart_vec=lstart_vec,
        expert_slot=(jnp.cumsum(has_tiles.astype(i32)) - 1) % 2,
        expert_next=jnp.where(expert_next == N_EXPERTS, -1, expert_next).astype(i32))
    return plan, max_tiles


def _slot_positions(gates, lstart_vec):
    sel = gates > 0.0
    r = lax.broadcasted_iota(jnp.int32, (_TT, _TT), 0)
    c = lax.broadcasted_iota(jnp.int32, (_TT, _TT), 1)
    before = (c < r).astype(BF16)
    rank = jnp.dot(before, sel.astype(BF16), preferred_element_type=F32)
    return sel, rank + lstart_vec


def _dispatch_kernel(nchunk_ref, lstart_ref, roff_ref, tchunks_ref, tail_start_ref, tail_chunks_ref, n_active_ref,
                     n2_ref, gates_ref, lvec_ref, xs_ref, xloc_ref, zero_ref, sem, zsem):
    i = pl.program_id(0)
    last = pl.num_programs(0) - 1
    buf = i % 2

    def run_copy(b, src0, dst0, c):
        src = pl.multiple_of(src0 + c * _CHUNK, _CHUNK)
        dst = pl.multiple_of(dst0 + c * _CHUNK, _CHUNK)
        return pltpu.make_async_copy(xloc_ref.at[b, pl.ds(src, _CHUNK), :], xs_ref.at[pl.ds(dst, _CHUNK), :],
                                     sem.at[b])

    def wait_step(b, step):
        def wait(c, carry):
            run_copy(b, 0, 0, 0).wait()
            return carry

        lax.fori_loop(0, tchunks_ref[step], wait, 0)

    @pl.when(i >= 2)
    def _buffer_free():
        wait_step(buf, i - 2)

    sel, pos = _slot_positions(gates_ref[...], lvec_ref[0])
    lo = jnp.min(jnp.where(sel, pos, _BIG).T, axis=0, keepdims=True)
    hi = jnp.max(jnp.where(sel, pos, -1.0).T, axis=0, keepdims=True)
    slot = lax.broadcasted_iota(jnp.int32, (_SLOTS, _TT), 0).astype(F32)
    onehot = ((slot == lo) | (slot == hi)).astype(BF16)
    xloc_ref[buf] = jnp.dot(onehot, n2_ref[...], preferred_element_type=F32).astype(BF16)

    for e in range(N_EXPERTS):
        idx = i * N_EXPERTS + e
        src0, dst0 = lstart_ref[idx], roff_ref[idx]

        def start(c, carry, src0=src0, dst0=dst0):
            run_copy(buf, src0, dst0, c).start()
            return carry

        lax.fori_loop(0, nchunk_ref[idx], start, 0)

    @pl.when(i == last)
    def _drain():
        @pl.when(i >= 1)
        def _previous():
            wait_step(1 - buf, i - 1)

        wait_step(buf, i)

    @pl.when(i == last)
    def _zero_tails():
        zero_ref[...] = jnp.zeros_like(zero_ref)

        def tail_copy(dst0, c):
            dst = pl.multiple_of(dst0 + c * _CHUNK, _CHUNK)
            return pltpu.make_async_copy(zero_ref.at[pl.ds(0, _CHUNK), :], xs_ref.at[pl.ds(dst, _CHUNK), :], zsem)

        def tile_copy(t):
            dst = pl.multiple_of(t * _TM, _TM)
            return pltpu.make_async_copy(zero_ref, xs_ref.at[pl.ds(dst, _TM), :], zsem)

        n_tiles_total = xs_ref.shape[0] // _TM

        def tstart(t, carry):
            tile_copy(t).start()
            return carry

        def twait(t, carry):
            tile_copy(0).wait()
            return carry

        lax.fori_loop(n_active_ref[0], n_tiles_total, tstart, 0)
        lax.fori_loop(n_active_ref[0], n_tiles_total, twait, 0)

        for e in range(N_EXPERTS):
            dst0 = tail_start_ref[e]

            def zstart(c, carry, dst0=dst0):
                tail_copy(dst0, c).start()
                return carry

            def zwait(c, carry):
                tail_copy(0, 0).wait()
                return carry

            lax.fori_loop(0, tail_chunks_ref[e], zstart, 0)
            lax.fori_loop(0, tail_chunks_ref[e], zwait, 0)


def _dispatch(plan, n2, gates, n_tiles, n_sorted):
    row = lambda w: pl.BlockSpec((_TT, w), lambda i, *_: (i, 0))
    return pl.pallas_call(
        _dispatch_kernel,
        grid_spec=pltpu.PrefetchScalarGridSpec(
            num_scalar_prefetch=7,
            grid=(n_tiles,),
            in_specs=[row(D_MODEL), row(_ROUTER_LANES),
                      pl.BlockSpec((1, 1, _ROUTER_LANES), lambda i, *_: (i, 0, 0))],
            out_specs=pl.BlockSpec(memory_space=pl.ANY),
            scratch_shapes=[pltpu.VMEM((2, _SLOTS, D_MODEL), BF16), pltpu.VMEM((_TM, D_MODEL), BF16),
                            pltpu.SemaphoreType.DMA((2,)), pltpu.SemaphoreType.DMA],
        ),
        out_shape=jax.ShapeDtypeStruct((n_sorted, D_MODEL), BF16),
        compiler_params=_params(1),
        name="moe_dispatch",
    )(plan["nchunk"], plan["lstart"], plan["roff"], plan["tile_chunks"], plan["tail_start"], plan["tail_chunks"],
      plan["n_active"], n2, gates, plan["lstart_vec"])


def _experts_kernel(tile_expert_ref, tile_block_ref, n_active_ref, slot_ref, next_ref,
                    xs_ref, wg_hbm, wu_hbm, wd_hbm, ys_ref, wg_f32, wu_f32, wd_f32, wg_bf, wu_bf, wd_bf, sem):
    g = pl.program_id(0)
    active = g < n_active_ref[0]
    expert = tile_expert_ref[g]
    new_expert = (g == 0) | (expert != tile_expert_ref[jnp.maximum(g - 1, 0)])

    def weight_copies(e, slot):
        return [pltpu.make_async_copy(hbm.at[e], buf.at[slot], sem.at[slot])
                for hbm, buf in ((wg_hbm, wg_f32), (wu_hbm, wu_f32), (wd_hbm, wd_f32))]

    @pl.when(active & new_expert)
    def _switch_expert():
        slot = slot_ref[expert]

        @pl.when(g == 0)
        def _first():
            for copy in weight_copies(expert, slot):
                copy.start()

        for copy in weight_copies(expert, slot):
            copy.wait()
        wg_bf[...] = wg_f32[slot].astype(BF16)
        wu_bf[...] = wu_f32[slot].astype(BF16)
        wd_bf[...] = wd_f32[slot].astype(BF16)

        @pl.when(next_ref[expert] >= 0)
        def _prefetch():
            for copy in weight_copies(next_ref[expert], 1 - slot):
                copy.start()

    @pl.when(active)
    def _ffn():
        xs = xs_ref[...]
        gate = jnp.dot(xs, wg_bf[...], preferred_element_type=F32)
        up = jnp.dot(xs, wu_bf[...], preferred_element_type=F32)
        hidden = (gate * _sigmoid(gate) * up).astype(BF16)
        ys_ref[...] = jnp.dot(hidden, wd_bf[...], preferred_element_type=F32).astype(BF16)

    @pl.when(jnp.logical_not(active))
    def _unused_tile():
        ys_ref[...] = jnp.zeros_like(ys_ref)


def _experts(plan, xs, w_gate, w_up, w_down, max_tiles):
    rows_in = pl.BlockSpec((_TM, D_MODEL), lambda g, te, tb, *_: (tb[g], 0))
    rows_out = pl.BlockSpec((_TM, D_MODEL), lambda g, *_: (g, 0))
    weights = (w_gate, w_up, w_down)
    return pl.pallas_call(
        _experts_kernel,
        grid_spec=pltpu.PrefetchScalarGridSpec(
            num_scalar_prefetch=5,
            grid=(max_tiles,),
            in_specs=[rows_in] + [pl.BlockSpec(memory_space=pl.ANY)] * len(weights),
            out_specs=rows_out,
            scratch_shapes=[pltpu.VMEM((2,) + w.shape[1:], F32) for w in weights]
            + [pltpu.VMEM(w.shape[1:], BF16) for w in weights] + [pltpu.SemaphoreType.DMA((2,))],
        ),
        out_shape=jax.ShapeDtypeStruct(xs.shape, BF16),
        compiler_params=_params(1),
        name="moe_experts",
    )(plan["tile_expert"], plan["tile_block"], plan["n_active"], plan["expert_slot"], plan["expert_next"],
      xs, w_gate, w_up, w_down)


def _combine_kernel(nchunk_ref, lstart_ref, roff_ref, tchunks_ref,
                    gates_ref, lvec_ref, h1_ref, g_ref, ys_ref, o_ref, yloc_ref, sem):
    i = pl.program_id(0)
    buf = i % 2

    def run_copy(b, src0, dst0, c):
        src = pl.multiple_of(src0 + c * _CHUNK, _CHUNK)
        dst = pl.multiple_of(dst0 + c * _CHUNK, _CHUNK)
        return pltpu.make_async_copy(ys_ref.at[pl.ds(src, _CHUNK), :], yloc_ref.at[b, pl.ds(dst, _CHUNK), :],
                                     sem.at[b])

    def fetch(b, step):
        for e in range(N_EXPERTS):
            idx = step * N_EXPERTS + e
            src0, dst0 = roff_ref[idx], lstart_ref[idx]

            def start(c, carry, src0=src0, dst0=dst0):
                run_copy(b, src0, dst0, c).start()
                return carry

            lax.fori_loop(0, nchunk_ref[idx], start, 0)

    @pl.when(i == 0)
    def _first():
        yloc_ref[...] = jnp.zeros_like(yloc_ref)
        fetch(buf, i)

    @pl.when(i + 1 < pl.num_programs(0))
    def _prefetch():
        fetch(1 - buf, i + 1)

    def wait(c, carry):
        run_copy(buf, 0, 0, 0).wait()
        return carry

    lax.fori_loop(0, tchunks_ref[i], wait, 0)

    gates = gates_ref[...]
    sel, pos = _slot_positions(gates, lvec_ref[0])
    pos_lo = jnp.where(sel, pos, _BIG)
    pos_hi = jnp.where(sel, pos, -1.0)
    lo = jnp.min(pos_lo, axis=-1, keepdims=True)
    hi = jnp.max(pos_hi, axis=-1, keepdims=True)
    w_lo = jnp.sum(jnp.where(pos_lo == lo, gates, 0.0), axis=-1, keepdims=True)
    w_hi = jnp.where(hi == lo, 0.0, jnp.sum(jnp.where(pos_hi == hi, gates, 0.0), axis=-1, keepdims=True))
    moe = None
    for s0 in range(0, _SLOTS, _SLOT_CHUNK):
        slot = (lax.broadcasted_iota(jnp.int32, (_TT, _SLOT_CHUNK), 1) + s0).astype(F32)
        weights = (jnp.where(slot == lo, w_lo, 0.0) + jnp.where(slot == hi, w_hi, 0.0)).astype(BF16)
        part = jnp.dot(weights, yloc_ref[buf, s0:s0 + _SLOT_CHUNK, :], preferred_element_type=F32)
        moe = part if moe is None else moe + part
    o_ref[...] = _rms_norm(h1_ref[...] + moe, g_ref[...])


def _combine(plan, gates, h1, g, ys, n_tiles):
    row = lambda w: pl.BlockSpec((_TT, w), lambda i, *_: (i, 0))
    return pl.pallas_call(
        _combine_kernel,
        grid_spec=pltpu.PrefetchScalarGridSpec(
            num_scalar_prefetch=4,
            grid=(n_tiles,),
            in_specs=[row(_ROUTER_LANES), pl.BlockSpec((1, 1, _ROUTER_LANES), lambda i, *_: (i, 0, 0)),
                      row(D_MODEL), pl.BlockSpec(g.shape, lambda i, *_: (0, 0)),
                      pl.BlockSpec(memory_space=pl.ANY)],
            out_specs=row(D_MODEL),
            scratch_shapes=[pltpu.VMEM((2, _SLOTS, D_MODEL), BF16), pltpu.SemaphoreType.DMA((2,))],
        ),
        out_shape=jax.ShapeDtypeStruct(h1.shape, F32),
        compiler_params=_params(1),
        name="moe_combine",
    )(plan["nchunk"], plan["lstart"], plan["roff"], plan["tile_chunks"], gates, plan["lstart_vec"], h1, g, ys)


def _moe(n2, gates, h1, w_gate, w_up, w_down, g):
    n_rows = n2.shape[0]
    assert n_rows % _TT == 0 and _SLOTS >= 2 * _TT + N_EXPERTS * (_CHUNK - 1)
    n_tiles = n_rows // _TT
    plan, max_tiles = _moe_plan(gates, n_tiles)
    xs = _dispatch(plan, n2, gates, n_tiles, max_tiles * _TM)
    ys = _experts(plan, xs, w_gate, w_up, w_down, max_tiles)
    return _combine(plan, gates, h1, g, ys, n_tiles)


def kernel(x, meta_tokens, norm_mix_g, w_in, conv_w, conv_b, lru_w_a, lru_b_a, lru_w_x, lru_b_x, lru_lambda, attn_sink, w_attn_branch, w_rec_branch, w_out, norm_ffn_g, w_group, b_group, w_router, b_router, moe_w_gate, moe_w_up, moe_w_down, final_norm_g):
    batch, seq, _ = x.shape
    assert norm_mix_g.shape[0] == 1, "single-layer block"
    assert seq % _TQ == 0 and seq % _TC == 0
    n_rows = batch * seq
    x2 = x.reshape(n_rows, D_MODEL)
    row = lambda a: a.reshape(1, -1).astype(F32)

    w_in_bf = w_in[0].astype(BF16)
    g_mix = row(norm_mix_g[0])
    q, k, v, xr, gy, ga, gr = _in_proj(x2, g_mix, w_in_bf, 512)
    _, k_meta, v_meta, xr_meta, _, _, _ = _in_proj(meta_tokens.astype(F32), g_mix, w_in_bf, N_META)

    sink_rows = jnp.broadcast_to((attn_sink[0].astype(F32) * HEAD_DIM ** 0.5)[:, None, None],
                                 (N_HEADS, BLOCK, BLOCK)).reshape(N_HEADS * BLOCK, BLOCK)
    shape3 = lambda a: a.reshape(batch, seq, a.shape[-1])
    pad_keys = lambda a: jnp.pad(a, ((0, BLOCK - N_META), (0, 0)))
    attn = _attention(shape3(q), shape3(k), shape3(v), pad_keys(k_meta), pad_keys(v_meta), sink_rows, shape3(ga),
                      w_attn_branch[0].astype(BF16))

    h_dirs = []
    for d, reverse in enumerate((False, True)):
        wg = jnp.concatenate([lru_w_a[0, d], lru_w_x[0, d]], axis=-1).astype(BF16)
        bg_half = 0.5 * jnp.stack([lru_b_a[0, d], lru_b_x[0, d]]).astype(F32)
        h_dirs.append(_lru(shape3(xr), xr_meta, 0.5 * conv_w[0].astype(F32), 0.5 * row(conv_b[0]), wg, bg_half,
                           row(lru_lambda[0, d]), reverse))

    w_route = jnp.concatenate([w_group[0], w_router[0]], axis=1).astype(F32)
    w_route = jnp.pad(w_route, ((0, 0), (0, _ROUTER_LANES - w_route.shape[1])))
    wr_hi = w_route.astype(BF16)
    wr_lo = (w_route - wr_hi.astype(F32)).astype(BF16)
    b_route = jnp.pad(jnp.concatenate([b_group[0], b_router[0]]).astype(F32),
                      (0, _ROUTER_LANES - N_GROUPS - N_EXPERTS)).reshape(1, _ROUTER_LANES)
    h1, n2, gates = _merge(x2, h_dirs[0].reshape(n_rows, LRU_WIDTH), h_dirs[1].reshape(n_rows, LRU_WIDTH),
                           gy, attn.reshape(n_rows, D_MODEL), gr,
                           w_rec_branch[0].astype(BF16), w_out[0].astype(BF16), row(norm_ffn_g[0]),
                           wr_hi, wr_lo, b_route, 512)

    out = _moe(n2, gates, h1, moe_w_gate[0].astype(F32), moe_w_up[0].astype(F32), moe_w_down[0].astype(F32),
               row(final_norm_g))
    return out.reshape(batch, seq, D_MODEL)
```

```python
import functools
import math

import jax
import jax.numpy as jnp
from jax import lax
from jax.experimental import pallas as pl
from jax.experimental.pallas import tpu as pltpu

D_MODEL = 1024
N_META = 16
N_HEADS = 8
N_KV_HEADS = 2
HEAD_DIM = 128
Q_PER_KV = N_HEADS // N_KV_HEADS
ATTN_WIDTH = N_HEADS * HEAD_DIM
KV_WIDTH = N_KV_HEADS * HEAD_DIM
WINDOW = 128
BLOCK = 128
LRU_WIDTH = D_MODEL
LRU_BLOCKS = 8
LRU_BLOCK_DIM = LRU_WIDTH // LRU_BLOCKS
CONV_WIDTH = 4
LRU_C = 8.0
N_GROUPS = 4
EXPERTS_PER_GROUP = 4
N_EXPERTS = N_GROUPS * EXPERTS_PER_GROUP
EXPERT_FF = 512
IN_WIDTH = ATTN_WIDTH + 2 * KV_WIDTH + 2 * LRU_WIDTH + 2 * D_MODEL
EPS = 1e-6
NEG_INF = -1e30

LANES = 128
SUBLANES = 8
VMEM_LIMIT = 56 * 1024 * 1024

BF16 = jnp.bfloat16
F32 = jnp.float32


def _params(n_grid_dims):
    return pltpu.CompilerParams(
        dimension_semantics=("arbitrary",) * n_grid_dims,
        vmem_limit_bytes=VMEM_LIMIT,
    )


def _sigmoid(x):
    return 0.5 * jnp.tanh(0.5 * x) + 0.5


def _gelu_tanh(x):
    c = math.sqrt(2.0 / math.pi)
    return 0.5 * x * (1.0 + jnp.tanh(c * (x + 0.044715 * (x * x * x))))


def _rms_norm(xf, g):
    ms = jnp.mean(xf * xf, axis=-1, keepdims=True)
    return xf * lax.rsqrt(ms + EPS) * g


_IN_CHUNK = 512


def _in_proj_kernel(x_ref, g_ref, w_ref, q_ref, k_ref, v_ref, xr_ref, gy_ref, ga_ref, gr_ref):
    n = _rms_norm(x_ref[...], g_ref[...]).astype(BF16)

    def proj(c0, width):
        return jnp.dot(n, w_ref[:, c0:c0 + width], preferred_element_type=F32)

    c = 0
    for j in range(ATTN_WIDTH // _IN_CHUNK):
        q_ref[:, j * _IN_CHUNK:(j + 1) * _IN_CHUNK] = proj(c, _IN_CHUNK).astype(BF16)
        c += _IN_CHUNK
    kv = proj(c, 2 * KV_WIDTH)
    k_ref[...] = kv[:, :KV_WIDTH].astype(BF16)
    v_ref[...] = kv[:, KV_WIDTH:].astype(BF16)
    c += 2 * KV_WIDTH
    for j in range(LRU_WIDTH // _IN_CHUNK):
        xr_ref[:, j * _IN_CHUNK:(j + 1) * _IN_CHUNK] = proj(c, _IN_CHUNK)
        c += _IN_CHUNK
    for j in range(LRU_WIDTH // _IN_CHUNK):
        gy_ref[:, j * _IN_CHUNK:(j + 1) * _IN_CHUNK] = _gelu_tanh(proj(c, _IN_CHUNK)).astype(BF16)
        c += _IN_CHUNK
    for ref in (ga_ref, gr_ref):
        for j in range(D_MODEL // _IN_CHUNK):
            ref[:, j * _IN_CHUNK:(j + 1) * _IN_CHUNK] = _sigmoid(proj(c, _IN_CHUNK)).astype(BF16)
            c += _IN_CHUNK


def _in_proj(x2, g, w_bf, tm):
    n_rows = x2.shape[0]
    row = lambda w: pl.BlockSpec((tm, w), lambda i: (i, 0))
    full = lambda a: pl.BlockSpec(a.shape, lambda i: (0,) * a.ndim)
    out_shapes = (
        jax.ShapeDtypeStruct((n_rows, ATTN_WIDTH), BF16),
        jax.ShapeDtypeStruct((n_rows, KV_WIDTH), BF16),
        jax.ShapeDtypeStruct((n_rows, KV_WIDTH), BF16),
        jax.ShapeDtypeStruct((n_rows, LRU_WIDTH), F32),
        jax.ShapeDtypeStruct((n_rows, LRU_WIDTH), BF16),
        jax.ShapeDtypeStruct((n_rows, D_MODEL), BF16),
        jax.ShapeDtypeStruct((n_rows, D_MODEL), BF16),
    )
    return pl.pallas_call(
        _in_proj_kernel,
        grid=(n_rows // tm,),
        in_specs=[row(D_MODEL), full(g), full(w_bf)],
        out_specs=tuple(row(s.shape[1]) for s in out_shapes),
        out_shape=out_shapes,
        compiler_params=_params(1),
        name="in_proj",
    )(x2, g, w_bf)


_TQ = 512
_SUB = _TQ // BLOCK
_GROUP_ROWS = Q_PER_KV * BLOCK


_KEYS = 4 * BLOCK
_SM_ROWS = 32


def _attn_kernel(q_ref, kp_ref, kc_ref, kn_ref, vp_ref, vc_ref, vn_ref, km_ref, vm_ref,
                 sink_ref, ga_ref, w_ref, o_ref, bias_ref, attn_ref, s_ref, p_ref, m_ref):
    i = pl.program_id(1)
    n_i = pl.num_programs(1)
    scale = HEAD_DIM ** -0.5
    exp_scale = scale * math.log2(math.e)

    @pl.when((pl.program_id(0) == 0) & (i == 0))
    def _init_bias():
        r = lax.broadcasted_iota(jnp.int32, (BLOCK, BLOCK), 0)
        c = lax.broadcasted_iota(jnp.int32, (BLOCK, BLOCK), 1)
        d_prev = (r + BLOCK - c).astype(F32)
        d_cur = jnp.abs(r - c).astype(F32)
        d_next = (c + BLOCK - r).astype(F32)
        for h in range(N_HEADS):
            slope = 2.0 ** (-8.0 * (h + 1.0) / N_HEADS) / scale
            rows = slice(h * BLOCK, (h + 1) * BLOCK)
            bias_ref[rows, 0:BLOCK] = jnp.where(c >= r, -slope * d_prev, NEG_INF / scale)
            bias_ref[rows, BLOCK:2 * BLOCK] = -slope * d_cur
            bias_ref[rows, 2 * BLOCK:3 * BLOCK] = jnp.where(c <= r, -slope * d_next, NEG_INF / scale)
            bias_ref[rows, 3 * BLOCK:4 * BLOCK] = jnp.where(c < N_META, 0.0, NEG_INF / scale)

    nt = (((1,), (1,)), ((), ()))
    for j in range(_SUB):
        rows = slice(j * BLOCK, (j + 1) * BLOCK)
        q = q_ref[0, rows, :]
        if j == 0:
            k3 = [kp_ref[0], kc_ref[0, 0:2 * BLOCK, :]]
            v3 = [vp_ref[0], vc_ref[0, 0:2 * BLOCK, :]]
        elif j == _SUB - 1:
            k3 = [kc_ref[0, (j - 1) * BLOCK:(j + 1) * BLOCK, :], kn_ref[0]]
            v3 = [vc_ref[0, (j - 1) * BLOCK:(j + 1) * BLOCK, :], vn_ref[0]]
        else:
            k3 = [kc_ref[0, (j - 1) * BLOCK:(j + 2) * BLOCK, :]]
            v3 = [vc_ref[0, (j - 1) * BLOCK:(j + 2) * BLOCK, :]]
        k_cat = jnp.concatenate(k3 + [km_ref[...]], axis=0)
        v_cat = jnp.concatenate(v3 + [vm_ref[...]], axis=0)
        masked = []
        if j == 0:
            masked.append((slice(0, BLOCK), i == 0))
        if j == _SUB - 1:
            masked.append((slice(2 * BLOCK, 3 * BLOCK), i == n_i - 1))
        for g in range(N_KV_HEADS):
            cols = slice(g * HEAD_DIM, (g + 1) * HEAD_DIM)
            row0 = g * _GROUP_ROWS
            qg = jnp.concatenate(
                [q[:, (g * Q_PER_KV + h) * HEAD_DIM:(g * Q_PER_KV + h + 1) * HEAD_DIM] for h in range(Q_PER_KV)],
                axis=0)
            pair = j * N_KV_HEADS + g
            s_ref[pair] = lax.dot_general(qg, k_cat[:, cols], nt, preferred_element_type=F32)
            for mask_cols, mask_on in masked:
                s_ref[pair, :, mask_cols] = jnp.where(mask_on, NEG_INF / scale, s_ref[pair, :, mask_cols])

            chunks = [(slice(c * _SM_ROWS, (c + 1) * _SM_ROWS), slice(row0 + c * _SM_ROWS, row0 + (c + 1) * _SM_ROWS))
                      for c in range(_GROUP_ROWS // _SM_ROWS)]
            wide = lambda col: jnp.broadcast_to(col, (_SM_ROWS, BLOCK))
            tiled = lambda stat: jnp.concatenate([stat] * (_KEYS // BLOCK), axis=1)
            for r, rb in chunks:
                z = s_ref[pair, r, :] + bias_ref[rb, :]
                m_ref[pair, r, :] = jnp.maximum(wide(jnp.max(z, axis=-1, keepdims=True)), sink_ref[rb, :])
            for r, rb in chunks:
                m = m_ref[pair, r, :]
                p = jnp.exp2((s_ref[pair, r, :] + bias_ref[rb, :] - tiled(m)) * exp_scale)
                denom = wide(jnp.sum(p, axis=-1, keepdims=True)) + jnp.exp2((sink_ref[rb, :] - m) * exp_scale)
                p_ref[pair, r, :] = p.astype(BF16)
                m_ref[pair, r, :] = 1.0 / denom
            o = jnp.dot(p_ref[pair], v_cat[:, cols], preferred_element_type=F32)
            o = (o * m_ref[pair]).astype(BF16)
            for h in range(Q_PER_KV):
                head = g * Q_PER_KV + h
                attn_ref[rows, head * HEAD_DIM:(head + 1) * HEAD_DIM] = o[h * BLOCK:(h + 1) * BLOCK, :]

    proj = jnp.dot(attn_ref[...], w_ref[...], preferred_element_type=F32)
    o_ref[0] = (ga_ref[0].astype(F32) * proj).astype(BF16)


def _attention(q, k, v, k_meta, v_meta, sink_rows, g_attn, w_bf):
    batch, seq, _ = q.shape
    n_blk = seq // BLOCK
    main = lambda w: pl.BlockSpec((1, _TQ, w), lambda b, i: (b, i, 0))
    prev = pl.BlockSpec((1, BLOCK, KV_WIDTH), lambda b, i: (b, jnp.maximum(i * _SUB - 1, 0), 0))
    nxt = pl.BlockSpec((1, BLOCK, KV_WIDTH), lambda b, i: (b, jnp.minimum((i + 1) * _SUB, n_blk - 1), 0))
    full = lambda a: pl.BlockSpec(a.shape, lambda b, i: (0,) * a.ndim)
    return pl.pallas_call(
        _attn_kernel,
        grid=(batch, seq // _TQ),
        in_specs=[main(ATTN_WIDTH), prev, main(KV_WIDTH), nxt, prev, main(KV_WIDTH), nxt,
                  full(k_meta), full(v_meta), full(sink_rows), main(D_MODEL), full(w_bf)],
        out_specs=main(D_MODEL),
        out_shape=jax.ShapeDtypeStruct((batch, seq, D_MODEL), BF16),
        scratch_shapes=[pltpu.VMEM((N_HEADS * BLOCK, _KEYS), F32),
                        pltpu.VMEM((_TQ, ATTN_WIDTH), BF16),
                        pltpu.VMEM((_SUB * N_KV_HEADS, _GROUP_ROWS, _KEYS), F32),
                        pltpu.VMEM((_SUB * N_KV_HEADS, _GROUP_ROWS, _KEYS), BF16),
                        pltpu.VMEM((_SUB * N_KV_HEADS, _GROUP_ROWS, BLOCK), F32)],
        compiler_params=_params(2),
        name="attention",
    )(q, k, k, k, v, v, v, k_meta, v_meta, sink_rows, g_attn, w_bf)


_TC = 512
_TS = 256
_HALO = SUBLANES


def _interleave_in(dst_ref, src, n_rows, row0=0):
    seg = n_rows // SUBLANES
    for n in range(LRU_BLOCKS):
        for s in range(SUBLANES):
            dst_ref[n, pl.ds(row0 + s, seg, stride=SUBLANES), :] = src(row0 + s * seg, seg, n)


def _interleave_out(write, src_ref, n_rows, row0=0):
    seg = n_rows // SUBLANES
    for n in range(LRU_BLOCKS):
        for s in range(SUBLANES):
            write(row0 + s * seg, seg, n, src_ref[n, pl.ds(row0 + s, seg, stride=SUBLANES), :])


def _lru_gates(n_rows, row0, prev2, prev1, next0, x_ref, cw_ref, cb_ref, wg_ref, bg_ref, lam_ref, a_ref, u_ref):
    seg = n_rows // SUBLANES
    sub = lax.broadcasted_iota(jnp.int32, (SUBLANES, LRU_BLOCK_DIM), 0)
    lam = lam_ref[...]
    decay_scale = (-0.5 * LRU_C * math.log2(math.e)) * (
        jnp.maximum(-lam, 0.0) + jnp.log(1.0 + jnp.exp(-jnp.abs(lam))))
    for n in range(LRU_BLOCKS):
        cols = slice(n * LRU_BLOCK_DIM, (n + 1) * LRU_BLOCK_DIM)
        x = x_ref[n, row0:row0 + n_rows, :]
        group = lambda j: x[j * SUBLANES:(j + 1) * SUBLANES, :]
        e0 = jnp.where(sub == 0, prev2(n), pltpu.roll(group(seg - 2), 1, axis=0))
        e1 = jnp.where(sub == 0, prev1(n), pltpu.roll(group(seg - 1), 1, axis=0))
        e_next = jnp.where(sub == SUBLANES - 1, next0(n), pltpu.roll(group(0), SUBLANES - 1, axis=0))
        ext = jnp.concatenate([e0, e1, x, e_next], axis=0)
        xh = cb_ref[:, cols] + sum(
            cw_ref[t:t + 1, cols] * ext[t * SUBLANES:t * SUBLANES + n_rows, :] for t in range(CONV_WIDTH))
        pre = jnp.dot(xh.astype(BF16), wg_ref[n], preferred_element_type=F32)
        t_a = jnp.tanh(pre[:, :LRU_BLOCK_DIM] + bg_ref[0:1, cols])
        t_x = jnp.tanh(pre[:, LRU_BLOCK_DIM:] + bg_ref[1:2, cols])
        scale = decay_scale[:, cols]
        a = jnp.exp2(t_a * scale + scale)
        y = 1.0 - a * a
        a_ref[n, row0:row0 + n_rows, :] = a
        u_ref[n, row0:row0 + n_rows, :] = (y * lax.rsqrt(jnp.maximum(y, 1e-30))) * ((t_x + 1.0) * xh)


def _lru_scan(n_rows, tiles, reverse, carry_in, a_ref, u_ref, h_ref):
    seg = n_rows // SUBLANES
    unroll = min(32, seg)
    sub = lax.broadcasted_iota(jnp.int32, (SUBLANES, LRU_BLOCK_DIM), 0)
    chains = [(row0, n) for row0 in tiles for n in range(LRU_BLOCKS)]

    def rows(jj, row0):
        j = (seg - 1 - jj) if reverse else jj
        return pl.ds(pl.multiple_of(row0 + j * SUBLANES, SUBLANES), SUBLANES)

    def local(jj, state):
        hs, ps = state
        a = [a_ref[n, rows(jj, row0), :] for row0, n in chains]
        return (tuple(a[c] * hs[c] + u_ref[n, rows(jj, row0), :] for c, (row0, n) in enumerate(chains)),
                tuple(a[c] * ps[c] for c in range(len(chains))))

    zeros = tuple(jnp.zeros((SUBLANES, LRU_BLOCK_DIM), F32) for _ in chains)
    ones = tuple(jnp.ones((SUBLANES, LRU_BLOCK_DIM), F32) for _ in chains)
    h_end, p_end = lax.fori_loop(0, seg, local, (zeros, ones), unroll=unroll)

    seg_in = [None] * len(chains)
    carry = list(carry_in)
    first, last = (SUBLANES - 1, 0) if reverse else (0, SUBLANES - 1)
    for row0 in (reversed(tiles) if reverse else tiles):
        for n in range(LRU_BLOCKS):
            c = chains.index((row0, n))
            p, h = p_end[c], h_end[c]
            for d in (1, 2, 4):
                shift = SUBLANES - d if reverse else d
                ok = (sub < SUBLANES - d) if reverse else (sub >= d)
                h = h + p * jnp.where(ok, pltpu.roll(h, shift, axis=0), 0.0)
                p = p * jnp.where(ok, pltpu.roll(p, shift, axis=0), 1.0)
            seg_out = h + p * carry[n]
            shift = SUBLANES - 1 if reverse else 1
            seg_in[c] = jnp.where(sub == first, carry[n], pltpu.roll(seg_out, shift, axis=0))
            carry[n] = seg_out[last:last + 1, :]

    if h_ref is not None:
        def final(jj, hs):
            new = tuple(a_ref[n, rows(jj, row0), :] * hs[c] + u_ref[n, rows(jj, row0), :]
                        for c, (row0, n) in enumerate(chains))
            for c, (row0, n) in enumerate(chains):
                h_ref[n, rows(jj, row0), :] = new[c]
            return new

        lax.fori_loop(0, seg, final, tuple(seg_in), unroll=unroll)
    return carry


def _lru_kernel(reverse, xr_ref, xp_ref, xn_ref, xm_ref, cw_ref, cb_ref, wg_ref, bg_ref, lam_ref,
                h_ref, carry_ref, x_scr, a_scr, u_scr, h_scr):
    step = pl.program_id(1)
    n_steps = pl.num_programs(1)
    t = (n_steps - 1 - step) if reverse else step
    args = (x_scr, cw_ref, cb_ref, wg_ref, bg_ref, lam_ref, a_scr, u_scr)
    lanes = lambda n: slice(n * LRU_BLOCK_DIM, (n + 1) * LRU_BLOCK_DIM)
    zero_row = lambda n: jnp.zeros((1, LRU_BLOCK_DIM), F32)

    if reverse:
        @pl.when(step == 0)
        def _zero_state():
            carry_ref[...] = jnp.zeros_like(carry_ref)
    else:
        @pl.when(step == 0)
        def _meta_state():
            _interleave_in(x_scr, lambda r0, nr, n: xm_ref[r0:r0 + nr, lanes(n)], N_META)
            _lru_gates(N_META, 0, zero_row, zero_row, lambda n: xr_ref[0, 0:1, lanes(n)], *args)
            state = _lru_scan(N_META, [0], False, [zero_row(n) for n in range(LRU_BLOCKS)], a_scr, u_scr, None)
            for n in range(LRU_BLOCKS):
                carry_ref[0:1, lanes(n)] = state[n]

    def before(row):
        return lambda n: jnp.where(t == 0, xm_ref[N_META - _HALO + row:N_META - _HALO + row + 1, lanes(n)],
                                   xp_ref[0, row:row + 1, lanes(n)])

    after = lambda n: jnp.where(t == n_steps - 1, 0.0, xn_ref[0, 0:1, lanes(n)])
    inside = lambda row: (lambda n: xr_ref[0, row:row + 1, lanes(n)])
    tiles = list(range(0, _TC, _TS))
    for row0 in tiles:
        _interleave_in(x_scr, lambda r0, nr, n: xr_ref[0, r0:r0 + nr, lanes(n)], _TS, row0)
        prev2, prev1 = (before(_HALO - 2), before(_HALO - 1)) if row0 == 0 else (inside(row0 - 2), inside(row0 - 1))
        next0 = after if row0 + _TS == _TC else inside(row0 + _TS)
        _lru_gates(_TS, row0, prev2, prev1, next0, *args)
    state = _lru_scan(_TS, tiles, reverse, [carry_ref[0:1, lanes(n)] for n in range(LRU_BLOCKS)],
                      a_scr, u_scr, h_scr)
    for n in range(LRU_BLOCKS):
        carry_ref[0:1, lanes(n)] = state[n]

    def write(r0, nr, n, rows):
        h_ref[0, r0:r0 + nr, lanes(n)] = rows.astype(h_ref.dtype)

    for row0 in tiles:
        _interleave_out(write, h_scr, _TS, row0)


def _lru(xr, xr_meta, conv_w, conv_b, wg_bf, bg, lam, reverse):
    batch, seq, _ = xr.shape
    n_steps = seq // _TC
    n_halo = seq // _HALO
    per_tile = _TC // _HALO
    tile = (lambda s: n_steps - 1 - s) if reverse else (lambda s: s)
    main = pl.BlockSpec((1, _TC, LRU_WIDTH), lambda b, s: (b, tile(s), 0))
    before = pl.BlockSpec((1, _HALO, LRU_WIDTH), lambda b, s: (b, jnp.maximum(tile(s) * per_tile - 1, 0), 0))
    after = pl.BlockSpec((1, _HALO, LRU_WIDTH),
                         lambda b, s: (b, jnp.minimum((tile(s) + 1) * per_tile, n_halo - 1), 0))
    full = lambda a: pl.BlockSpec(a.shape, lambda b, s: (0,) * a.ndim)
    return pl.pallas_call(
        functools.partial(_lru_kernel, reverse),
        grid=(batch, n_steps),
        in_specs=[main, before, after, full(xr_meta), full(conv_w), full(conv_b), full(wg_bf), full(bg),
                  full(lam)],
        out_specs=main,
        out_shape=jax.ShapeDtypeStruct((batch, seq, LRU_WIDTH), BF16),
        scratch_shapes=[pltpu.VMEM((SUBLANES, LRU_WIDTH), F32)]
        + [pltpu.VMEM((LRU_BLOCKS, _TC, LRU_BLOCK_DIM), F32)] * 4,
        compiler_params=_params(2),
        name="lru_bwd" if reverse else "lru_fwd",
    )(xr, xr, xr, xr_meta, conv_w, conv_b, wg_bf, bg, lam)


_ROUTER_LANES = LANES
_MERGE_ROWS = 256


def _split_dot(a, b_hi, b_lo):
    a_hi = a.astype(BF16)
    a_lo = (a - a_hi.astype(F32)).astype(BF16)
    both = jnp.dot(a_hi, jnp.concatenate([b_hi, b_lo], axis=1), preferred_element_type=F32)
    return (both[:, :_ROUTER_LANES]
            + (jnp.dot(a_lo, b_hi, preferred_element_type=F32) + both[:, _ROUTER_LANES:]))


def _merge_kernel(x_ref, hf_ref, hb_ref, gy_ref, ga_ref, gr_ref, wrec_ref, wout_ref, g_ref,
                  wr_hi_ref, wr_lo_ref, br_ref, h1_ref, n2_ref, gates_ref):
    subs = [slice(r0, r0 + _MERGE_ROWS) for r0 in range(0, x_ref.shape[0], _MERGE_ROWS)]
    rec, n2 = {}, {}
    for k, rows in enumerate(subs):
        rec_in = (hf_ref[rows, :] + hb_ref[rows, :]) * gy_ref[rows, :]
        rec[k] = jnp.dot(rec_in, wrec_ref[...], preferred_element_type=F32)
    for k, rows in enumerate(subs):
        mix = (ga_ref[rows, :].astype(F32) + gr_ref[rows, :].astype(F32) * rec[k]).astype(BF16)
        h1 = x_ref[rows, :] + jnp.dot(mix, wout_ref[...], preferred_element_type=F32)
        h1_ref[rows, :] = h1
        n2[k] = _rms_norm(h1, g_ref[...])
        n2_ref[rows, :] = n2[k].astype(BF16)
    for k, rows in enumerate(subs):
        _route(rows, n2[k], wr_hi_ref, wr_lo_ref, br_ref, gates_ref)


def _route(rows, n2, wr_hi_ref, wr_lo_ref, br_ref, gates_ref):
    logits = _split_dot(n2, wr_hi_ref[...], wr_lo_ref[...]) + br_ref[...]
    lane_i = lax.broadcasted_iota(jnp.int32, logits.shape, 1)
    lane = lane_i.astype(F32)
    first = lambda mask: jnp.min(jnp.where(mask, lane, float(_ROUTER_LANES)), axis=-1, keepdims=True)
    lg = jnp.where(lane < N_GROUPS, logits, -jnp.inf)
    g_max = jnp.max(lg, axis=-1, keepdims=True)
    g_top_p = 1.0 / jnp.sum(jnp.exp(lg - g_max), axis=-1, keepdims=True)
    g_idx = first(lg == g_max)
    e = lane_i - N_GROUPS
    e_group = jnp.right_shift(e, int(math.log2(EXPERTS_PER_GROUP))).astype(F32)
    in_group = (e >= 0) & (e < N_EXPERTS) & (e_group == g_idx)
    le = jnp.where(in_group, logits, -jnp.inf)
    m1 = jnp.max(le, axis=-1, keepdims=True)
    i1 = first(le == m1)
    le2 = jnp.where(lane == i1, -jnp.inf, le)
    m2 = jnp.max(le2, axis=-1, keepdims=True)
    i2 = first(le2 == m2)
    e2 = jnp.exp(m2 - m1)
    w1 = g_top_p / (1.0 + e2)
    w2 = g_top_p * e2 / (1.0 + e2)
    gates_ref[rows, :] = jnp.where(lane == i1, w1, 0.0) + jnp.where(lane == i2, w2, 0.0)


def _merge(x2, hf, hb, gy, ga, gr, wrec_bf, wout_bf, g, wr_hi, wr_lo, br, tm):
    n_rows = x2.shape[0]
    row = lambda w: pl.BlockSpec((tm, w), lambda i: (i, 0))
    full = lambda a: pl.BlockSpec(a.shape, lambda i: (0,) * a.ndim)
    return pl.pallas_call(
        _merge_kernel,
        grid=(n_rows // tm,),
        in_specs=[row(D_MODEL)] * 6 + [full(wrec_bf), full(wout_bf), full(g), full(wr_hi), full(wr_lo), full(br)],
        out_specs=(row(D_MODEL), row(D_MODEL), row(_ROUTER_LANES)),
        out_shape=(jax.ShapeDtypeStruct((n_rows, D_MODEL), F32),
                   jax.ShapeDtypeStruct((n_rows, D_MODEL), BF16),
                   jax.ShapeDtypeStruct((n_rows, _ROUTER_LANES), F32)),
        compiler_params=_params(1),
        name="merge",
    )(x2, hf, hb, gy, ga, gr, wrec_bf, wout_bf, g, wr_hi, wr_lo, br)


_TT = 512
_CHUNK = 16
_TM = 512
_SLOTS = 1280
_SLOT_CHUNK = 256
_BIG = 1.0e6


def _moe_plan(gates, n_tiles):
    i32 = jnp.int32
    sel = gates[:, N_GROUPS:N_GROUPS + N_EXPERTS] > 0.0
    cnt = jnp.sum(sel.reshape(n_tiles, _TT, N_EXPERTS), axis=1, dtype=i32)
    padc = (cnt + _CHUNK - 1) // _CHUNK * _CHUNK
    lstart = jnp.cumsum(padc, axis=1) - padc
    tot = jnp.sum(padc, axis=0)
    ntile = (tot + _TM - 1) // _TM
    tile_end = jnp.cumsum(ntile)
    base = (tile_end - ntile) * _TM
    roff = base[None, :] + jnp.cumsum(padc, axis=0) - padc
    n_active = tile_end[-1]
    max_tiles = (2 * n_tiles * _TT + n_tiles * N_EXPERTS * (_CHUNK - 1)) // _TM + N_EXPERTS
    g = jnp.minimum(jnp.arange(max_tiles, dtype=i32), n_active - 1)
    tile_expert = jnp.sum(g[:, None] >= tile_end[None, :], axis=1, dtype=i32)
    lstart_vec = jnp.zeros((n_tiles, 1, _ROUTER_LANES), F32).at[:, 0, N_GROUPS:N_GROUPS + N_EXPERTS].set(
        lstart.astype(F32))
    has_tiles = ntile > 0
    ids = jnp.arange(N_EXPERTS, dtype=i32)
    later = (ids[None, :] > ids[:, None]) & has_tiles[None, :]
    expert_next = jnp.min(jnp.where(later, ids[None, :], N_EXPERTS), axis=1)
    plan = dict(
        nchunk=(padc // _CHUNK).reshape(-1), lstart=lstart.reshape(-1), roff=roff.reshape(-1),
        tile_chunks=jnp.sum(padc // _CHUNK, axis=1, dtype=i32),
        tail_start=base + tot, tail_chunks=(ntile * _TM - tot) // _CHUNK,
        tile_expert=tile_expert, tile_block=g, n_active=n_active.reshape(1), lstart_vec=lstart_vec,
        expert_slot=(jnp.cumsum(has_tiles.astype(i32)) - 1) % 2,
        expert_next=jnp.where(expert_next == N_EXPERTS, -1, expert_next).astype(i32))
    return plan, max_tiles


def _slot_positions(gates, lstart_vec):
    sel = gates > 0.0
    r = lax.broadcasted_iota(jnp.int32, (_TT, _TT), 0)
    c = lax.broadcasted_iota(jnp.int32, (_TT, _TT), 1)
    before = (c < r).astype(BF16)
    rank = jnp.dot(before, sel.astype(BF16), preferred_element_type=F32)
    return sel, rank + lstart_vec


_RUN_BITS = (2 * _TT + N_EXPERTS * (_CHUNK - 1)) // _CHUNK


def _for_each_piece(n_chunks, fn):
    for bit in reversed(range(_RUN_BITS.bit_length())):
        @pl.when((n_chunks >> bit) & 1 == 1)
        def _piece(bit=bit):
            fn((n_chunks >> (bit + 1)) << (bit + 1), 1 << bit)


def _dispatch_kernel(nchunk_ref, lstart_ref, roff_ref, tchunks_ref, tail_start_ref, tail_chunks_ref, n_active_ref,
                     n2_ref, gates_ref, lvec_ref, xs_ref, xloc_ref, zero_ref, sem, zsem):
    i = pl.program_id(0)
    last = pl.num_programs(0) - 1
    buf = i % 2

    def run_copy(b, src0, dst0, first, chunks):
        src = pl.multiple_of(src0 + first * _CHUNK, _CHUNK)
        dst = pl.multiple_of(dst0 + first * _CHUNK, _CHUNK)
        rows = chunks * _CHUNK
        return pltpu.make_async_copy(xloc_ref.at[b, pl.ds(src, rows), :], xs_ref.at[pl.ds(dst, rows), :], sem.at[b])

    def wait_step(b, step):
        _for_each_piece(tchunks_ref[step], lambda first, chunks: run_copy(b, 0, 0, 0, chunks).wait())

    @pl.when(i >= 2)
    def _buffer_free():
        wait_step(buf, i - 2)

    sel, pos = _slot_positions(gates_ref[...], lvec_ref[0])
    lo = jnp.min(jnp.where(sel, pos, _BIG).T, axis=0, keepdims=True)
    hi = jnp.max(jnp.where(sel, pos, -1.0).T, axis=0, keepdims=True)
    slot = lax.broadcasted_iota(jnp.int32, (_SLOTS, _TT), 0).astype(F32)
    onehot = ((slot == lo) | (slot == hi)).astype(BF16)
    xloc_ref[buf] = jnp.dot(onehot, n2_ref[...], preferred_element_type=F32).astype(BF16)

    for e in range(N_EXPERTS):
        idx = i * N_EXPERTS + e
        src0, dst0 = lstart_ref[idx], roff_ref[idx]
        _for_each_piece(nchunk_ref[idx],
                        lambda first, chunks, src0=src0, dst0=dst0: run_copy(buf, src0, dst0, first, chunks).start())

    @pl.when(i == last)
    def _drain():
        @pl.when(i >= 1)
        def _previous():
            wait_step(1 - buf, i - 1)

        wait_step(buf, i)

    @pl.when(i == last)
    def _zero_tails():
        zero_ref[...] = jnp.zeros_like(zero_ref)

        def tail_copy(dst0, c):
            dst = pl.multiple_of(dst0 + c * _CHUNK, _CHUNK)
            return pltpu.make_async_copy(zero_ref.at[pl.ds(0, _CHUNK), :], xs_ref.at[pl.ds(dst, _CHUNK), :], zsem)

        def tile_copy(t):
            dst = pl.multiple_of(t * _TM, _TM)
            return pltpu.make_async_copy(zero_ref, xs_ref.at[pl.ds(dst, _TM), :], zsem)

        n_tiles_total = xs_ref.shape[0] // _TM

        def tstart(t, carry):
            tile_copy(t).start()
            return carry

        def twait(t, carry):
            tile_copy(0).wait()
            return carry

        lax.fori_loop(n_active_ref[0], n_tiles_total, tstart, 0)
        lax.fori_loop(n_active_ref[0], n_tiles_total, twait, 0)

        for e in range(N_EXPERTS):
            dst0 = tail_start_ref[e]

            def zstart(c, carry, dst0=dst0):
                tail_copy(dst0, c).start()
                return carry

            def zwait(c, carry):
                tail_copy(0, 0).wait()
                return carry

            lax.fori_loop(0, tail_chunks_ref[e], zstart, 0)
            lax.fori_loop(0, tail_chunks_ref[e], zwait, 0)


def _dispatch(plan, n2, gates, n_tiles, n_sorted):
    row = lambda w: pl.BlockSpec((_TT, w), lambda i, *_: (i, 0))
    return pl.pallas_call(
        _dispatch_kernel,
        grid_spec=pltpu.PrefetchScalarGridSpec(
            num_scalar_prefetch=7,
            grid=(n_tiles,),
            in_specs=[row(D_MODEL), row(_ROUTER_LANES),
                      pl.BlockSpec((1, 1, _ROUTER_LANES), lambda i, *_: (i, 0, 0))],
            out_specs=pl.BlockSpec(memory_space=pl.ANY),
            scratch_shapes=[pltpu.VMEM((2, _SLOTS, D_MODEL), BF16), pltpu.VMEM((_TM, D_MODEL), BF16),
                            pltpu.SemaphoreType.DMA((2,)), pltpu.SemaphoreType.DMA],
        ),
        out_shape=jax.ShapeDtypeStruct((n_sorted, D_MODEL), BF16),
        compiler_params=_params(1),
        name="moe_dispatch",
    )(plan["nchunk"], plan["lstart"], plan["roff"], plan["tile_chunks"], plan["tail_start"], plan["tail_chunks"],
      plan["n_active"], n2, gates, plan["lstart_vec"])


def _experts_kernel(tile_expert_ref, tile_block_ref, n_active_ref, slot_ref, next_ref,
                    xs_ref, wg_hbm, wu_hbm, wd_hbm, ys_ref, wg_f32, wu_f32, wd_f32, wg_bf, wu_bf, wd_bf, sem):
    g = pl.program_id(0)
    active = g < n_active_ref[0]
    expert = tile_expert_ref[g]
    new_expert = (g == 0) | (expert != tile_expert_ref[jnp.maximum(g - 1, 0)])

    def weight_copies(e, slot):
        return [pltpu.make_async_copy(hbm.at[e], buf.at[slot], sem.at[slot])
                for hbm, buf in ((wg_hbm, wg_f32), (wu_hbm, wu_f32), (wd_hbm, wd_f32))]

    @pl.when(active & new_expert)
    def _switch_expert():
        slot = slot_ref[expert]

        @pl.when(g == 0)
        def _first():
            for copy in weight_copies(expert, slot):
                copy.start()

        for copy in weight_copies(expert, slot):
            copy.wait()
        wg_bf[...] = wg_f32[slot].astype(BF16)
        wu_bf[...] = wu_f32[slot].astype(BF16)
        wd_bf[...] = wd_f32[slot].astype(BF16)

        @pl.when(next_ref[expert] >= 0)
        def _prefetch():
            for copy in weight_copies(next_ref[expert], 1 - slot):
                copy.start()

    @pl.when(active)
    def _ffn():
        xs = xs_ref[...]
        gate = jnp.dot(xs, wg_bf[...], preferred_element_type=F32)
        up = jnp.dot(xs, wu_bf[...], preferred_element_type=F32)
        hidden = (gate * _sigmoid(gate) * up).astype(BF16)
        ys_ref[...] = jnp.dot(hidden, wd_bf[...], preferred_element_type=F32).astype(BF16)

    @pl.when(jnp.logical_not(active))
    def _unused_tile():
        ys_ref[...] = jnp.zeros_like(ys_ref)


def _experts(plan, xs, w_gate, w_up, w_down, max_tiles):
    rows_in = pl.BlockSpec((_TM, D_MODEL), lambda g, te, tb, *_: (tb[g], 0))
    rows_out = pl.BlockSpec((_TM, D_MODEL), lambda g, *_: (g, 0))
    weights = (w_gate, w_up, w_down)
    return pl.pallas_call(
        _experts_kernel,
        grid_spec=pltpu.PrefetchScalarGridSpec(
            num_scalar_prefetch=5,
            grid=(max_tiles,),
            in_specs=[rows_in] + [pl.BlockSpec(memory_space=pl.ANY)] * len(weights),
            out_specs=rows_out,
            scratch_shapes=[pltpu.VMEM((2,) + w.shape[1:], F32) for w in weights]
            + [pltpu.VMEM(w.shape[1:], BF16) for w in weights] + [pltpu.SemaphoreType.DMA((2,))],
        ),
        out_shape=jax.ShapeDtypeStruct(xs.shape, BF16),
        compiler_params=_params(1),
        name="moe_experts",
    )(plan["tile_expert"], plan["tile_block"], plan["n_active"], plan["expert_slot"], plan["expert_next"],
      xs, w_gate, w_up, w_down)


def _combine_kernel(nchunk_ref, lstart_ref, roff_ref, tchunks_ref,
                    gates_ref, lvec_ref, h1_ref, g_ref, ys_ref, o_ref, yloc_ref, sem):
    i = pl.program_id(0)
    buf = i % 2

    def run_copy(b, src0, dst0, first, chunks):
        src = pl.multiple_of(src0 + first * _CHUNK, _CHUNK)
        dst = pl.multiple_of(dst0 + first * _CHUNK, _CHUNK)
        rows = chunks * _CHUNK
        return pltpu.make_async_copy(ys_ref.at[pl.ds(src, rows), :], yloc_ref.at[b, pl.ds(dst, rows), :], sem.at[b])

    def fetch(b, step):
        for e in range(N_EXPERTS):
            idx = step * N_EXPERTS + e
            src0, dst0 = roff_ref[idx], lstart_ref[idx]
            _for_each_piece(nchunk_ref[idx],
                            lambda first, chunks, src0=src0, dst0=dst0: run_copy(b, src0, dst0, first, chunks).start())

    @pl.when(i == 0)
    def _first():
        yloc_ref[...] = jnp.zeros_like(yloc_ref)
        fetch(buf, i)

    @pl.when(i + 1 < pl.num_programs(0))
    def _prefetch():
        fetch(1 - buf, i + 1)

    _for_each_piece(tchunks_ref[i], lambda first, chunks: run_copy(buf, 0, 0, 0, chunks).wait())

    gates = gates_ref[...]
    sel, pos = _slot_positions(gates, lvec_ref[0])
    pos_lo = jnp.where(sel, pos, _BIG)
    pos_hi = jnp.where(sel, pos, -1.0)
    lo = jnp.min(pos_lo, axis=-1, keepdims=True)
    hi = jnp.max(pos_hi, axis=-1, keepdims=True)
    w_lo = jnp.sum(jnp.where(pos_lo == lo, gates, 0.0), axis=-1, keepdims=True)
    w_hi = jnp.where(hi == lo, 0.0, jnp.sum(jnp.where(pos_hi == hi, gates, 0.0), axis=-1, keepdims=True))
    moe = None
    for s0 in range(0, _SLOTS, _SLOT_CHUNK):
        slot = (lax.broadcasted_iota(jnp.int32, (_TT, _SLOT_CHUNK), 1) + s0).astype(F32)
        weights = (jnp.where(slot == lo, w_lo, 0.0) + jnp.where(slot == hi, w_hi, 0.0)).astype(BF16)
        part = jnp.dot(weights, yloc_ref[buf, s0:s0 + _SLOT_CHUNK, :], preferred_element_type=F32)
        moe = part if moe is None else moe + part
    o_ref[...] = _rms_norm(h1_ref[...] + moe, g_ref[...])


def _combine(plan, gates, h1, g, ys, n_tiles):
    row = lambda w: pl.BlockSpec((_TT, w), lambda i, *_: (i, 0))
    return pl.pallas_call(
        _combine_kernel,
        grid_spec=pltpu.PrefetchScalarGridSpec(
            num_scalar_prefetch=4,
            grid=(n_tiles,),
            in_specs=[row(_ROUTER_LANES), pl.BlockSpec((1, 1, _ROUTER_LANES), lambda i, *_: (i, 0, 0)),
                      row(D_MODEL), pl.BlockSpec(g.shape, lambda i, *_: (0, 0)),
                      pl.BlockSpec(memory_space=pl.ANY)],
            out_specs=row(D_MODEL),
            scratch_shapes=[pltpu.VMEM((2, _SLOTS, D_MODEL), BF16), pltpu.SemaphoreType.DMA((2,))],
        ),
        out_shape=jax.ShapeDtypeStruct(h1.shape, F32),
        compiler_params=_params(1),
        name="moe_combine",
    )(plan["nchunk"], plan["lstart"], plan["roff"], plan["tile_chunks"], gates, plan["lstart_vec"], h1, g, ys)


def _moe(n2, gates, h1, w_gate, w_up, w_down, g):
    n_rows = n2.shape[0]
    assert n_rows % _TT == 0 and _SLOTS >= 2 * _TT + N_EXPERTS * (_CHUNK - 1)
    n_tiles = n_rows // _TT
    plan, max_tiles = _moe_plan(gates, n_tiles)
    xs = _dispatch(plan, n2, gates, n_tiles, max_tiles * _TM)
    ys = _experts(plan, xs, w_gate, w_up, w_down, max_tiles)
    return _combine(plan, gates, h1, g, ys, n_tiles)


def kernel(x, meta_tokens, norm_mix_g, w_in, conv_w, conv_b, lru_w_a, lru_b_a, lru_w_x, lru_b_x, lru_lambda, attn_sink, w_attn_branch, w_rec_branch, w_out, norm_ffn_g, w_group, b_group, w_router, b_router, moe_w_gate, moe_w_up, moe_w_down, final_norm_g):
    batch, seq, _ = x.shape
    assert norm_mix_g.shape[0] == 1, "single-layer block"
    assert seq % _TQ == 0 and seq % _TC == 0
    n_rows = batch * seq
    x2 = x.reshape(n_rows, D_MODEL)
    row = lambda a: a.reshape(1, -1).astype(F32)

    w_in_bf = w_in[0].astype(BF16)
    g_mix = row(norm_mix_g[0])
    q, k, v, xr, gy, ga, gr = _in_proj(x2, g_mix, w_in_bf, 512)
    _, k_meta, v_meta, xr_meta, _, _, _ = _in_proj(meta_tokens.astype(F32), g_mix, w_in_bf, N_META)

    sink_rows = jnp.broadcast_to((attn_sink[0].astype(F32) * HEAD_DIM ** 0.5)[:, None, None],
                                 (N_HEADS, BLOCK, BLOCK)).reshape(N_HEADS * BLOCK, BLOCK)
    shape3 = lambda a: a.reshape(batch, seq, a.shape[-1])
    pad_keys = lambda a: jnp.pad(a, ((0, BLOCK - N_META), (0, 0)))
    attn = _attention(shape3(q), shape3(k), shape3(v), pad_keys(k_meta), pad_keys(v_meta), sink_rows, shape3(ga),
                      w_attn_branch[0].astype(BF16))

    h_dirs = []
    for d, reverse in enumerate((False, True)):
        wg = jnp.concatenate([lru_w_a[0, d], lru_w_x[0, d]], axis=-1).astype(BF16)
        bg_half = 0.5 * jnp.stack([lru_b_a[0, d], lru_b_x[0, d]]).astype(F32)
        h_dirs.append(_lru(shape3(xr), xr_meta, 0.5 * conv_w[0].astype(F32), 0.5 * row(conv_b[0]), wg, bg_half,
                           row(lru_lambda[0, d]), reverse))

    w_route = jnp.concatenate([w_group[0], w_router[0]], axis=1).astype(F32)
    w_route = jnp.pad(w_route, ((0, 0), (0, _ROUTER_LANES - w_route.shape[1])))
    wr_hi = w_route.astype(BF16)
    wr_lo = (w_route - wr_hi.astype(F32)).astype(BF16)
    b_route = jnp.pad(jnp.concatenate([b_group[0], b_router[0]]).astype(F32),
                      (0, _ROUTER_LANES - N_GROUPS - N_EXPERTS)).reshape(1, _ROUTER_LANES)
    h1, n2, gates = _merge(x2, h_dirs[0].reshape(n_rows, LRU_WIDTH), h_dirs[1].reshape(n_rows, LRU_WIDTH),
                           gy, attn.reshape(n_rows, D_MODEL), gr,
                           w_rec_branch[0].astype(BF16), w_out[0].astype(BF16), row(norm_ffn_g[0]),
                           wr_hi, wr_lo, b_route, 512)

    out = _moe(n2, gates, h1, moe_w_gate[0].astype(F32), moe_w_up[0].astype(F32), moe_w_down[0].astype(F32),
               row(final_norm_g))
    return out.reshape(batch, seq, D_MODEL)
```

```python
import functools
import math

import jax
import jax.numpy as jnp
from jax import lax
from jax.experimental import pallas as pl
from jax.experimental.pallas import tpu as pltpu

D_MODEL = 1024
N_META = 16
N_HEADS = 8
N_KV_HEADS = 2
HEAD_DIM = 128
Q_PER_KV = N_HEADS // N_KV_HEADS
ATTN_WIDTH = N_HEADS * HEAD_DIM
KV_WIDTH = N_KV_HEADS * HEAD_DIM
WINDOW = 128
BLOCK = 128
LRU_WIDTH = D_MODEL
LRU_BLOCKS = 8
LRU_BLOCK_DIM = LRU_WIDTH // LRU_BLOCKS
CONV_WIDTH = 4
LRU_C = 8.0
N_GROUPS = 4
EXPERTS_PER_GROUP = 4
N_EXPERTS = N_GROUPS * EXPERTS_PER_GROUP
EXPERT_FF = 512
IN_WIDTH = ATTN_WIDTH + 2 * KV_WIDTH + 2 * LRU_WIDTH + 2 * D_MODEL
EPS = 1e-6
NEG_INF = -1e30

LANES = 128
SUBLANES = 8
VMEM_LIMIT = 56 * 1024 * 1024

BF16 = jnp.bfloat16
F32 = jnp.float32


def _params(n_grid_dims):
    return pltpu.CompilerParams(
        dimension_semantics=("arbitrary",) * n_grid_dims,
        vmem_limit_bytes=VMEM_LIMIT,
    )


def _sigmoid(x):
    return 0.5 * jnp.tanh(0.5 * x) + 0.5


def _gelu_tanh(x):
    c = math.sqrt(2.0 / math.pi)
    return 0.5 * x * (1.0 + jnp.tanh(c * (x + 0.044715 * (x * x * x))))


def _rms_norm(xf, g):
    ms = jnp.mean(xf * xf, axis=-1, keepdims=True)
    return xf * lax.rsqrt(ms + EPS) * g


_IN_CHUNK = 512


def _in_proj_kernel(x_ref, g_ref, w_ref, q_ref, k_ref, v_ref, xr_ref, gy_ref, ga_ref, gr_ref):
    n = _rms_norm(x_ref[...], g_ref[...]).astype(BF16)

    def proj(c0, width):
        return jnp.dot(n, w_ref[:, c0:c0 + width], preferred_element_type=F32)

    c = 0
    for j in range(ATTN_WIDTH // _IN_CHUNK):
        q_ref[:, j * _IN_CHUNK:(j + 1) * _IN_CHUNK] = proj(c, _IN_CHUNK).astype(BF16)
        c += _IN_CHUNK
    kv = proj(c, 2 * KV_WIDTH)
    k_ref[...] = kv[:, :KV_WIDTH].astype(BF16)
    v_ref[...] = kv[:, KV_WIDTH:].astype(BF16)
    c += 2 * KV_WIDTH
    for j in range(LRU_WIDTH // _IN_CHUNK):
        xr_ref[:, j * _IN_CHUNK:(j + 1) * _IN_CHUNK] = proj(c, _IN_CHUNK)
        c += _IN_CHUNK
    for j in range(LRU_WIDTH // _IN_CHUNK):
        gy_ref[:, j * _IN_CHUNK:(j + 1) * _IN_CHUNK] = _gelu_tanh(proj(c, _IN_CHUNK)).astype(BF16)
        c += _IN_CHUNK
    for ref in (ga_ref, gr_ref):
        for j in range(D_MODEL // _IN_CHUNK):
            ref[:, j * _IN_CHUNK:(j + 1) * _IN_CHUNK] = _sigmoid(proj(c, _IN_CHUNK)).astype(BF16)
            c += _IN_CHUNK


def _in_proj(x2, g, w_bf, tm):
    n_rows = x2.shape[0]
    row = lambda w: pl.BlockSpec((tm, w), lambda i: (i, 0))
    full = lambda a: pl.BlockSpec(a.shape, lambda i: (0,) * a.ndim)
    out_shapes = (
        jax.ShapeDtypeStruct((n_rows, ATTN_WIDTH), BF16),
        jax.ShapeDtypeStruct((n_rows, KV_WIDTH), BF16),
        jax.ShapeDtypeStruct((n_rows, KV_WIDTH), BF16),
        jax.ShapeDtypeStruct((n_rows, LRU_WIDTH), F32),
        jax.ShapeDtypeStruct((n_rows, LRU_WIDTH), BF16),
        jax.ShapeDtypeStruct((n_rows, D_MODEL), BF16),
        jax.ShapeDtypeStruct((n_rows, D_MODEL), BF16),
    )
    return pl.pallas_call(
        _in_proj_kernel,
        grid=(n_rows // tm,),
        in_specs=[row(D_MODEL), full(g), full(w_bf)],
        out_specs=tuple(row(s.shape[1]) for s in out_shapes),
        out_shape=out_shapes,
        compiler_params=_params(1),
        name="in_proj",
    )(x2, g, w_bf)


def _meta_proj_kernel(x_ref, g_ref, w_ref, wbf_ref, z_ref):
    n = _rms_norm(x_ref[...], g_ref[...]).astype(BF16)
    w = w_ref[...].astype(BF16)
    wbf_ref[...] = w
    z_ref[...] = jnp.dot(n, w, preferred_element_type=F32)


def _meta_proj(meta, g, w):
    chunk = lambda rows: pl.BlockSpec((rows, _IN_CHUNK), lambda j: (0, j))
    full = lambda a: pl.BlockSpec(a.shape, lambda j: (0,) * a.ndim)
    return pl.pallas_call(
        _meta_proj_kernel,
        grid=(w.shape[1] // _IN_CHUNK,),
        in_specs=[full(meta), full(g), chunk(w.shape[0])],
        out_specs=(chunk(w.shape[0]), chunk(meta.shape[0])),
        out_shape=(jax.ShapeDtypeStruct(w.shape, BF16), jax.ShapeDtypeStruct((meta.shape[0], w.shape[1]), F32)),
        compiler_params=_params(1),
        name="meta_proj",
    )(meta, g, w)


_TQ = 512
_SUB = _TQ // BLOCK
_GROUP_ROWS = Q_PER_KV * BLOCK


_KEYS = 4 * BLOCK
_SM_ROWS = 32


def _attn_kernel(q_ref, kp_ref, kc_ref, kn_ref, vp_ref, vc_ref, vn_ref, km_ref, vm_ref,
                 sink_ref, ga_ref, w_ref, o_ref, bias_ref, attn_ref, s_ref, p_ref, m_ref):
    i = pl.program_id(1)
    n_i = pl.num_programs(1)
    scale = HEAD_DIM ** -0.5
    exp_scale = scale * math.log2(math.e)

    @pl.when((pl.program_id(0) == 0) & (i == 0))
    def _init_bias():
        r = lax.broadcasted_iota(jnp.int32, (BLOCK, BLOCK), 0)
        c = lax.broadcasted_iota(jnp.int32, (BLOCK, BLOCK), 1)
        d_prev = (r + BLOCK - c).astype(F32)
        d_cur = jnp.abs(r - c).astype(F32)
        d_next = (c + BLOCK - r).astype(F32)
        for h in range(N_HEADS):
            slope = 2.0 ** (-8.0 * (h + 1.0) / N_HEADS) / scale
            rows = slice(h * BLOCK, (h + 1) * BLOCK)
            bias_ref[rows, 0:BLOCK] = jnp.where(c >= r, -slope * d_prev, NEG_INF / scale)
            bias_ref[rows, BLOCK:2 * BLOCK] = -slope * d_cur
            bias_ref[rows, 2 * BLOCK:3 * BLOCK] = jnp.where(c <= r, -slope * d_next, NEG_INF / scale)
            bias_ref[rows, 3 * BLOCK:4 * BLOCK] = jnp.where(c < N_META, 0.0, NEG_INF / scale)

    nt = (((1,), (1,)), ((), ()))
    for j in range(_SUB):
        rows = slice(j * BLOCK, (j + 1) * BLOCK)
        q = q_ref[0, rows, :]
        if j == 0:
            k3 = [kp_ref[0], kc_ref[0, 0:2 * BLOCK, :]]
            v3 = [vp_ref[0], vc_ref[0, 0:2 * BLOCK, :]]
        elif j == _SUB - 1:
            k3 = [kc_ref[0, (j - 1) * BLOCK:(j + 1) * BLOCK, :], kn_ref[0]]
            v3 = [vc_ref[0, (j - 1) * BLOCK:(j + 1) * BLOCK, :], vn_ref[0]]
        else:
            k3 = [kc_ref[0, (j - 1) * BLOCK:(j + 2) * BLOCK, :]]
            v3 = [vc_ref[0, (j - 1) * BLOCK:(j + 2) * BLOCK, :]]
        k_cat = jnp.concatenate(k3 + [km_ref[...]], axis=0)
        v_cat = jnp.concatenate(v3 + [vm_ref[...]], axis=0)
        masked = []
        if j == 0:
            masked.append((slice(0, BLOCK), i == 0))
        if j == _SUB - 1:
            masked.append((slice(2 * BLOCK, 3 * BLOCK), i == n_i - 1))
        for g in range(N_KV_HEADS):
            cols = slice(g * HEAD_DIM, (g + 1) * HEAD_DIM)
            row0 = g * _GROUP_ROWS
            qg = jnp.concatenate(
                [q[:, (g * Q_PER_KV + h) * HEAD_DIM:(g * Q_PER_KV + h + 1) * HEAD_DIM] for h in range(Q_PER_KV)],
                axis=0)
            pair = j * N_KV_HEADS + g
            s_ref[pair] = lax.dot_general(qg, k_cat[:, cols], nt, preferred_element_type=F32)
            for mask_cols, mask_on in masked:
                s_ref[pair, :, mask_cols] = jnp.where(mask_on, NEG_INF / scale, s_ref[pair, :, mask_cols])

            chunks = [(slice(c * _SM_ROWS, (c + 1) * _SM_ROWS), slice(row0 + c * _SM_ROWS, row0 + (c + 1) * _SM_ROWS))
                      for c in range(_GROUP_ROWS // _SM_ROWS)]
            wide = lambda col: jnp.broadcast_to(col, (_SM_ROWS, BLOCK))
            tiled = lambda stat: jnp.concatenate([stat] * (_KEYS // BLOCK), axis=1)
            for r, rb in chunks:
                z = s_ref[pair, r, :] + bias_ref[rb, :]
                m_ref[pair, r, :] = jnp.maximum(wide(jnp.max(z, axis=-1, keepdims=True)), sink_ref[rb, :])
            for r, rb in chunks:
                m = m_ref[pair, r, :]
                p = jnp.exp2((s_ref[pair, r, :] + bias_ref[rb, :] - tiled(m)) * exp_scale)
                denom = wide(jnp.sum(p, axis=-1, keepdims=True)) + jnp.exp2((sink_ref[rb, :] - m) * exp_scale)
                p_ref[pair, r, :] = p.astype(BF16)
                m_ref[pair, r, :] = 1.0 / denom
            o = jnp.dot(p_ref[pair], v_cat[:, cols], preferred_element_type=F32)
            o = (o * m_ref[pair]).astype(BF16)
            for h in range(Q_PER_KV):
                head = g * Q_PER_KV + h
                attn_ref[rows, head * HEAD_DIM:(head + 1) * HEAD_DIM] = o[h * BLOCK:(h + 1) * BLOCK, :]

    proj = jnp.dot(attn_ref[...], w_ref[...], preferred_element_type=F32)
    o_ref[0] = (ga_ref[0].astype(F32) * proj).astype(BF16)


def _attention(q, k, v, k_meta, v_meta, sink_rows, g_attn, w_bf):
    batch, seq, _ = q.shape
    n_blk = seq // BLOCK
    main = lambda w: pl.BlockSpec((1, _TQ, w), lambda b, i: (b, i, 0))
    prev = pl.BlockSpec((1, BLOCK, KV_WIDTH), lambda b, i: (b, jnp.maximum(i * _SUB - 1, 0), 0))
    nxt = pl.BlockSpec((1, BLOCK, KV_WIDTH), lambda b, i: (b, jnp.minimum((i + 1) * _SUB, n_blk - 1), 0))
    full = lambda a: pl.BlockSpec(a.shape, lambda b, i: (0,) * a.ndim)
    return pl.pallas_call(
        _attn_kernel,
        grid=(batch, seq // _TQ),
        in_specs=[main(ATTN_WIDTH), prev, main(KV_WIDTH), nxt, prev, main(KV_WIDTH), nxt,
                  full(k_meta), full(v_meta), full(sink_rows), main(D_MODEL), full(w_bf)],
        out_specs=main(D_MODEL),
        out_shape=jax.ShapeDtypeStruct((batch, seq, D_MODEL), BF16),
        scratch_shapes=[pltpu.VMEM((N_HEADS * BLOCK, _KEYS), F32),
                        pltpu.VMEM((_TQ, ATTN_WIDTH), BF16),
                        pltpu.VMEM((_SUB * N_KV_HEADS, _GROUP_ROWS, _KEYS), F32),
                        pltpu.VMEM((_SUB * N_KV_HEADS, _GROUP_ROWS, _KEYS), BF16),
                        pltpu.VMEM((_SUB * N_KV_HEADS, _GROUP_ROWS, BLOCK), F32)],
        compiler_params=_params(2),
        name="attention",
    )(q, k, k, k, v, v, v, k_meta, v_meta, sink_rows, g_attn, w_bf)


_TC = 512
_TS = 256
_HALO = SUBLANES


def _interleave_in(dst_ref, src, n_rows, row0=0):
    seg = n_rows // SUBLANES
    for n in range(LRU_BLOCKS):
        for s in range(SUBLANES):
            dst_ref[n, pl.ds(row0 + s, seg, stride=SUBLANES), :] = src(row0 + s * seg, seg, n)


def _interleave_out(write, src_ref, n_rows, row0=0):
    seg = n_rows // SUBLANES
    for n in range(LRU_BLOCKS):
        for s in range(SUBLANES):
            write(row0 + s * seg, seg, n, src_ref[n, pl.ds(row0 + s, seg, stride=SUBLANES), :])


def _lru_gates(n_rows, row0, prev2, prev1, next0, x_ref, cw_ref, cb_ref, wg_ref, bg_ref, lam_ref, a_ref, u_ref):
    seg = n_rows // SUBLANES
    sub = lax.broadcasted_iota(jnp.int32, (SUBLANES, LRU_BLOCK_DIM), 0)
    lam = lam_ref[...]
    decay_scale = (-0.5 * LRU_C * math.log2(math.e)) * (
        jnp.maximum(-lam, 0.0) + jnp.log(1.0 + jnp.exp(-jnp.abs(lam))))
    for n in range(LRU_BLOCKS):
        cols = slice(n * LRU_BLOCK_DIM, (n + 1) * LRU_BLOCK_DIM)
        x = x_ref[n, row0:row0 + n_rows, :]
        group = lambda j: x[j * SUBLANES:(j + 1) * SUBLANES, :]
        e0 = jnp.where(sub == 0, prev2(n), pltpu.roll(group(seg - 2), 1, axis=0))
        e1 = jnp.where(sub == 0, prev1(n), pltpu.roll(group(seg - 1), 1, axis=0))
        e_next = jnp.where(sub == SUBLANES - 1, next0(n), pltpu.roll(group(0), SUBLANES - 1, axis=0))
        ext = jnp.concatenate([e0, e1, x, e_next], axis=0)
        xh = cb_ref[:, cols] + sum(
            cw_ref[t:t + 1, cols] * ext[t * SUBLANES:t * SUBLANES + n_rows, :] for t in range(CONV_WIDTH))
        pre = jnp.dot(xh.astype(BF16), wg_ref[n], preferred_element_type=F32)
        t_a = jnp.tanh(pre[:, :LRU_BLOCK_DIM] + bg_ref[0:1, cols])
        t_x = jnp.tanh(pre[:, LRU_BLOCK_DIM:] + bg_ref[1:2, cols])
        scale = decay_scale[:, cols]
        a = jnp.exp2(t_a * scale + scale)
        y = 1.0 - a * a
        a_ref[n, row0:row0 + n_rows, :] = a
        u_ref[n, row0:row0 + n_rows, :] = (y * lax.rsqrt(jnp.maximum(y, 1e-30))) * ((t_x + 1.0) * xh)


def _lru_scan(n_rows, tiles, reverse, carry_in, a_ref, u_ref, h_ref):
    seg = n_rows // SUBLANES
    unroll = min(32, seg)
    sub = lax.broadcasted_iota(jnp.int32, (SUBLANES, LRU_BLOCK_DIM), 0)
    chains = [(row0, n) for row0 in tiles for n in range(LRU_BLOCKS)]

    def rows(jj, row0):
        j = (seg - 1 - jj) if reverse else jj
        return pl.ds(pl.multiple_of(row0 + j * SUBLANES, SUBLANES), SUBLANES)

    def local(jj, state):
        hs, ps = state
        a = [a_ref[n, rows(jj, row0), :] for row0, n in chains]
        return (tuple(a[c] * hs[c] + u_ref[n, rows(jj, row0), :] for c, (row0, n) in enumerate(chains)),
                tuple(a[c] * ps[c] for c in range(len(chains))))

    zeros = tuple(jnp.zeros((SUBLANES, LRU_BLOCK_DIM), F32) for _ in chains)
    ones = tuple(jnp.ones((SUBLANES, LRU_BLOCK_DIM), F32) for _ in chains)
    h_end, p_end = lax.fori_loop(0, seg, local, (zeros, ones), unroll=unroll)

    seg_in = [None] * len(chains)
    carry = list(carry_in)
    first, last = (SUBLANES - 1, 0) if reverse else (0, SUBLANES - 1)
    for row0 in (reversed(tiles) if reverse else tiles):
        for n in range(LRU_BLOCKS):
            c = chains.index((row0, n))
            p, h = p_end[c], h_end[c]
            for d in (1, 2, 4):
                shift = SUBLANES - d if reverse else d
                ok = (sub < SUBLANES - d) if reverse else (sub >= d)
                h = h + p * jnp.where(ok, pltpu.roll(h, shift, axis=0), 0.0)
                p = p * jnp.where(ok, pltpu.roll(p, shift, axis=0), 1.0)
            seg_out = h + p * carry[n]
            shift = SUBLANES - 1 if reverse else 1
            seg_in[c] = jnp.where(sub == first, carry[n], pltpu.roll(seg_out, shift, axis=0))
            carry[n] = seg_out[last:last + 1, :]

    if h_ref is not None:
        def final(jj, hs):
            new = tuple(a_ref[n, rows(jj, row0), :] * hs[c] + u_ref[n, rows(jj, row0), :]
                        for c, (row0, n) in enumerate(chains))
            for c, (row0, n) in enumerate(chains):
                h_ref[n, rows(jj, row0), :] = new[c]
            return new

        lax.fori_loop(0, seg, final, tuple(seg_in), unroll=unroll)
    return carry


def _lru_kernel(reverse, xr_ref, xp_ref, xn_ref, xm_ref, cw_ref, cb_ref, wg_ref, bg_ref, lam_ref,
                h_ref, carry_ref, x_scr, a_scr, u_scr, h_scr):
    step = pl.program_id(1)
    n_steps = pl.num_programs(1)
    t = (n_steps - 1 - step) if reverse else step
    args = (x_scr, cw_ref, cb_ref, wg_ref, bg_ref, lam_ref, a_scr, u_scr)
    lanes = lambda n: slice(n * LRU_BLOCK_DIM, (n + 1) * LRU_BLOCK_DIM)
    zero_row = lambda n: jnp.zeros((1, LRU_BLOCK_DIM), F32)

    if reverse:
        @pl.when(step == 0)
        def _zero_state():
            carry_ref[...] = jnp.zeros_like(carry_ref)
    else:
        @pl.when(step == 0)
        def _meta_state():
            _interleave_in(x_scr, lambda r0, nr, n: xm_ref[r0:r0 + nr, lanes(n)], N_META)
            _lru_gates(N_META, 0, zero_row, zero_row, lambda n: xr_ref[0, 0:1, lanes(n)], *args)
            state = _lru_scan(N_META, [0], False, [zero_row(n) for n in range(LRU_BLOCKS)], a_scr, u_scr, None)
            for n in range(LRU_BLOCKS):
                carry_ref[0:1, lanes(n)] = state[n]

    def before(row):
        return lambda n: jnp.where(t == 0, xm_ref[N_META - _HALO + row:N_META - _HALO + row + 1, lanes(n)],
                                   xp_ref[0, row:row + 1, lanes(n)])

    after = lambda n: jnp.where(t == n_steps - 1, 0.0, xn_ref[0, 0:1, lanes(n)])
    inside = lambda row: (lambda n: xr_ref[0, row:row + 1, lanes(n)])
    tiles = list(range(0, _TC, _TS))
    for row0 in tiles:
        _interleave_in(x_scr, lambda r0, nr, n: xr_ref[0, r0:r0 + nr, lanes(n)], _TS, row0)
        prev2, prev1 = (before(_HALO - 2), before(_HALO - 1)) if row0 == 0 else (inside(row0 - 2), inside(row0 - 1))
        next0 = after if row0 + _TS == _TC else inside(row0 + _TS)
        _lru_gates(_TS, row0, prev2, prev1, next0, *args)
    state = _lru_scan(_TS, tiles, reverse, [carry_ref[0:1, lanes(n)] for n in range(LRU_BLOCKS)],
                      a_scr, u_scr, h_scr)
    for n in range(LRU_BLOCKS):
        carry_ref[0:1, lanes(n)] = state[n]

    def write(r0, nr, n, rows):
        h_ref[0, r0:r0 + nr, lanes(n)] = rows.astype(h_ref.dtype)

    for row0 in tiles:
        _interleave_out(write, h_scr, _TS, row0)


def _lru(xr, xr_meta, conv_w, conv_b, wg_bf, bg, lam, reverse):
    batch, seq, _ = xr.shape
    n_steps = seq // _TC
    n_halo = seq // _HALO
    per_tile = _TC // _HALO
    tile = (lambda s: n_steps - 1 - s) if reverse else (lambda s: s)
    main = pl.BlockSpec((1, _TC, LRU_WIDTH), lambda b, s: (b, tile(s), 0))
    before = pl.BlockSpec((1, _HALO, LRU_WIDTH), lambda b, s: (b, jnp.maximum(tile(s) * per_tile - 1, 0), 0))
    after = pl.BlockSpec((1, _HALO, LRU_WIDTH),
                         lambda b, s: (b, jnp.minimum((tile(s) + 1) * per_tile, n_halo - 1), 0))
    full = lambda a: pl.BlockSpec(a.shape, lambda b, s: (0,) * a.ndim)
    return pl.pallas_call(
        functools.partial(_lru_kernel, reverse),
        grid=(batch, n_steps),
        in_specs=[main, before, after, full(xr_meta), full(conv_w), full(conv_b), full(wg_bf), full(bg),
                  full(lam)],
        out_specs=main,
        out_shape=jax.ShapeDtypeStruct((batch, seq, LRU_WIDTH), BF16),
        scratch_shapes=[pltpu.VMEM((SUBLANES, LRU_WIDTH), F32)]
        + [pltpu.VMEM((LRU_BLOCKS, _TC, LRU_BLOCK_DIM), F32)] * 4,
        compiler_params=_params(2),
        name="lru_bwd" if reverse else "lru_fwd",
    )(xr, xr, xr, xr_meta, conv_w, conv_b, wg_bf, bg, lam)


_ROUTER_LANES = LANES
_MERGE_ROWS = 256


def _split_dot(a, b_hi, b_lo):
    a_hi = a.astype(BF16)
    a_lo = (a - a_hi.astype(F32)).astype(BF16)
    both = jnp.dot(a_hi, jnp.concatenate([b_hi, b_lo], axis=1), preferred_element_type=F32)
    return (both[:, :_ROUTER_LANES]
            + (jnp.dot(a_lo, b_hi, preferred_element_type=F32) + both[:, _ROUTER_LANES:]))


def _merge_kernel(x_ref, hf_ref, hb_ref, gy_ref, ga_ref, gr_ref, wrec_ref, wout_ref, g_ref,
                  wr_hi_ref, wr_lo_ref, br_ref, h1_ref, n2_ref, gates_ref, count_ref):
    subs = [slice(r0, r0 + _MERGE_ROWS) for r0 in range(0, x_ref.shape[0], _MERGE_ROWS)]
    rec, n2 = {}, {}
    for k, rows in enumerate(subs):
        rec_in = (hf_ref[rows, :] + hb_ref[rows, :]) * gy_ref[rows, :]
        rec[k] = jnp.dot(rec_in, wrec_ref[...], preferred_element_type=F32)
    for k, rows in enumerate(subs):
        mix = (ga_ref[rows, :].astype(F32) + gr_ref[rows, :].astype(F32) * rec[k]).astype(BF16)
        h1 = x_ref[rows, :] + jnp.dot(mix, wout_ref[...], preferred_element_type=F32)
        h1_ref[rows, :] = h1
        n2[k] = _rms_norm(h1, g_ref[...])
        n2_ref[rows, :] = n2[k].astype(BF16)
    count_ref[0] = sum(_route(rows, n2[k], wr_hi_ref, wr_lo_ref, br_ref, gates_ref) for k, rows in enumerate(subs))


def _route(rows, n2, wr_hi_ref, wr_lo_ref, br_ref, gates_ref):
    logits = _split_dot(n2, wr_hi_ref[...], wr_lo_ref[...]) + br_ref[...]
    lane_i = lax.broadcasted_iota(jnp.int32, logits.shape, 1)
    lane = lane_i.astype(F32)
    first = lambda mask: jnp.min(jnp.where(mask, lane, float(_ROUTER_LANES)), axis=-1, keepdims=True)
    lg = jnp.where(lane < N_GROUPS, logits, -jnp.inf)
    g_max = jnp.max(lg, axis=-1, keepdims=True)
    g_top_p = 1.0 / jnp.sum(jnp.exp(lg - g_max), axis=-1, keepdims=True)
    g_idx = first(lg == g_max)
    e = lane_i - N_GROUPS
    e_group = jnp.right_shift(e, int(math.log2(EXPERTS_PER_GROUP))).astype(F32)
    in_group = (e >= 0) & (e < N_EXPERTS) & (e_group == g_idx)
    le = jnp.where(in_group, logits, -jnp.inf)
    m1 = jnp.max(le, axis=-1, keepdims=True)
    i1 = first(le == m1)
    le2 = jnp.where(lane == i1, -jnp.inf, le)
    m2 = jnp.max(le2, axis=-1, keepdims=True)
    i2 = first(le2 == m2)
    e2 = jnp.exp(m2 - m1)
    w1 = g_top_p / (1.0 + e2)
    w2 = g_top_p * e2 / (1.0 + e2)
    gates = jnp.where(lane == i1, w1, 0.0) + jnp.where(lane == i2, w2, 0.0)
    gates_ref[rows, :] = gates
    return jnp.sum((gates > 0.0).astype(F32), axis=0, keepdims=True)


def _merge(x2, hf, hb, gy, ga, gr, wrec_bf, wout_bf, g, wr_hi, wr_lo, br, tm):
    n_rows = x2.shape[0]
    row = lambda w: pl.BlockSpec((tm, w), lambda i: (i, 0))
    full = lambda a: pl.BlockSpec(a.shape, lambda i: (0,) * a.ndim)
    return pl.pallas_call(
        _merge_kernel,
        grid=(n_rows // tm,),
        in_specs=[row(D_MODEL)] * 6 + [full(wrec_bf), full(wout_bf), full(g), full(wr_hi), full(wr_lo), full(br)],
        out_specs=(row(D_MODEL), row(D_MODEL), row(_ROUTER_LANES),
                   pl.BlockSpec((1, 1, _ROUTER_LANES), lambda i: (i, 0, 0))),
        out_shape=(jax.ShapeDtypeStruct((n_rows, D_MODEL), F32),
                   jax.ShapeDtypeStruct((n_rows, D_MODEL), BF16),
                   jax.ShapeDtypeStruct((n_rows, _ROUTER_LANES), F32),
                   jax.ShapeDtypeStruct((n_rows // tm, 1, _ROUTER_LANES), F32)),
        compiler_params=_params(1),
        name="merge",
    )(x2, hf, hb, gy, ga, gr, wrec_bf, wout_bf, g, wr_hi, wr_lo, br)


_TT = 512
_CHUNK = 16
_TM = 512
_SLOTS = 1280
_SLOT_CHUNK = 256
_BIG = 1.0e6


def _moe_plan(counts, n_tiles):
    i32 = jnp.int32
    cnt = counts[:, 0, N_GROUPS:N_GROUPS + N_EXPERTS].astype(i32)
    padc = (cnt + _CHUNK - 1) // _CHUNK * _CHUNK
    lstart = jnp.cumsum(padc, axis=1) - padc
    tot = jnp.sum(padc, axis=0)
    ntile = (tot + _TM - 1) // _TM
    tile_end = jnp.cumsum(ntile)
    base = (tile_end - ntile) * _TM
    roff = base[None, :] + jnp.cumsum(padc, axis=0) - padc
    n_active = tile_end[-1]
    max_tiles = (2 * n_tiles * _TT + n_tiles * N_EXPERTS * (_CHUNK - 1)) // _TM + N_EXPERTS
    g = jnp.minimum(jnp.arange(max_tiles, dtype=i32), n_active - 1)
    tile_expert = jnp.sum(g[:, None] >= tile_end[None, :], axis=1, dtype=i32)
    lstart_vec = jnp.pad(lstart.astype(F32), ((0, 0), (N_GROUPS, _ROUTER_LANES - N_GROUPS - N_EXPERTS)))[:, None, :]
    has_tiles = ntile > 0
    ids = jnp.arange(N_EXPERTS, dtype=i32)
    later = (ids[None, :] > ids[:, None]) & has_tiles[None, :]
    expert_next = jnp.min(jnp.where(later, ids[None, :], N_EXPERTS), axis=1)
    plan = dict(
        nchunk=(padc // _CHUNK).reshape(-1), lstart=lstart.reshape(-1), roff=roff.reshape(-1),
        tile_chunks=jnp.sum(padc // _CHUNK, axis=1, dtype=i32),
        tail_start=base + tot, tail_chunks=(ntile * _TM - tot) // _CHUNK,
        tile_expert=tile_expert, tile_block=g, n_active=n_active.reshape(1), lstart_vec=lstart_vec,
        expert_slot=(jnp.cumsum(has_tiles.astype(i32)) - 1) % 2,
        expert_next=jnp.where(expert_next == N_EXPERTS, -1, expert_next).astype(i32))
    return plan, max_tiles


def _slot_positions(gates, lstart_vec):
    sel = gates > 0.0
    r = lax.broadcasted_iota(jnp.int32, (_TT, _TT), 0)
    c = lax.broadcasted_iota(jnp.int32, (_TT, _TT), 1)
    before = (c < r).astype(BF16)
    rank = jnp.dot(before, sel.astype(BF16), preferred_element_type=F32)
    return sel, rank + lstart_vec


_TILE_CHUNKS = (2 * _TT + N_EXPERTS * (_CHUNK - 1)) // _CHUNK
_RUN_CHUNKS = (_TT + _CHUNK - 1) // _CHUNK


def _for_each_piece(n_chunks, most, fn):
    for bit in reversed(range(most.bit_length())):
        @pl.when((n_chunks >> bit) & 1 == 1)
        def _piece(bit=bit):
            fn((n_chunks >> (bit + 1)) << (bit + 1), 1 << bit)


def _dispatch_kernel(nchunk_ref, lstart_ref, roff_ref, tchunks_ref, tail_start_ref, tail_chunks_ref, n_active_ref,
                     n2_ref, gates_ref, lvec_ref, xs_ref, xloc_ref, zero_ref, sem, zsem):
    i = pl.program_id(0)
    last = pl.num_programs(0) - 1
    buf = i % 2

    def run_copy(b, src0, dst0, first, chunks):
        src = pl.multiple_of(src0 + first * _CHUNK, _CHUNK)
        dst = pl.multiple_of(dst0 + first * _CHUNK, _CHUNK)
        rows = chunks * _CHUNK
        return pltpu.make_async_copy(xloc_ref.at[b, pl.ds(src, rows), :], xs_ref.at[pl.ds(dst, rows), :], sem.at[b])

    def wait_step(b, step):
        _for_each_piece(tchunks_ref[step], _TILE_CHUNKS, lambda first, chunks: run_copy(b, 0, 0, 0, chunks).wait())

    @pl.when(i >= 2)
    def _buffer_free():
        wait_step(buf, i - 2)

    sel, pos = _slot_positions(gates_ref[...], lvec_ref[0])
    lo = jnp.min(jnp.where(sel, pos, _BIG).T, axis=0, keepdims=True)
    hi = jnp.max(jnp.where(sel, pos, -1.0).T, axis=0, keepdims=True)
    slot = lax.broadcasted_iota(jnp.int32, (_SLOTS, _TT), 0).astype(F32)
    onehot = ((slot == lo) | (slot == hi)).astype(BF16)
    xloc_ref[buf] = jnp.dot(onehot, n2_ref[...], preferred_element_type=F32).astype(BF16)

    for e in range(N_EXPERTS):
        idx = i * N_EXPERTS + e
        src0, dst0 = lstart_ref[idx], roff_ref[idx]
        _for_each_piece(nchunk_ref[idx], _RUN_CHUNKS,
                        lambda first, chunks, src0=src0, dst0=dst0: run_copy(buf, src0, dst0, first, chunks).start())

    @pl.when(i == last)
    def _drain():
        @pl.when(i >= 1)
        def _previous():
            wait_step(1 - buf, i - 1)

        wait_step(buf, i)

    @pl.when(i == last)
    def _zero_tails():
        zero_ref[...] = jnp.zeros_like(zero_ref)

        def tail_copy(dst0, c):
            dst = pl.multiple_of(dst0 + c * _CHUNK, _CHUNK)
            return pltpu.make_async_copy(zero_ref.at[pl.ds(0, _CHUNK), :], xs_ref.at[pl.ds(dst, _CHUNK), :], zsem)

        def tile_copy(t):
            dst = pl.multiple_of(t * _TM, _TM)
            return pltpu.make_async_copy(zero_ref, xs_ref.at[pl.ds(dst, _TM), :], zsem)

        n_tiles_total = xs_ref.shape[0] // _TM

        def tstart(t, carry):
            tile_copy(t).start()
            return carry

        def twait(t, carry):
            tile_copy(0).wait()
            return carry

        lax.fori_loop(n_active_ref[0], n_tiles_total, tstart, 0)
        lax.fori_loop(n_active_ref[0], n_tiles_total, twait, 0)

        for e in range(N_EXPERTS):
            dst0 = tail_start_ref[e]

            def zstart(c, carry, dst0=dst0):
                tail_copy(dst0, c).start()
                return carry

            def zwait(c, carry):
                tail_copy(0, 0).wait()
                return carry

            lax.fori_loop(0, tail_chunks_ref[e], zstart, 0)
            lax.fori_loop(0, tail_chunks_ref[e], zwait, 0)


def _dispatch(plan, n2, gates, n_tiles, n_sorted):
    row = lambda w: pl.BlockSpec((_TT, w), lambda i, *_: (i, 0))
    return pl.pallas_call(
        _dispatch_kernel,
        grid_spec=pltpu.PrefetchScalarGridSpec(
            num_scalar_prefetch=7,
            grid=(n_tiles,),
            in_specs=[row(D_MODEL), row(_ROUTER_LANES),
                      pl.BlockSpec((1, 1, _ROUTER_LANES), lambda i, *_: (i, 0, 0))],
            out_specs=pl.BlockSpec(memory_space=pl.ANY),
            scratch_shapes=[pltpu.VMEM((2, _SLOTS, D_MODEL), BF16), pltpu.VMEM((_TM, D_MODEL), BF16),
                            pltpu.SemaphoreType.DMA((2,)), pltpu.SemaphoreType.DMA],
        ),
        out_shape=jax.ShapeDtypeStruct((n_sorted, D_MODEL), BF16),
        compiler_params=_params(1),
        name="moe_dispatch",
    )(plan["nchunk"], plan["lstart"], plan["roff"], plan["tile_chunks"], plan["tail_start"], plan["tail_chunks"],
      plan["n_active"], n2, gates, plan["lstart_vec"])


def _experts_kernel(tile_expert_ref, tile_block_ref, n_active_ref, slot_ref, next_ref,
                    xs_ref, wg_hbm, wu_hbm, wd_hbm, ys_ref, wg_f32, wu_f32, wd_f32, wg_bf, wu_bf, wd_bf, sem):
    g = pl.program_id(0)
    active = g < n_active_ref[0]
    expert = tile_expert_ref[g]
    new_expert = (g == 0) | (expert != tile_expert_ref[jnp.maximum(g - 1, 0)])

    def weight_copies(e, slot):
        return [pltpu.make_async_copy(hbm.at[e], buf.at[slot], sem.at[slot])
                for hbm, buf in ((wg_hbm, wg_f32), (wu_hbm, wu_f32), (wd_hbm, wd_f32))]

    @pl.when(active & new_expert)
    def _switch_expert():
        slot = slot_ref[expert]

        @pl.when(g == 0)
        def _first():
            for copy in weight_copies(expert, slot):
                copy.start()

        for copy in weight_copies(expert, slot):
            copy.wait()
        wg_bf[...] = wg_f32[slot].astype(BF16)
        wu_bf[...] = wu_f32[slot].astype(BF16)
        wd_bf[...] = wd_f32[slot].astype(BF16)

        @pl.when(next_ref[expert] >= 0)
        def _prefetch():
            for copy in weight_copies(next_ref[expert], 1 - slot):
                copy.start()

    @pl.when(active)
    def _ffn():
        xs = xs_ref[...]
        gate = jnp.dot(xs, wg_bf[...], preferred_element_type=F32)
        up = jnp.dot(xs, wu_bf[...], preferred_element_type=F32)
        hidden = (gate * _sigmoid(gate) * up).astype(BF16)
        ys_ref[...] = jnp.dot(hidden, wd_bf[...], preferred_element_type=F32).astype(BF16)

    @pl.when(jnp.logical_not(active))
    def _unused_tile():
        ys_ref[...] = jnp.zeros_like(ys_ref)


def _experts(plan, xs, w_gate, w_up, w_down, max_tiles):
    rows_in = pl.BlockSpec((_TM, D_MODEL), lambda g, te, tb, *_: (tb[g], 0))
    rows_out = pl.BlockSpec((_TM, D_MODEL), lambda g, *_: (g, 0))
    weights = (w_gate, w_up, w_down)
    return pl.pallas_call(
        _experts_kernel,
        grid_spec=pltpu.PrefetchScalarGridSpec(
            num_scalar_prefetch=5,
            grid=(max_tiles,),
            in_specs=[rows_in] + [pl.BlockSpec(memory_space=pl.ANY)] * len(weights),
            out_specs=rows_out,
            scratch_shapes=[pltpu.VMEM((2,) + w.shape[1:], F32) for w in weights]
            + [pltpu.VMEM(w.shape[1:], BF16) for w in weights] + [pltpu.SemaphoreType.DMA((2,))],
        ),
        out_shape=jax.ShapeDtypeStruct(xs.shape, BF16),
        compiler_params=_params(1),
        name="moe_experts",
    )(plan["tile_expert"], plan["tile_block"], plan["n_active"], plan["expert_slot"], plan["expert_next"],
      xs, w_gate, w_up, w_down)


def _combine_kernel(nchunk_ref, lstart_ref, roff_ref, tchunks_ref,
                    gates_ref, lvec_ref, h1_ref, g_ref, ys_ref, o_ref, yloc_ref, sem):
    i = pl.program_id(0)
    buf = i % 2

    def run_copy(b, src0, dst0, first, chunks):
        src = pl.multiple_of(src0 + first * _CHUNK, _CHUNK)
        dst = pl.multiple_of(dst0 + first * _CHUNK, _CHUNK)
        rows = chunks * _CHUNK
        return pltpu.make_async_copy(ys_ref.at[pl.ds(src, rows), :], yloc_ref.at[b, pl.ds(dst, rows), :], sem.at[b])

    def fetch(b, step):
        for e in range(N_EXPERTS):
            idx = step * N_EXPERTS + e
            src0, dst0 = roff_ref[idx], lstart_ref[idx]
            _for_each_piece(nchunk_ref[idx], _RUN_CHUNKS,
                            lambda first, chunks, src0=src0, dst0=dst0: run_copy(b, src0, dst0, first, chunks).start())

    @pl.when(i == 0)
    def _first():
        yloc_ref[...] = jnp.zeros_like(yloc_ref)
        fetch(buf, i)

    @pl.when(i + 1 < pl.num_programs(0))
    def _prefetch():
        fetch(1 - buf, i + 1)

    _for_each_piece(tchunks_ref[i], _TILE_CHUNKS, lambda first, chunks: run_copy(buf, 0, 0, 0, chunks).wait())

    gates = gates_ref[...]
    sel, pos = _slot_positions(gates, lvec_ref[0])
    pos_lo = jnp.where(sel, pos, _BIG)
    pos_hi = jnp.where(sel, pos, -1.0)
    lo = jnp.min(pos_lo, axis=-1, keepdims=True)
    hi = jnp.max(pos_hi, axis=-1, keepdims=True)
    w_lo = jnp.sum(jnp.where(pos_lo == lo, gates, 0.0), axis=-1, keepdims=True)
    w_hi = jnp.where(hi == lo, 0.0, jnp.sum(jnp.where(pos_hi == hi, gates, 0.0), axis=-1, keepdims=True))
    moe = None
    for s0 in range(0, _SLOTS, _SLOT_CHUNK):
        slot = (lax.broadcasted_iota(jnp.int32, (_TT, _SLOT_CHUNK), 1) + s0).astype(F32)
        weights = (jnp.where(slot == lo, w_lo, 0.0) + jnp.where(slot == hi, w_hi, 0.0)).astype(BF16)
        part = jnp.dot(weights, yloc_ref[buf, s0:s0 + _SLOT_CHUNK, :], preferred_element_type=F32)
        moe = part if moe is None else moe + part
    o_ref[...] = _rms_norm(h1_ref[...] + moe, g_ref[...])


def _combine(plan, gates, h1, g, ys, n_tiles):
    row = lambda w: pl.BlockSpec((_TT, w), lambda i, *_: (i, 0))
    return pl.pallas_call(
        _combine_kernel,
        grid_spec=pltpu.PrefetchScalarGridSpec(
            num_scalar_prefetch=4,
            grid=(n_tiles,),
            in_specs=[row(_ROUTER_LANES), pl.BlockSpec((1, 1, _ROUTER_LANES), lambda i, *_: (i, 0, 0)),
                      row(D_MODEL), pl.BlockSpec(g.shape, lambda i, *_: (0, 0)),
                      pl.BlockSpec(memory_space=pl.ANY)],
            out_specs=row(D_MODEL),
            scratch_shapes=[pltpu.VMEM((2, _SLOTS, D_MODEL), BF16), pltpu.SemaphoreType.DMA((2,))],
        ),
        out_shape=jax.ShapeDtypeStruct(h1.shape, F32),
        compiler_params=_params(1),
        name="moe_combine",
    )(plan["nchunk"], plan["lstart"], plan["roff"], plan["tile_chunks"], gates, plan["lstart_vec"], h1, g, ys)


def _moe(n2, gates, counts, h1, w_gate, w_up, w_down, g):
    n_rows = n2.shape[0]
    assert n_rows % _TT == 0 and _SLOTS >= 2 * _TT + N_EXPERTS * (_CHUNK - 1)
    n_tiles = n_rows // _TT
    assert counts.shape[0] == n_tiles
    plan, max_tiles = _moe_plan(counts, n_tiles)
    xs = _dispatch(plan, n2, gates, n_tiles, max_tiles * _TM)
    ys = _experts(plan, xs, w_gate, w_up, w_down, max_tiles)
    return _combine(plan, gates, h1, g, ys, n_tiles)


def kernel(x, meta_tokens, norm_mix_g, w_in, conv_w, conv_b, lru_w_a, lru_b_a, lru_w_x, lru_b_x, lru_lambda, attn_sink, w_attn_branch, w_rec_branch, w_out, norm_ffn_g, w_group, b_group, w_router, b_router, moe_w_gate, moe_w_up, moe_w_down, final_norm_g):
    batch, seq, _ = x.shape
    assert norm_mix_g.shape[0] == 1, "single-layer block"
    assert seq % _TQ == 0 and seq % _TC == 0
    n_rows = batch * seq
    x2 = x.reshape(n_rows, D_MODEL)
    row = lambda a: a.reshape(1, -1).astype(F32)

    g_mix = row(norm_mix_g[0])
    w_in_bf, z_meta = _meta_proj(meta_tokens.astype(F32), g_mix, w_in[0].astype(F32))
    k_meta = z_meta[:, ATTN_WIDTH:ATTN_WIDTH + KV_WIDTH].astype(BF16)
    v_meta = z_meta[:, ATTN_WIDTH + KV_WIDTH:ATTN_WIDTH + 2 * KV_WIDTH].astype(BF16)
    xr_meta = z_meta[:, ATTN_WIDTH + 2 * KV_WIDTH:ATTN_WIDTH + 2 * KV_WIDTH + LRU_WIDTH]
    q, k, v, xr, gy, ga, gr = _in_proj(x2, g_mix, w_in_bf, 512)

    sink_rows = jnp.broadcast_to((attn_sink[0].astype(F32) * HEAD_DIM ** 0.5)[:, None, None],
                                 (N_HEADS, BLOCK, BLOCK)).reshape(N_HEADS * BLOCK, BLOCK)
    shape3 = lambda a: a.reshape(batch, seq, a.shape[-1])
    pad_keys = lambda a: jnp.pad(a, ((0, BLOCK - N_META), (0, 0)))
    attn = _attention(shape3(q), shape3(k), shape3(v), pad_keys(k_meta), pad_keys(v_meta), sink_rows, shape3(ga),
                      w_attn_branch[0].astype(BF16))

    h_dirs = []
    for d, reverse in enumerate((False, True)):
        wg = jnp.concatenate([lru_w_a[0, d], lru_w_x[0, d]], axis=-1).astype(BF16)
        bg_half = 0.5 * jnp.stack([lru_b_a[0, d], lru_b_x[0, d]]).astype(F32)
        h_dirs.append(_lru(shape3(xr), xr_meta, 0.5 * conv_w[0].astype(F32), 0.5 * row(conv_b[0]), wg, bg_half,
                           row(lru_lambda[0, d]), reverse))

    w_route = jnp.concatenate([w_group[0], w_router[0]], axis=1).astype(F32)
    w_route = jnp.pad(w_route, ((0, 0), (0, _ROUTER_LANES - w_route.shape[1])))
    wr_hi = w_route.astype(BF16)
    wr_lo = (w_route - wr_hi.astype(F32)).astype(BF16)
    b_route = jnp.pad(jnp.concatenate([b_group[0], b_router[0]]).astype(F32),
                      (0, _ROUTER_LANES - N_GROUPS - N_EXPERTS)).reshape(1, _ROUTER_LANES)
    h1, n2, gates, counts = _merge(x2, h_dirs[0].reshape(n_rows, LRU_WIDTH), h_dirs[1].reshape(n_rows, LRU_WIDTH),
                                   gy, attn.reshape(n_rows, D_MODEL), gr,
                                   w_rec_branch[0].astype(BF16), w_out[0].astype(BF16), row(norm_ffn_g[0]),
                                   wr_hi, wr_lo, b_route, _TT)

    out = _moe(n2, gates, counts, h1, moe_w_gate[0].astype(F32), moe_w_up[0].astype(F32),
               moe_w_down[0].astype(F32), row(final_norm_g))
    return out.reshape(batch, seq, D_MODEL)
```

```python
import functools
import math

import jax
import jax.numpy as jnp
from jax import lax
from jax.experimental import pallas as pl
from jax.experimental.pallas import tpu as pltpu

D_MODEL = 1024
N_META = 16
N_HEADS = 8
N_KV_HEADS = 2
HEAD_DIM = 128
Q_PER_KV = N_HEADS // N_KV_HEADS
ATTN_WIDTH = N_HEADS * HEAD_DIM
KV_WIDTH = N_KV_HEADS * HEAD_DIM
WINDOW = 128
BLOCK = 128
LRU_WIDTH = D_MODEL
LRU_BLOCKS = 8
LRU_BLOCK_DIM = LRU_WIDTH // LRU_BLOCKS
CONV_WIDTH = 4
LRU_C = 8.0
N_GROUPS = 4
EXPERTS_PER_GROUP = 4
N_EXPERTS = N_GROUPS * EXPERTS_PER_GROUP
EXPERT_FF = 512
IN_WIDTH = ATTN_WIDTH + 2 * KV_WIDTH + 2 * LRU_WIDTH + 2 * D_MODEL
EPS = 1e-6
NEG_INF = -1e30

LANES = 128
SUBLANES = 8
VMEM_LIMIT = 56 * 1024 * 1024

BF16 = jnp.bfloat16
F32 = jnp.float32


def _params(n_grid_dims):
    return pltpu.CompilerParams(
        dimension_semantics=("arbitrary",) * n_grid_dims,
        vmem_limit_bytes=VMEM_LIMIT,
    )


def _sigmoid(x):
    return 0.5 * jnp.tanh(0.5 * x) + 0.5


def _gelu_tanh(x):
    c = math.sqrt(2.0 / math.pi)
    return 0.5 * x * (1.0 + jnp.tanh(c * (x + 0.044715 * (x * x * x))))


def _rms_norm(xf, g):
    ms = jnp.mean(xf * xf, axis=-1, keepdims=True)
    return xf * lax.rsqrt(ms + EPS) * g


_IN_CHUNK = 512


def _in_proj_kernel(x_ref, g_ref, w_ref, q_ref, k_ref, v_ref, xr_ref, gy_ref, ga_ref, gr_ref):
    n = _rms_norm(x_ref[...], g_ref[...]).astype(BF16)

    def proj(c0, width):
        return jnp.dot(n, w_ref[:, c0:c0 + width], preferred_element_type=F32)

    c = 0
    for j in range(ATTN_WIDTH // _IN_CHUNK):
        q_ref[:, j * _IN_CHUNK:(j + 1) * _IN_CHUNK] = proj(c, _IN_CHUNK).astype(BF16)
        c += _IN_CHUNK
    kv = proj(c, 2 * KV_WIDTH)
    k_ref[...] = kv[:, :KV_WIDTH].astype(BF16)
    v_ref[...] = kv[:, KV_WIDTH:].astype(BF16)
    c += 2 * KV_WIDTH
    for j in range(LRU_WIDTH // _IN_CHUNK):
        xr_ref[:, j * _IN_CHUNK:(j + 1) * _IN_CHUNK] = proj(c, _IN_CHUNK)
        c += _IN_CHUNK
    for j in range(LRU_WIDTH // _IN_CHUNK):
        gy_ref[:, j * _IN_CHUNK:(j + 1) * _IN_CHUNK] = _gelu_tanh(proj(c, _IN_CHUNK)).astype(BF16)
        c += _IN_CHUNK
    for ref in (ga_ref, gr_ref):
        for j in range(D_MODEL // _IN_CHUNK):
            ref[:, j * _IN_CHUNK:(j + 1) * _IN_CHUNK] = _sigmoid(proj(c, _IN_CHUNK)).astype(BF16)
            c += _IN_CHUNK


def _in_proj(x2, g, w_bf, tm):
    n_rows = x2.shape[0]
    row = lambda w: pl.BlockSpec((tm, w), lambda i: (i, 0))
    full = lambda a: pl.BlockSpec(a.shape, lambda i: (0,) * a.ndim)
    out_shapes = (
        jax.ShapeDtypeStruct((n_rows, ATTN_WIDTH), BF16),
        jax.ShapeDtypeStruct((n_rows, KV_WIDTH), BF16),
        jax.ShapeDtypeStruct((n_rows, KV_WIDTH), BF16),
        jax.ShapeDtypeStruct((n_rows, LRU_WIDTH), F32),
        jax.ShapeDtypeStruct((n_rows, LRU_WIDTH), BF16),
        jax.ShapeDtypeStruct((n_rows, D_MODEL), BF16),
        jax.ShapeDtypeStruct((n_rows, D_MODEL), BF16),
    )
    return pl.pallas_call(
        _in_proj_kernel,
        grid=(n_rows // tm,),
        in_specs=[row(D_MODEL), full(g), full(w_bf)],
        out_specs=tuple(row(s.shape[1]) for s in out_shapes),
        out_shape=out_shapes,
        compiler_params=_params(1),
        name="in_proj",
    )(x2, g, w_bf)


def _meta_proj_kernel(x_ref, g_ref, w_ref, wbf_ref, z_ref):
    n = _rms_norm(x_ref[...], g_ref[...]).astype(BF16)
    w = w_ref[...].astype(BF16)
    wbf_ref[...] = w
    z_ref[...] = jnp.dot(n, w, preferred_element_type=F32)


def _meta_proj(meta, g, w):
    chunk = lambda rows: pl.BlockSpec((rows, _IN_CHUNK), lambda j: (0, j))
    full = lambda a: pl.BlockSpec(a.shape, lambda j: (0,) * a.ndim)
    return pl.pallas_call(
        _meta_proj_kernel,
        grid=(w.shape[1] // _IN_CHUNK,),
        in_specs=[full(meta), full(g), chunk(w.shape[0])],
        out_specs=(chunk(w.shape[0]), chunk(meta.shape[0])),
        out_shape=(jax.ShapeDtypeStruct(w.shape, BF16), jax.ShapeDtypeStruct((meta.shape[0], w.shape[1]), F32)),
        compiler_params=_params(1),
        name="meta_proj",
    )(meta, g, w)


_TQ = 512
_SUB = _TQ // BLOCK
_GROUP_ROWS = Q_PER_KV * BLOCK


_KEYS = 4 * BLOCK
_SM_ROWS = 32


def _attn_kernel(q_ref, kp_ref, kc_ref, kn_ref, vp_ref, vc_ref, vn_ref, km_ref, vm_ref,
                 sink_ref, ga_ref, w_ref, o_ref, bias_ref, attn_ref, s_ref, p_ref, m_ref):
    i = pl.program_id(1)
    n_i = pl.num_programs(1)
    scale = HEAD_DIM ** -0.5
    exp_scale = scale * math.log2(math.e)

    @pl.when((pl.program_id(0) == 0) & (i == 0))
    def _init_bias():
        r = lax.broadcasted_iota(jnp.int32, (BLOCK, BLOCK), 0)
        c = lax.broadcasted_iota(jnp.int32, (BLOCK, BLOCK), 1)
        d_prev = (r + BLOCK - c).astype(F32)
        d_cur = jnp.abs(r - c).astype(F32)
        d_next = (c + BLOCK - r).astype(F32)
        for h in range(N_HEADS):
            slope = 2.0 ** (-8.0 * (h + 1.0) / N_HEADS) / scale
            rows = slice(h * BLOCK, (h + 1) * BLOCK)
            bias_ref[rows, 0:BLOCK] = jnp.where(c >= r, -slope * d_prev, NEG_INF / scale)
            bias_ref[rows, BLOCK:2 * BLOCK] = -slope * d_cur
            bias_ref[rows, 2 * BLOCK:3 * BLOCK] = jnp.where(c <= r, -slope * d_next, NEG_INF / scale)
            bias_ref[rows, 3 * BLOCK:4 * BLOCK] = jnp.where(c < N_META, 0.0, NEG_INF / scale)

    nt = (((1,), (1,)), ((), ()))
    for j in range(_SUB):
        rows = slice(j * BLOCK, (j + 1) * BLOCK)
        q = q_ref[0, rows, :]
        if j == 0:
            k3 = [kp_ref[0], kc_ref[0, 0:2 * BLOCK, :]]
            v3 = [vp_ref[0], vc_ref[0, 0:2 * BLOCK, :]]
        elif j == _SUB - 1:
            k3 = [kc_ref[0, (j - 1) * BLOCK:(j + 1) * BLOCK, :], kn_ref[0]]
            v3 = [vc_ref[0, (j - 1) * BLOCK:(j + 1) * BLOCK, :], vn_ref[0]]
        else:
            k3 = [kc_ref[0, (j - 1) * BLOCK:(j + 2) * BLOCK, :]]
            v3 = [vc_ref[0, (j - 1) * BLOCK:(j + 2) * BLOCK, :]]
        k_cat = jnp.concatenate(k3 + [km_ref[...]], axis=0)
        v_cat = jnp.concatenate(v3 + [vm_ref[...]], axis=0)
        masked = []
        if j == 0:
            masked.append((slice(0, BLOCK), i == 0))
        if j == _SUB - 1:
            masked.append((slice(2 * BLOCK, 3 * BLOCK), i == n_i - 1))
        for g in range(N_KV_HEADS):
            cols = slice(g * HEAD_DIM, (g + 1) * HEAD_DIM)
            row0 = g * _GROUP_ROWS
            qg = jnp.concatenate(
                [q[:, (g * Q_PER_KV + h) * HEAD_DIM:(g * Q_PER_KV + h + 1) * HEAD_DIM] for h in range(Q_PER_KV)],
                axis=0)
            pair = j * N_KV_HEADS + g
            s_ref[pair] = lax.dot_general(qg, k_cat[:, cols], nt, preferred_element_type=F32)
            for mask_cols, mask_on in masked:
                s_ref[pair, :, mask_cols] = jnp.where(mask_on, NEG_INF / scale, s_ref[pair, :, mask_cols])

            chunks = [(slice(c * _SM_ROWS, (c + 1) * _SM_ROWS), slice(row0 + c * _SM_ROWS, row0 + (c + 1) * _SM_ROWS))
                      for c in range(_GROUP_ROWS // _SM_ROWS)]
            wide = lambda col: jnp.broadcast_to(col, (_SM_ROWS, BLOCK))
            tiled = lambda stat: jnp.concatenate([stat] * (_KEYS // BLOCK), axis=1)
            for r, rb in chunks:
                z = s_ref[pair, r, :] + bias_ref[rb, :]
                m_ref[pair, r, :] = jnp.maximum(wide(jnp.max(z, axis=-1, keepdims=True)), sink_ref[rb, :])
            for r, rb in chunks:
                m = m_ref[pair, r, :]
                p = jnp.exp2((s_ref[pair, r, :] + bias_ref[rb, :] - tiled(m)) * exp_scale)
                denom = wide(jnp.sum(p, axis=-1, keepdims=True)) + jnp.exp2((sink_ref[rb, :] - m) * exp_scale)
                p_ref[pair, r, :] = p.astype(BF16)
                m_ref[pair, r, :] = 1.0 / denom
            o = jnp.dot(p_ref[pair], v_cat[:, cols], preferred_element_type=F32)
            o = (o * m_ref[pair]).astype(BF16)
            for h in range(Q_PER_KV):
                head = g * Q_PER_KV + h
                attn_ref[rows, head * HEAD_DIM:(head + 1) * HEAD_DIM] = o[h * BLOCK:(h + 1) * BLOCK, :]

    proj = jnp.dot(attn_ref[...], w_ref[...].astype(BF16), preferred_element_type=F32)
    o_ref[0] = (ga_ref[0].astype(F32) * proj).astype(BF16)


def _attention(q, k, v, k_meta, v_meta, sink_rows, g_attn, w_attn):
    batch, seq, _ = q.shape
    n_blk = seq // BLOCK
    main = lambda w: pl.BlockSpec((1, _TQ, w), lambda b, i: (b, i, 0))
    prev = pl.BlockSpec((1, BLOCK, KV_WIDTH), lambda b, i: (b, jnp.maximum(i * _SUB - 1, 0), 0))
    nxt = pl.BlockSpec((1, BLOCK, KV_WIDTH), lambda b, i: (b, jnp.minimum((i + 1) * _SUB, n_blk - 1), 0))
    full = lambda a: pl.BlockSpec(a.shape, lambda b, i: (0,) * a.ndim)
    return pl.pallas_call(
        _attn_kernel,
        grid=(batch, seq // _TQ),
        in_specs=[main(ATTN_WIDTH), prev, main(KV_WIDTH), nxt, prev, main(KV_WIDTH), nxt,
                  full(k_meta), full(v_meta), full(sink_rows), main(D_MODEL), full(w_attn)],
        out_specs=main(D_MODEL),
        out_shape=jax.ShapeDtypeStruct((batch, seq, D_MODEL), BF16),
        scratch_shapes=[pltpu.VMEM((N_HEADS * BLOCK, _KEYS), F32),
                        pltpu.VMEM((_TQ, ATTN_WIDTH), BF16),
                        pltpu.VMEM((_SUB * N_KV_HEADS, _GROUP_ROWS, _KEYS), F32),
                        pltpu.VMEM((_SUB * N_KV_HEADS, _GROUP_ROWS, _KEYS), BF16),
                        pltpu.VMEM((_SUB * N_KV_HEADS, _GROUP_ROWS, BLOCK), F32)],
        compiler_params=_params(2),
        name="attention",
    )(q, k, k, k, v, v, v, k_meta, v_meta, sink_rows, g_attn, w_attn)


_TC = 512
_TS = 256
_HALO = SUBLANES


def _interleave_in(dst_ref, src, n_rows, row0=0):
    seg = n_rows // SUBLANES
    for n in range(LRU_BLOCKS):
        for s in range(SUBLANES):
            dst_ref[n, pl.ds(row0 + s, seg, stride=SUBLANES), :] = src(row0 + s * seg, seg, n)


def _interleave_out(write, src_ref, n_rows, row0=0):
    seg = n_rows // SUBLANES
    for n in range(LRU_BLOCKS):
        for s in range(SUBLANES):
            write(row0 + s * seg, seg, n, src_ref[n, pl.ds(row0 + s, seg, stride=SUBLANES), :])


def _lru_gates(n_rows, row0, prev2, prev1, next0, x_ref, cw_ref, cb_ref, wg_ref, bg_ref, lam_ref, a_ref, u_ref):
    seg = n_rows // SUBLANES
    sub = lax.broadcasted_iota(jnp.int32, (SUBLANES, LRU_BLOCK_DIM), 0)
    lam = lam_ref[...]
    decay_scale = (-0.5 * LRU_C * math.log2(math.e)) * (
        jnp.maximum(-lam, 0.0) + jnp.log(1.0 + jnp.exp(-jnp.abs(lam))))
    for n in range(LRU_BLOCKS):
        cols = slice(n * LRU_BLOCK_DIM, (n + 1) * LRU_BLOCK_DIM)
        x = x_ref[n, row0:row0 + n_rows, :]
        group = lambda j: x[j * SUBLANES:(j + 1) * SUBLANES, :]
        e0 = jnp.where(sub == 0, prev2(n), pltpu.roll(group(seg - 2), 1, axis=0))
        e1 = jnp.where(sub == 0, prev1(n), pltpu.roll(group(seg - 1), 1, axis=0))
        e_next = jnp.where(sub == SUBLANES - 1, next0(n), pltpu.roll(group(0), SUBLANES - 1, axis=0))
        ext = jnp.concatenate([e0, e1, x, e_next], axis=0)
        xh = cb_ref[:, cols] + sum(
            cw_ref[t:t + 1, cols] * ext[t * SUBLANES:t * SUBLANES + n_rows, :] for t in range(CONV_WIDTH))
        pre = jnp.dot(xh.astype(BF16), wg_ref[n], preferred_element_type=F32)
        t_a = jnp.tanh(pre[:, :LRU_BLOCK_DIM] + bg_ref[0:1, cols])
        t_x = jnp.tanh(pre[:, LRU_BLOCK_DIM:] + bg_ref[1:2, cols])
        scale = decay_scale[:, cols]
        a = jnp.exp2(t_a * scale + scale)
        y = 1.0 - a * a
        a_ref[n, row0:row0 + n_rows, :] = a
        u_ref[n, row0:row0 + n_rows, :] = (y * lax.rsqrt(jnp.maximum(y, 1e-30))) * ((t_x + 1.0) * xh)


def _lru_scan(n_rows, tiles, reverse, carry_in, a_ref, u_ref, h_ref):
    seg = n_rows // SUBLANES
    unroll = min(32, seg)
    sub = lax.broadcasted_iota(jnp.int32, (SUBLANES, LRU_BLOCK_DIM), 0)
    chains = [(row0, n) for row0 in tiles for n in range(LRU_BLOCKS)]

    def rows(jj, row0):
        j = (seg - 1 - jj) if reverse else jj
        return pl.ds(pl.multiple_of(row0 + j * SUBLANES, SUBLANES), SUBLANES)

    def local(jj, state):
        hs, ps = state
        a = [a_ref[n, rows(jj, row0), :] for row0, n in chains]
        return (tuple(a[c] * hs[c] + u_ref[n, rows(jj, row0), :] for c, (row0, n) in enumerate(chains)),
                tuple(a[c] * ps[c] for c in range(len(chains))))

    zeros = tuple(jnp.zeros((SUBLANES, LRU_BLOCK_DIM), F32) for _ in chains)
    ones = tuple(jnp.ones((SUBLANES, LRU_BLOCK_DIM), F32) for _ in chains)
    h_end, p_end = lax.fori_loop(0, seg, local, (zeros, ones), unroll=unroll)

    seg_in = [None] * len(chains)
    carry = list(carry_in)
    first, last = (SUBLANES - 1, 0) if reverse else (0, SUBLANES - 1)
    for row0 in (reversed(tiles) if reverse else tiles):
        for n in range(LRU_BLOCKS):
            c = chains.index((row0, n))
            p, h = p_end[c], h_end[c]
            for d in (1, 2, 4):
                shift = SUBLANES - d if reverse else d
                ok = (sub < SUBLANES - d) if reverse else (sub >= d)
                h = h + p * jnp.where(ok, pltpu.roll(h, shift, axis=0), 0.0)
                p = p * jnp.where(ok, pltpu.roll(p, shift, axis=0), 1.0)
            seg_out = h + p * carry[n]
            shift = SUBLANES - 1 if reverse else 1
            seg_in[c] = jnp.where(sub == first, carry[n], pltpu.roll(seg_out, shift, axis=0))
            carry[n] = seg_out[last:last + 1, :]

    if h_ref is not None:
        def final(jj, hs):
            new = tuple(a_ref[n, rows(jj, row0), :] * hs[c] + u_ref[n, rows(jj, row0), :]
                        for c, (row0, n) in enumerate(chains))
            for c, (row0, n) in enumerate(chains):
                h_ref[n, rows(jj, row0), :] = new[c]
            return new

        lax.fori_loop(0, seg, final, tuple(seg_in), unroll=unroll)
    return carry


def _lru_kernel(reverse, xr_ref, xp_ref, xn_ref, xm_ref, cw_ref, cb_ref, wg_ref, bg_ref, lam_ref,
                h_ref, carry_ref, x_scr, a_scr, u_scr, h_scr):
    step = pl.program_id(1)
    n_steps = pl.num_programs(1)
    t = (n_steps - 1 - step) if reverse else step
    args = (x_scr, cw_ref, cb_ref, wg_ref, bg_ref, lam_ref, a_scr, u_scr)
    lanes = lambda n: slice(n * LRU_BLOCK_DIM, (n + 1) * LRU_BLOCK_DIM)
    zero_row = lambda n: jnp.zeros((1, LRU_BLOCK_DIM), F32)

    if reverse:
        @pl.when(step == 0)
        def _zero_state():
            carry_ref[...] = jnp.zeros_like(carry_ref)
    else:
        @pl.when(step == 0)
        def _meta_state():
            _interleave_in(x_scr, lambda r0, nr, n: xm_ref[r0:r0 + nr, lanes(n)], N_META)
            _lru_gates(N_META, 0, zero_row, zero_row, lambda n: xr_ref[0, 0:1, lanes(n)], *args)
            state = _lru_scan(N_META, [0], False, [zero_row(n) for n in range(LRU_BLOCKS)], a_scr, u_scr, None)
            for n in range(LRU_BLOCKS):
                carry_ref[0:1, lanes(n)] = state[n]

    def before(row):
        return lambda n: jnp.where(t == 0, xm_ref[N_META - _HALO + row:N_META - _HALO + row + 1, lanes(n)],
                                   xp_ref[0, row:row + 1, lanes(n)])

    after = lambda n: jnp.where(t == n_steps - 1, 0.0, xn_ref[0, 0:1, lanes(n)])
    inside = lambda row: (lambda n: xr_ref[0, row:row + 1, lanes(n)])
    tiles = list(range(0, _TC, _TS))
    for row0 in tiles:
        _interleave_in(x_scr, lambda r0, nr, n: xr_ref[0, r0:r0 + nr, lanes(n)], _TS, row0)
        prev2, prev1 = (before(_HALO - 2), before(_HALO - 1)) if row0 == 0 else (inside(row0 - 2), inside(row0 - 1))
        next0 = after if row0 + _TS == _TC else inside(row0 + _TS)
        _lru_gates(_TS, row0, prev2, prev1, next0, *args)
    state = _lru_scan(_TS, tiles, reverse, [carry_ref[0:1, lanes(n)] for n in range(LRU_BLOCKS)],
                      a_scr, u_scr, h_scr)
    for n in range(LRU_BLOCKS):
        carry_ref[0:1, lanes(n)] = state[n]

    def write(r0, nr, n, rows):
        h_ref[0, r0:r0 + nr, lanes(n)] = rows.astype(h_ref.dtype)

    for row0 in tiles:
        _interleave_out(write, h_scr, _TS, row0)


def _lru(xr, xr_meta, conv_w, conv_b, wg_bf, bg, lam, reverse):
    batch, seq, _ = xr.shape
    n_steps = seq // _TC
    n_halo = seq // _HALO
    per_tile = _TC // _HALO
    tile = (lambda s: n_steps - 1 - s) if reverse else (lambda s: s)
    main = pl.BlockSpec((1, _TC, LRU_WIDTH), lambda b, s: (b, tile(s), 0))
    before = pl.BlockSpec((1, _HALO, LRU_WIDTH), lambda b, s: (b, jnp.maximum(tile(s) * per_tile - 1, 0), 0))
    after = pl.BlockSpec((1, _HALO, LRU_WIDTH),
                         lambda b, s: (b, jnp.minimum((tile(s) + 1) * per_tile, n_halo - 1), 0))
    full = lambda a: pl.BlockSpec(a.shape, lambda b, s: (0,) * a.ndim)
    return pl.pallas_call(
        functools.partial(_lru_kernel, reverse),
        grid=(batch, n_steps),
        in_specs=[main, before, after, full(xr_meta), full(conv_w), full(conv_b), full(wg_bf), full(bg),
                  full(lam)],
        out_specs=main,
        out_shape=jax.ShapeDtypeStruct((batch, seq, LRU_WIDTH), BF16),
        scratch_shapes=[pltpu.VMEM((SUBLANES, LRU_WIDTH), F32)]
        + [pltpu.VMEM((LRU_BLOCKS, _TC, LRU_BLOCK_DIM), F32)] * 4,
        compiler_params=_params(2),
        name="lru_bwd" if reverse else "lru_fwd",
    )(xr, xr, xr, xr_meta, conv_w, conv_b, wg_bf, bg, lam)


_ROUTER_LANES = LANES
_MERGE_ROWS = 256


def _split_dot(a, b_hi, b_lo):
    a_hi = a.astype(BF16)
    a_lo = (a - a_hi.astype(F32)).astype(BF16)
    both = jnp.dot(a_hi, jnp.concatenate([b_hi, b_lo], axis=1), preferred_element_type=F32)
    return (both[:, :_ROUTER_LANES]
            + (jnp.dot(a_lo, b_hi, preferred_element_type=F32) + both[:, _ROUTER_LANES:]))


def _merge_kernel(x_ref, hf_ref, hb_ref, gy_ref, ga_ref, gr_ref, wrec_ref, wout_ref, g_ref,
                  wr_hi_ref, wr_lo_ref, br_ref, h1_ref, n2_ref, gates_ref, count_ref):
    subs = [slice(r0, r0 + _MERGE_ROWS) for r0 in range(0, x_ref.shape[0], _MERGE_ROWS)]
    rec, n2 = {}, {}
    w_rec, w_out = wrec_ref[...].astype(BF16), wout_ref[...].astype(BF16)
    for k, rows in enumerate(subs):
        rec_in = (hf_ref[rows, :] + hb_ref[rows, :]) * gy_ref[rows, :]
        rec[k] = jnp.dot(rec_in, w_rec, preferred_element_type=F32)
    for k, rows in enumerate(subs):
        mix = (ga_ref[rows, :].astype(F32) + gr_ref[rows, :].astype(F32) * rec[k]).astype(BF16)
        h1 = x_ref[rows, :] + jnp.dot(mix, w_out, preferred_element_type=F32)
        h1_ref[rows, :] = h1
        n2[k] = _rms_norm(h1, g_ref[...])
        n2_ref[rows, :] = n2[k].astype(BF16)
    count_ref[0] = sum(_route(rows, n2[k], wr_hi_ref, wr_lo_ref, br_ref, gates_ref) for k, rows in enumerate(subs))


def _route(rows, n2, wr_hi_ref, wr_lo_ref, br_ref, gates_ref):
    logits = _split_dot(n2, wr_hi_ref[...], wr_lo_ref[...]) + br_ref[...]
    lane_i = lax.broadcasted_iota(jnp.int32, logits.shape, 1)
    lane = lane_i.astype(F32)
    first = lambda mask: jnp.min(jnp.where(mask, lane, float(_ROUTER_LANES)), axis=-1, keepdims=True)
    lg = jnp.where(lane < N_GROUPS, logits, -jnp.inf)
    g_max = jnp.max(lg, axis=-1, keepdims=True)
    g_top_p = 1.0 / jnp.sum(jnp.exp(lg - g_max), axis=-1, keepdims=True)
    g_idx = first(lg == g_max)
    e = lane_i - N_GROUPS
    e_group = jnp.right_shift(e, int(math.log2(EXPERTS_PER_GROUP))).astype(F32)
    in_group = (e >= 0) & (e < N_EXPERTS) & (e_group == g_idx)
    le = jnp.where(in_group, logits, -jnp.inf)
    m1 = jnp.max(le, axis=-1, keepdims=True)
    i1 = first(le == m1)
    le2 = jnp.where(lane == i1, -jnp.inf, le)
    m2 = jnp.max(le2, axis=-1, keepdims=True)
    i2 = first(le2 == m2)
    e2 = jnp.exp(m2 - m1)
    w1 = g_top_p / (1.0 + e2)
    w2 = g_top_p * e2 / (1.0 + e2)
    gates = jnp.where(lane == i1, w1, 0.0) + jnp.where(lane == i2, w2, 0.0)
    gates_ref[rows, :] = gates
    return jnp.sum((gates > 0.0).astype(F32), axis=0, keepdims=True)


def _merge(x2, hf, hb, gy, ga, gr, w_rec, w_out, g, wr_hi, wr_lo, br, tm):
    n_rows = x2.shape[0]
    row = lambda w: pl.BlockSpec((tm, w), lambda i: (i, 0))
    full = lambda a: pl.BlockSpec(a.shape, lambda i: (0,) * a.ndim)
    return pl.pallas_call(
        _merge_kernel,
        grid=(n_rows // tm,),
        in_specs=[row(D_MODEL)] * 6 + [full(w_rec), full(w_out), full(g), full(wr_hi), full(wr_lo), full(br)],
        out_specs=(row(D_MODEL), row(D_MODEL), row(_ROUTER_LANES),
                   pl.BlockSpec((1, 1, _ROUTER_LANES), lambda i: (i, 0, 0))),
        out_shape=(jax.ShapeDtypeStruct((n_rows, D_MODEL), F32),
                   jax.ShapeDtypeStruct((n_rows, D_MODEL), BF16),
                   jax.ShapeDtypeStruct((n_rows, _ROUTER_LANES), F32),
                   jax.ShapeDtypeStruct((n_rows // tm, 1, _ROUTER_LANES), F32)),
        compiler_params=_params(1),
        name="merge",
    )(x2, hf, hb, gy, ga, gr, w_rec, w_out, g, wr_hi, wr_lo, br)


_TT = 512
_CHUNK = 16
_TM = 512
_SLOTS = 1280
_SLOT_CHUNK = 256
_BIG = 1.0e6


def _moe_plan(counts, n_tiles):
    i32 = jnp.int32
    cnt = counts[:, 0, N_GROUPS:N_GROUPS + N_EXPERTS].astype(i32)
    padc = (cnt + _CHUNK - 1) // _CHUNK * _CHUNK
    lstart = jnp.cumsum(padc, axis=1) - padc
    tot = jnp.sum(padc, axis=0)
    ntile = (tot + _TM - 1) // _TM
    tile_end = jnp.cumsum(ntile)
    base = (tile_end - ntile) * _TM
    roff = base[None, :] + jnp.cumsum(padc, axis=0) - padc
    n_active = tile_end[-1]
    max_tiles = (2 * n_tiles * _TT + n_tiles * N_EXPERTS * (_CHUNK - 1)) // _TM + N_EXPERTS
    g = jnp.minimum(jnp.arange(max_tiles, dtype=i32), n_active - 1)
    tile_expert = jnp.sum(g[:, None] >= tile_end[None, :], axis=1, dtype=i32)
    lstart_vec = jnp.pad(lstart.astype(F32), ((0, 0), (N_GROUPS, _ROUTER_LANES - N_GROUPS - N_EXPERTS)))[:, None, :]
    has_tiles = ntile > 0
    ids = jnp.arange(N_EXPERTS, dtype=i32)
    later = (ids[None, :] > ids[:, None]) & has_tiles[None, :]
    expert_next = jnp.min(jnp.where(later, ids[None, :], N_EXPERTS), axis=1)
    plan = dict(
        nchunk=(padc // _CHUNK).reshape(-1), lstart=lstart.reshape(-1), roff=roff.reshape(-1),
        tile_chunks=jnp.sum(padc // _CHUNK, axis=1, dtype=i32),
        tail_start=base + tot, tail_chunks=(ntile * _TM - tot) // _CHUNK,
        tile_expert=tile_expert, tile_block=g, n_active=n_active.reshape(1), lstart_vec=lstart_vec,
        expert_slot=(jnp.cumsum(has_tiles.astype(i32)) - 1) % 2,
        expert_next=jnp.where(expert_next == N_EXPERTS, -1, expert_next).astype(i32))
    return plan, max_tiles


def _slot_positions(gates, lstart_vec):
    sel = gates > 0.0
    r = lax.broadcasted_iota(jnp.int32, (_TT, _TT), 0)
    c = lax.broadcasted_iota(jnp.int32, (_TT, _TT), 1)
    before = (c < r).astype(BF16)
    rank = jnp.dot(before, sel.astype(BF16), preferred_element_type=F32)
    return sel, rank + lstart_vec


_TILE_CHUNKS = (2 * _TT + N_EXPERTS * (_CHUNK - 1)) // _CHUNK
_RUN_CHUNKS = (_TT + _CHUNK - 1) // _CHUNK


def _for_each_piece(n_chunks, most, fn):
    for bit in reversed(range(most.bit_length())):
        @pl.when((n_chunks >> bit) & 1 == 1)
        def _piece(bit=bit):
            fn((n_chunks >> (bit + 1)) << (bit + 1), 1 << bit)


def _dispatch_kernel(nchunk_ref, lstart_ref, roff_ref, tchunks_ref, tail_start_ref, tail_chunks_ref, n_active_ref,
                     n2_ref, gates_ref, lvec_ref, xs_ref, xloc_ref, zero_ref, sem, zsem):
    i = pl.program_id(0)
    last = pl.num_programs(0) - 1
    buf = i % 2

    def run_copy(b, src0, dst0, first, chunks):
        src = pl.multiple_of(src0 + first * _CHUNK, _CHUNK)
        dst = pl.multiple_of(dst0 + first * _CHUNK, _CHUNK)
        rows = chunks * _CHUNK
        return pltpu.make_async_copy(xloc_ref.at[b, pl.ds(src, rows), :], xs_ref.at[pl.ds(dst, rows), :], sem.at[b])

    def wait_step(b, step):
        _for_each_piece(tchunks_ref[step], _TILE_CHUNKS, lambda first, chunks: run_copy(b, 0, 0, 0, chunks).wait())

    @pl.when(i >= 2)
    def _buffer_free():
        wait_step(buf, i - 2)

    sel, pos = _slot_positions(gates_ref[...], lvec_ref[0])
    lo = jnp.min(jnp.where(sel, pos, _BIG).T, axis=0, keepdims=True)
    hi = jnp.max(jnp.where(sel, pos, -1.0).T, axis=0, keepdims=True)
    slot = lax.broadcasted_iota(jnp.int32, (_SLOTS, _TT), 0).astype(F32)
    onehot = ((slot == lo) | (slot == hi)).astype(BF16)
    xloc_ref[buf] = jnp.dot(onehot, n2_ref[...], preferred_element_type=F32).astype(BF16)

    for e in range(N_EXPERTS):
        idx = i * N_EXPERTS + e
        src0, dst0 = lstart_ref[idx], roff_ref[idx]
        _for_each_piece(nchunk_ref[idx], _RUN_CHUNKS,
                        lambda first, chunks, src0=src0, dst0=dst0: run_copy(buf, src0, dst0, first, chunks).start())

    @pl.when(i == last)
    def _drain():
        @pl.when(i >= 1)
        def _previous():
            wait_step(1 - buf, i - 1)

        wait_step(buf, i)

    @pl.when(i == last)
    def _zero_tails():
        zero_ref[...] = jnp.zeros_like(zero_ref)

        def tail_copy(dst0, c):
            dst = pl.multiple_of(dst0 + c * _CHUNK, _CHUNK)
            return pltpu.make_async_copy(zero_ref.at[pl.ds(0, _CHUNK), :], xs_ref.at[pl.ds(dst, _CHUNK), :], zsem)

        def tile_copy(t):
            dst = pl.multiple_of(t * _TM, _TM)
            return pltpu.make_async_copy(zero_ref, xs_ref.at[pl.ds(dst, _TM), :], zsem)

        n_tiles_total = xs_ref.shape[0] // _TM

        def tstart(t, carry):
            tile_copy(t).start()
            return carry

        def twait(t, carry):
            tile_copy(0).wait()
            return carry

        lax.fori_loop(n_active_ref[0], n_tiles_total, tstart, 0)
        lax.fori_loop(n_active_ref[0], n_tiles_total, twait, 0)

        for e in range(N_EXPERTS):
            dst0 = tail_start_ref[e]

            def zstart(c, carry, dst0=dst0):
                tail_copy(dst0, c).start()
                return carry

            def zwait(c, carry):
                tail_copy(0, 0).wait()
                return carry

            lax.fori_loop(0, tail_chunks_ref[e], zstart, 0)
            lax.fori_loop(0, tail_chunks_ref[e], zwait, 0)


def _dispatch(plan, n2, gates, n_tiles, n_sorted):
    row = lambda w: pl.BlockSpec((_TT, w), lambda i, *_: (i, 0))
    return pl.pallas_call(
        _dispatch_kernel,
        grid_spec=pltpu.PrefetchScalarGridSpec(
            num_scalar_prefetch=7,
            grid=(n_tiles,),
            in_specs=[row(D_MODEL), row(_ROUTER_LANES),
                      pl.BlockSpec((1, 1, _ROUTER_LANES), lambda i, *_: (i, 0, 0))],
            out_specs=pl.BlockSpec(memory_space=pl.ANY),
            scratch_shapes=[pltpu.VMEM((2, _SLOTS, D_MODEL), BF16), pltpu.VMEM((_TM, D_MODEL), BF16),
                            pltpu.SemaphoreType.DMA((2,)), pltpu.SemaphoreType.DMA],
        ),
        out_shape=jax.ShapeDtypeStruct((n_sorted, D_MODEL), BF16),
        compiler_params=_params(1),
        name="moe_dispatch",
    )(plan["nchunk"], plan["lstart"], plan["roff"], plan["tile_chunks"], plan["tail_start"], plan["tail_chunks"],
      plan["n_active"], n2, gates, plan["lstart_vec"])


def _experts_kernel(tile_expert_ref, tile_block_ref, n_active_ref, slot_ref, next_ref,
                    xs_ref, wg_hbm, wu_hbm, wd_hbm, ys_ref, wg_f32, wu_f32, wd_f32, wg_bf, wu_bf, wd_bf, sem):
    g = pl.program_id(0)
    active = g < n_active_ref[0]
    expert = tile_expert_ref[g]
    new_expert = (g == 0) | (expert != tile_expert_ref[jnp.maximum(g - 1, 0)])

    def weight_copies(e, slot):
        return [pltpu.make_async_copy(hbm.at[e], buf.at[slot], sem.at[slot])
                for hbm, buf in ((wg_hbm, wg_f32), (wu_hbm, wu_f32), (wd_hbm, wd_f32))]

    @pl.when(active & new_expert)
    def _switch_expert():
        slot = slot_ref[expert]

        @pl.when(g == 0)
        def _first():
            for copy in weight_copies(expert, slot):
                copy.start()

        for copy in weight_copies(expert, slot):
            copy.wait()
        wg_bf[...] = wg_f32[slot].astype(BF16)
        wu_bf[...] = wu_f32[slot].astype(BF16)
        wd_bf[...] = wd_f32[slot].astype(BF16)

        @pl.when(next_ref[expert] >= 0)
        def _prefetch():
            for copy in weight_copies(next_ref[expert], 1 - slot):
                copy.start()

    @pl.when(active)
    def _ffn():
        xs = xs_ref[...]
        gate = jnp.dot(xs, wg_bf[...], preferred_element_type=F32)
        up = jnp.dot(xs, wu_bf[...], preferred_element_type=F32)
        hidden = (gate * _sigmoid(gate) * up).astype(BF16)
        ys_ref[...] = jnp.dot(hidden, wd_bf[...], preferred_element_type=F32).astype(BF16)

    @pl.when(jnp.logical_not(active))
    def _unused_tile():
        ys_ref[...] = jnp.zeros_like(ys_ref)


def _experts(plan, xs, w_gate, w_up, w_down, max_tiles):
    rows_in = pl.BlockSpec((_TM, D_MODEL), lambda g, te, tb, *_: (tb[g], 0))
    rows_out = pl.BlockSpec((_TM, D_MODEL), lambda g, *_: (g, 0))
    weights = (w_gate, w_up, w_down)
    return pl.pallas_call(
        _experts_kernel,
        grid_spec=pltpu.PrefetchScalarGridSpec(
            num_scalar_prefetch=5,
            grid=(max_tiles,),
            in_specs=[rows_in] + [pl.BlockSpec(memory_space=pl.ANY)] * len(weights),
            out_specs=rows_out,
            scratch_shapes=[pltpu.VMEM((2,) + w.shape[1:], F32) for w in weights]
            + [pltpu.VMEM(w.shape[1:], BF16) for w in weights] + [pltpu.SemaphoreType.DMA((2,))],
        ),
        out_shape=jax.ShapeDtypeStruct(xs.shape, BF16),
        compiler_params=_params(1),
        name="moe_experts",
    )(plan["tile_expert"], plan["tile_block"], plan["n_active"], plan["expert_slot"], plan["expert_next"],
      xs, w_gate, w_up, w_down)


def _combine_kernel(nchunk_ref, lstart_ref, roff_ref, tchunks_ref,
                    gates_ref, lvec_ref, h1_ref, g_ref, ys_ref, o_ref, yloc_ref, sem):
    i = pl.program_id(0)
    buf = i % 2

    def run_copy(b, src0, dst0, first, chunks):
        src = pl.multiple_of(src0 + first * _CHUNK, _CHUNK)
        dst = pl.multiple_of(dst0 + first * _CHUNK, _CHUNK)
        rows = chunks * _CHUNK
        return pltpu.make_async_copy(ys_ref.at[pl.ds(src, rows), :], yloc_ref.at[b, pl.ds(dst, rows), :], sem.at[b])

    def fetch(b, step):
        for e in range(N_EXPERTS):
            idx = step * N_EXPERTS + e
            src0, dst0 = roff_ref[idx], lstart_ref[idx]
            _for_each_piece(nchunk_ref[idx], _RUN_CHUNKS,
                            lambda first, chunks, src0=src0, dst0=dst0: run_copy(b, src0, dst0, first, chunks).start())

    @pl.when(i == 0)
    def _first():
        yloc_ref[...] = jnp.zeros_like(yloc_ref)
        fetch(buf, i)

    @pl.when(i + 1 < pl.num_programs(0))
    def _prefetch():
        fetch(1 - buf, i + 1)

    _for_each_piece(tchunks_ref[i], _TILE_CHUNKS, lambda first, chunks: run_copy(buf, 0, 0, 0, chunks).wait())

    gates = gates_ref[...]
    sel, pos = _slot_positions(gates, lvec_ref[0])
    pos_lo = jnp.where(sel, pos, _BIG)
    pos_hi = jnp.where(sel, pos, -1.0)
    lo = jnp.min(pos_lo, axis=-1, keepdims=True)
    hi = jnp.max(pos_hi, axis=-1, keepdims=True)
    w_lo = jnp.sum(jnp.where(pos_lo == lo, gates, 0.0), axis=-1, keepdims=True)
    w_hi = jnp.where(hi == lo, 0.0, jnp.sum(jnp.where(pos_hi == hi, gates, 0.0), axis=-1, keepdims=True))
    moe = None
    for s0 in range(0, _SLOTS, _SLOT_CHUNK):
        slot = (lax.broadcasted_iota(jnp.int32, (_TT, _SLOT_CHUNK), 1) + s0).astype(F32)
        weights = (jnp.where(slot == lo, w_lo, 0.0) + jnp.where(slot == hi, w_hi, 0.0)).astype(BF16)
        part = jnp.dot(weights, yloc_ref[buf, s0:s0 + _SLOT_CHUNK, :], preferred_element_type=F32)
        moe = part if moe is None else moe + part
    o_ref[...] = _rms_norm(h1_ref[...] + moe, g_ref[...])


def _combine(plan, gates, h1, g, ys, n_tiles):
    row = lambda w: pl.BlockSpec((_TT, w), lambda i, *_: (i, 0))
    return pl.pallas_call(
        _combine_kernel,
        grid_spec=pltpu.PrefetchScalarGridSpec(
            num_scalar_prefetch=4,
            grid=(n_tiles,),
            in_specs=[row(_ROUTER_LANES), pl.BlockSpec((1, 1, _ROUTER_LANES), lambda i, *_: (i, 0, 0)),
                      row(D_MODEL), pl.BlockSpec(g.shape, lambda i, *_: (0, 0)),
                      pl.BlockSpec(memory_space=pl.ANY)],
            out_specs=row(D_MODEL),
            scratch_shapes=[pltpu.VMEM((2, _SLOTS, D_MODEL), BF16), pltpu.SemaphoreType.DMA((2,))],
        ),
        out_shape=jax.ShapeDtypeStruct(h1.shape, F32),
        compiler_params=_params(1),
        name="moe_combine",
    )(plan["nchunk"], plan["lstart"], plan["roff"], plan["tile_chunks"], gates, plan["lstart_vec"], h1, g, ys)


def _moe(n2, gates, counts, h1, w_gate, w_up, w_down, g):
    n_rows = n2.shape[0]
    assert n_rows % _TT == 0 and _SLOTS >= 2 * _TT + N_EXPERTS * (_CHUNK - 1)
    n_tiles = n_rows // _TT
    assert counts.shape[0] == n_tiles
    plan, max_tiles = _moe_plan(counts, n_tiles)
    xs = _dispatch(plan, n2, gates, n_tiles, max_tiles * _TM)
    ys = _experts(plan, xs, w_gate, w_up, w_down, max_tiles)
    return _combine(plan, gates, h1, g, ys, n_tiles)


def kernel(x, meta_tokens, norm_mix_g, w_in, conv_w, conv_b, lru_w_a, lru_b_a, lru_w_x, lru_b_x, lru_lambda, attn_sink, w_attn_branch, w_rec_branch, w_out, norm_ffn_g, w_group, b_group, w_router, b_router, moe_w_gate, moe_w_up, moe_w_down, final_norm_g):
    batch, seq, _ = x.shape
    assert norm_mix_g.shape[0] == 1, "single-layer block"
    assert seq % _TQ == 0 and seq % _TC == 0
    n_rows = batch * seq
    x2 = x.reshape(n_rows, D_MODEL)
    row = lambda a: a.reshape(1, -1).astype(F32)

    g_mix = row(norm_mix_g[0])
    w_in_bf, z_meta = _meta_proj(meta_tokens.astype(F32), g_mix, w_in[0].astype(F32))
    k_meta = z_meta[:, ATTN_WIDTH:ATTN_WIDTH + KV_WIDTH].astype(BF16)
    v_meta = z_meta[:, ATTN_WIDTH + KV_WIDTH:ATTN_WIDTH + 2 * KV_WIDTH].astype(BF16)
    xr_meta = z_meta[:, ATTN_WIDTH + 2 * KV_WIDTH:ATTN_WIDTH + 2 * KV_WIDTH + LRU_WIDTH]
    q, k, v, xr, gy, ga, gr = _in_proj(x2, g_mix, w_in_bf, 512)

    sink_rows = jnp.broadcast_to((attn_sink[0].astype(F32) * HEAD_DIM ** 0.5)[:, None, None],
                                 (N_HEADS, BLOCK, BLOCK)).reshape(N_HEADS * BLOCK, BLOCK)
    shape3 = lambda a: a.reshape(batch, seq, a.shape[-1])
    pad_keys = lambda a: jnp.pad(a, ((0, BLOCK - N_META), (0, 0)))
    attn = _attention(shape3(q), shape3(k), shape3(v), pad_keys(k_meta), pad_keys(v_meta), sink_rows, shape3(ga),
                      w_attn_branch[0].astype(F32))

    h_dirs = []
    for d, reverse in enumerate((False, True)):
        wg = jnp.concatenate([lru_w_a[0, d], lru_w_x[0, d]], axis=-1).astype(BF16)
        bg_half = 0.5 * jnp.stack([lru_b_a[0, d], lru_b_x[0, d]]).astype(F32)
        h_dirs.append(_lru(shape3(xr), xr_meta, 0.5 * conv_w[0].astype(F32), 0.5 * row(conv_b[0]), wg, bg_half,
                           row(lru_lambda[0, d]), reverse))

    w_route = jnp.concatenate([w_group[0], w_router[0]], axis=1).astype(F32)
    w_route = jnp.pad(w_route, ((0, 0), (0, _ROUTER_LANES - w_route.shape[1])))
    wr_hi = w_route.astype(BF16)
    wr_lo = (w_route - wr_hi.astype(F32)).astype(BF16)
    b_route = jnp.pad(jnp.concatenate([b_group[0], b_router[0]]).astype(F32),
                      (0, _ROUTER_LANES - N_GROUPS - N_EXPERTS)).reshape(1, _ROUTER_LANES)
    h1, n2, gates, counts = _merge(x2, h_dirs[0].reshape(n_rows, LRU_WIDTH), h_dirs[1].reshape(n_rows, LRU_WIDTH),
                                   gy, attn.reshape(n_rows, D_MODEL), gr,
                                   w_rec_branch[0].astype(F32), w_out[0].astype(F32), row(norm_ffn_g[0]),
                                   wr_hi, wr_lo, b_route, _TT)

    out = _moe(n2, gates, counts, h1, moe_w_gate[0].astype(F32), moe_w_up[0].astype(F32),
               moe_w_down[0].astype(F32), row(final_norm_g))
    return out.reshape(batch, seq, D_MODEL)
```

```python
import functools
import math

import jax
import jax.numpy as jnp
from jax import lax
from jax.experimental import pallas as pl
from jax.experimental.pallas import tpu as pltpu

D_MODEL = 1024
N_META = 16
N_HEADS = 8
N_KV_HEADS = 2
HEAD_DIM = 128
Q_PER_KV = N_HEADS // N_KV_HEADS
ATTN_WIDTH = N_HEADS * HEAD_DIM
KV_WIDTH = N_KV_HEADS * HEAD_DIM
WINDOW = 128
BLOCK = 128
LRU_WIDTH = D_MODEL
LRU_BLOCKS = 8
LRU_BLOCK_DIM = LRU_WIDTH // LRU_BLOCKS
CONV_WIDTH = 4
LRU_C = 8.0
N_GROUPS = 4
EXPERTS_PER_GROUP = 4
N_EXPERTS = N_GROUPS * EXPERTS_PER_GROUP
EXPERT_FF = 512
IN_WIDTH = ATTN_WIDTH + 2 * KV_WIDTH + 2 * LRU_WIDTH + 2 * D_MODEL
EPS = 1e-6
NEG_INF = -1e30

LANES = 128
SUBLANES = 8
VMEM_LIMIT = 56 * 1024 * 1024

BF16 = jnp.bfloat16
F32 = jnp.float32


def _params(n_grid_dims):
    return pltpu.CompilerParams(
        dimension_semantics=("arbitrary",) * n_grid_dims,
        vmem_limit_bytes=VMEM_LIMIT,
    )


def _sigmoid(x):
    return 0.5 * jnp.tanh(0.5 * x) + 0.5


def _gelu_tanh(x):
    c = math.sqrt(2.0 / math.pi)
    return 0.5 * x * (1.0 + jnp.tanh(c * (x + 0.044715 * (x * x * x))))


def _rms_norm(xf, g):
    ms = jnp.mean(xf * xf, axis=-1, keepdims=True)
    return xf * lax.rsqrt(ms + EPS) * g


_IN_CHUNK = 512


def _in_proj_kernel(x_ref, g_ref, w_ref, q_ref, k_ref, v_ref, xr_ref, gy_ref, ga_ref, gr_ref):
    n = _rms_norm(x_ref[...], g_ref[...]).astype(BF16)

    def proj(c0, width):
        return jnp.dot(n, w_ref[:, c0:c0 + width], preferred_element_type=F32)

    c = 0
    for j in range(ATTN_WIDTH // _IN_CHUNK):
        q_ref[:, j * _IN_CHUNK:(j + 1) * _IN_CHUNK] = proj(c, _IN_CHUNK).astype(BF16)
        c += _IN_CHUNK
    kv = proj(c, 2 * KV_WIDTH)
    k_ref[...] = kv[:, :KV_WIDTH].astype(BF16)
    v_ref[...] = kv[:, KV_WIDTH:].astype(BF16)
    c += 2 * KV_WIDTH
    for j in range(LRU_WIDTH // _IN_CHUNK):
        xr_ref[:, j * _IN_CHUNK:(j + 1) * _IN_CHUNK] = proj(c, _IN_CHUNK)
        c += _IN_CHUNK
    for j in range(LRU_WIDTH // _IN_CHUNK):
        gy_ref[:, j * _IN_CHUNK:(j + 1) * _IN_CHUNK] = _gelu_tanh(proj(c, _IN_CHUNK)).astype(BF16)
        c += _IN_CHUNK
    for ref in (ga_ref, gr_ref):
        for j in range(D_MODEL // _IN_CHUNK):
            ref[:, j * _IN_CHUNK:(j + 1) * _IN_CHUNK] = _sigmoid(proj(c, _IN_CHUNK)).astype(BF16)
            c += _IN_CHUNK


def _in_proj(x2, g, w_bf, tm):
    n_rows = x2.shape[0]
    row = lambda w: pl.BlockSpec((tm, w), lambda i: (i, 0))
    full = lambda a: pl.BlockSpec(a.shape, lambda i: (0,) * a.ndim)
    out_shapes = (
        jax.ShapeDtypeStruct((n_rows, ATTN_WIDTH), BF16),
        jax.ShapeDtypeStruct((n_rows, KV_WIDTH), BF16),
        jax.ShapeDtypeStruct((n_rows, KV_WIDTH), BF16),
        jax.ShapeDtypeStruct((n_rows, LRU_WIDTH), F32),
        jax.ShapeDtypeStruct((n_rows, LRU_WIDTH), BF16),
        jax.ShapeDtypeStruct((n_rows, D_MODEL), BF16),
        jax.ShapeDtypeStruct((n_rows, D_MODEL), BF16),
    )
    return pl.pallas_call(
        _in_proj_kernel,
        grid=(n_rows // tm,),
        in_specs=[row(D_MODEL), full(g), full(w_bf)],
        out_specs=tuple(row(s.shape[1]) for s in out_shapes),
        out_shape=out_shapes,
        compiler_params=_params(1),
        name="in_proj",
    )(x2, g, w_bf)


def _meta_proj_kernel(x_ref, g_ref, w_ref, wbf_ref, z_ref):
    n = _rms_norm(x_ref[...], g_ref[...]).astype(BF16)
    w = w_ref[...].astype(BF16)
    wbf_ref[...] = w
    z_ref[...] = jnp.dot(n, w, preferred_element_type=F32)


def _meta_proj(meta, g, w):
    n_steps = 4
    assert w.shape[1] % (n_steps * LANES) == 0
    width = w.shape[1] // n_steps
    chunk = lambda rows: pl.BlockSpec((rows, width), lambda j: (0, j))
    full = lambda a: pl.BlockSpec(a.shape, lambda j: (0,) * a.ndim)
    return pl.pallas_call(
        _meta_proj_kernel,
        grid=(n_steps,),
        in_specs=[full(meta), full(g), chunk(w.shape[0])],
        out_specs=(chunk(w.shape[0]), chunk(meta.shape[0])),
        out_shape=(jax.ShapeDtypeStruct(w.shape, BF16), jax.ShapeDtypeStruct((meta.shape[0], w.shape[1]), F32)),
        compiler_params=_params(1),
        name="meta_proj",
    )(meta, g, w)


_TQ = 512
_SUB = _TQ // BLOCK
_GROUP_ROWS = Q_PER_KV * BLOCK


_KEYS = 4 * BLOCK
_SM_ROWS = 32


def _attn_kernel(q_ref, kp_ref, kc_ref, kn_ref, vp_ref, vc_ref, vn_ref, km_ref, vm_ref,
                 sink_ref, ga_ref, w_ref, o_ref, bias_ref, attn_ref, s_ref, p_ref, m_ref):
    i = pl.program_id(1)
    n_i = pl.num_programs(1)
    scale = HEAD_DIM ** -0.5
    exp_scale = scale * math.log2(math.e)

    @pl.when((pl.program_id(0) == 0) & (i == 0))
    def _init_bias():
        r = lax.broadcasted_iota(jnp.int32, (BLOCK, BLOCK), 0)
        c = lax.broadcasted_iota(jnp.int32, (BLOCK, BLOCK), 1)
        d_prev = (r + BLOCK - c).astype(F32)
        d_cur = jnp.abs(r - c).astype(F32)
        d_next = (c + BLOCK - r).astype(F32)
        for h in range(N_HEADS):
            slope = 2.0 ** (-8.0 * (h + 1.0) / N_HEADS) / scale
            rows = slice(h * BLOCK, (h + 1) * BLOCK)
            bias_ref[rows, 0:BLOCK] = jnp.where(c >= r, -slope * d_prev, NEG_INF / scale)
            bias_ref[rows, BLOCK:2 * BLOCK] = -slope * d_cur
            bias_ref[rows, 2 * BLOCK:3 * BLOCK] = jnp.where(c <= r, -slope * d_next, NEG_INF / scale)
            bias_ref[rows, 3 * BLOCK:4 * BLOCK] = jnp.where(c < N_META, 0.0, NEG_INF / scale)

    nt = (((1,), (1,)), ((), ()))
    for j in range(_SUB):
        rows = slice(j * BLOCK, (j + 1) * BLOCK)
        q = q_ref[0, rows, :]
        if j == 0:
            k3 = [kp_ref[0], kc_ref[0, 0:2 * BLOCK, :]]
            v3 = [vp_ref[0], vc_ref[0, 0:2 * BLOCK, :]]
        elif j == _SUB - 1:
            k3 = [kc_ref[0, (j - 1) * BLOCK:(j + 1) * BLOCK, :], kn_ref[0]]
            v3 = [vc_ref[0, (j - 1) * BLOCK:(j + 1) * BLOCK, :], vn_ref[0]]
        else:
            k3 = [kc_ref[0, (j - 1) * BLOCK:(j + 2) * BLOCK, :]]
            v3 = [vc_ref[0, (j - 1) * BLOCK:(j + 2) * BLOCK, :]]
        k_cat = jnp.concatenate(k3 + [km_ref[...]], axis=0)
        v_cat = jnp.concatenate(v3 + [vm_ref[...]], axis=0)
        masked = []
        if j == 0:
            masked.append((slice(0, BLOCK), i == 0))
        if j == _SUB - 1:
            masked.append((slice(2 * BLOCK, 3 * BLOCK), i == n_i - 1))
        for g in range(N_KV_HEADS):
            cols = slice(g * HEAD_DIM, (g + 1) * HEAD_DIM)
            row0 = g * _GROUP_ROWS
            qg = jnp.concatenate(
                [q[:, (g * Q_PER_KV + h) * HEAD_DIM:(g * Q_PER_KV + h + 1) * HEAD_DIM] for h in range(Q_PER_KV)],
                axis=0)
            pair = j * N_KV_HEADS + g
            s_ref[pair] = lax.dot_general(qg, k_cat[:, cols], nt, preferred_element_type=F32)
            for mask_cols, mask_on in masked:
                s_ref[pair, :, mask_cols] = jnp.where(mask_on, NEG_INF / scale, s_ref[pair, :, mask_cols])

            chunks = [(slice(c * _SM_ROWS, (c + 1) * _SM_ROWS), slice(row0 + c * _SM_ROWS, row0 + (c + 1) * _SM_ROWS))
                      for c in range(_GROUP_ROWS // _SM_ROWS)]
            wide = lambda col: jnp.broadcast_to(col, (_SM_ROWS, BLOCK))
            tiled = lambda stat: jnp.concatenate([stat] * (_KEYS // BLOCK), axis=1)
            for r, rb in chunks:
                z = s_ref[pair, r, :] + bias_ref[rb, :]
                m_ref[pair, r, :] = jnp.maximum(wide(jnp.max(z, axis=-1, keepdims=True)), sink_ref[rb, :])
            for r, rb in chunks:
                m = m_ref[pair, r, :]
                p = jnp.exp2((s_ref[pair, r, :] + bias_ref[rb, :] - tiled(m)) * exp_scale)
                denom = wide(jnp.sum(p, axis=-1, keepdims=True)) + jnp.exp2((sink_ref[rb, :] - m) * exp_scale)
                p_ref[pair, r, :] = p.astype(BF16)
                m_ref[pair, r, :] = 1.0 / denom
            o = jnp.dot(p_ref[pair], v_cat[:, cols], preferred_element_type=F32)
            o = (o * m_ref[pair]).astype(BF16)
            for h in range(Q_PER_KV):
                head = g * Q_PER_KV + h
                attn_ref[rows, head * HEAD_DIM:(head + 1) * HEAD_DIM] = o[h * BLOCK:(h + 1) * BLOCK, :]

    proj = jnp.dot(attn_ref[...], w_ref[...].astype(BF16), preferred_element_type=F32)
    o_ref[0] = (ga_ref[0].astype(F32) * proj).astype(BF16)


def _attention(q, k, v, k_meta, v_meta, sink_rows, g_attn, w_attn):
    batch, seq, _ = q.shape
    n_blk = seq // BLOCK
    main = lambda w: pl.BlockSpec((1, _TQ, w), lambda b, i: (b, i, 0))
    prev = pl.BlockSpec((1, BLOCK, KV_WIDTH), lambda b, i: (b, jnp.maximum(i * _SUB - 1, 0), 0))
    nxt = pl.BlockSpec((1, BLOCK, KV_WIDTH), lambda b, i: (b, jnp.minimum((i + 1) * _SUB, n_blk - 1), 0))
    full = lambda a: pl.BlockSpec(a.shape, lambda b, i: (0,) * a.ndim)
    return pl.pallas_call(
        _attn_kernel,
        grid=(batch, seq // _TQ),
        in_specs=[main(ATTN_WIDTH), prev, main(KV_WIDTH), nxt, prev, main(KV_WIDTH), nxt,
                  full(k_meta), full(v_meta), full(sink_rows), main(D_MODEL), full(w_attn)],
        out_specs=main(D_MODEL),
        out_shape=jax.ShapeDtypeStruct((batch, seq, D_MODEL), BF16),
        scratch_shapes=[pltpu.VMEM((N_HEADS * BLOCK, _KEYS), F32),
                        pltpu.VMEM((_TQ, ATTN_WIDTH), BF16),
                        pltpu.VMEM((_SUB * N_KV_HEADS, _GROUP_ROWS, _KEYS), F32),
                        pltpu.VMEM((_SUB * N_KV_HEADS, _GROUP_ROWS, _KEYS), BF16),
                        pltpu.VMEM((_SUB * N_KV_HEADS, _GROUP_ROWS, BLOCK), F32)],
        compiler_params=_params(2),
        name="attention",
    )(q, k, k, k, v, v, v, k_meta, v_meta, sink_rows, g_attn, w_attn)


_TC = 1024
_TS = 512
_HALO = SUBLANES


def _interleave_in(dst_ref, src, n_rows, row0=0):
    seg = n_rows // SUBLANES
    for n in range(LRU_BLOCKS):
        for s in range(SUBLANES):
            dst_ref[n, pl.ds(row0 + s, seg, stride=SUBLANES), :] = src(row0 + s * seg, seg, n)


def _interleave_out(write, src_ref, n_rows, row0=0):
    seg = n_rows // SUBLANES
    for n in range(LRU_BLOCKS):
        for s in range(SUBLANES):
            write(row0 + s * seg, seg, n, src_ref[n, pl.ds(row0 + s, seg, stride=SUBLANES), :])


def _lru_gates(n_rows, row0, prev2, prev1, next0, x_ref, cw_ref, cb_ref, wg_ref, bg_ref, lam_ref, a_ref, u_ref):
    seg = n_rows // SUBLANES
    sub = lax.broadcasted_iota(jnp.int32, (SUBLANES, LRU_BLOCK_DIM), 0)
    lam = lam_ref[...]
    decay_scale = (-0.5 * LRU_C * math.log2(math.e)) * (
        jnp.maximum(-lam, 0.0) + jnp.log(1.0 + jnp.exp(-jnp.abs(lam))))
    for n in range(LRU_BLOCKS):
        cols = slice(n * LRU_BLOCK_DIM, (n + 1) * LRU_BLOCK_DIM)
        x = x_ref[n, row0:row0 + n_rows, :]
        group = lambda j: x[j * SUBLANES:(j + 1) * SUBLANES, :]
        e0 = jnp.where(sub == 0, prev2(n), pltpu.roll(group(seg - 2), 1, axis=0))
        e1 = jnp.where(sub == 0, prev1(n), pltpu.roll(group(seg - 1), 1, axis=0))
        e_next = jnp.where(sub == SUBLANES - 1, next0(n), pltpu.roll(group(0), SUBLANES - 1, axis=0))
        ext = jnp.concatenate([e0, e1, x, e_next], axis=0)
        xh = cb_ref[:, cols] + sum(
            cw_ref[t:t + 1, cols] * ext[t * SUBLANES:t * SUBLANES + n_rows, :] for t in range(CONV_WIDTH))
        pre = jnp.dot(xh.astype(BF16), wg_ref[n], preferred_element_type=F32)
        t_a = jnp.tanh(pre[:, :LRU_BLOCK_DIM] + bg_ref[0:1, cols])
        t_x = jnp.tanh(pre[:, LRU_BLOCK_DIM:] + bg_ref[1:2, cols])
        scale = decay_scale[:, cols]
        a = jnp.exp2(t_a * scale + scale)
        y = 1.0 - a * a
        a_ref[n, row0:row0 + n_rows, :] = a
        u_ref[n, row0:row0 + n_rows, :] = (y * lax.rsqrt(jnp.maximum(y, 1e-30))) * ((t_x + 1.0) * xh)


def _lru_scan(n_rows, tiles, reverse, carry_in, a_ref, u_ref, h_ref):
    seg = n_rows // SUBLANES
    unroll = min(64, seg)
    sub = lax.broadcasted_iota(jnp.int32, (SUBLANES, LRU_BLOCK_DIM), 0)
    chains = [(row0, n) for row0 in tiles for n in range(LRU_BLOCKS)]

    def rows(jj, row0):
        j = (seg - 1 - jj) if reverse else jj
        return pl.ds(pl.multiple_of(row0 + j * SUBLANES, SUBLANES), SUBLANES)

    def local(jj, state):
        hs, ps = state
        a = [a_ref[n, rows(jj, row0), :] for row0, n in chains]
        return (tuple(a[c] * hs[c] + u_ref[n, rows(jj, row0), :] for c, (row0, n) in enumerate(chains)),
                tuple(a[c] * ps[c] for c in range(len(chains))))

    zeros = tuple(jnp.zeros((SUBLANES, LRU_BLOCK_DIM), F32) for _ in chains)
    ones = tuple(jnp.ones((SUBLANES, LRU_BLOCK_DIM), F32) for _ in chains)
    h_end, p_end = lax.fori_loop(0, seg, local, (zeros, ones), unroll=unroll)

    seg_in = [None] * len(chains)
    carry = list(carry_in)
    first, last = (SUBLANES - 1, 0) if reverse else (0, SUBLANES - 1)
    for row0 in (reversed(tiles) if reverse else tiles):
        for n in range(LRU_BLOCKS):
            c = chains.index((row0, n))
            p, h = p_end[c], h_end[c]
            for d in (1, 2, 4):
                shift = SUBLANES - d if reverse else d
                ok = (sub < SUBLANES - d) if reverse else (sub >= d)
                h = h + p * jnp.where(ok, pltpu.roll(h, shift, axis=0), 0.0)
                p = p * jnp.where(ok, pltpu.roll(p, shift, axis=0), 1.0)
            seg_out = h + p * carry[n]
            shift = SUBLANES - 1 if reverse else 1
            seg_in[c] = jnp.where(sub == first, carry[n], pltpu.roll(seg_out, shift, axis=0))
            carry[n] = seg_out[last:last + 1, :]

    if h_ref is not None:
        def final(jj, hs):
            new = tuple(a_ref[n, rows(jj, row0), :] * hs[c] + u_ref[n, rows(jj, row0), :]
                        for c, (row0, n) in enumerate(chains))
            for c, (row0, n) in enumerate(chains):
                h_ref[n, rows(jj, row0), :] = new[c]
            return new

        lax.fori_loop(0, seg, final, tuple(seg_in), unroll=unroll)
    return carry


def _lru_kernel(reverse, xr_ref, xp_ref, xn_ref, xm_ref, cw_ref, cb_ref, wg_ref, bg_ref, lam_ref,
                h_ref, carry_ref, x_scr, a_scr, u_scr, h_scr):
    step = pl.program_id(1)
    n_steps = pl.num_programs(1)
    t = (n_steps - 1 - step) if reverse else step
    args = (x_scr, cw_ref, cb_ref, wg_ref, bg_ref, lam_ref, a_scr, u_scr)
    lanes = lambda n: slice(n * LRU_BLOCK_DIM, (n + 1) * LRU_BLOCK_DIM)
    zero_row = lambda n: jnp.zeros((1, LRU_BLOCK_DIM), F32)

    if reverse:
        @pl.when(step == 0)
        def _zero_state():
            carry_ref[...] = jnp.zeros_like(carry_ref)
    else:
        @pl.when(step == 0)
        def _meta_state():
            _interleave_in(x_scr, lambda r0, nr, n: xm_ref[r0:r0 + nr, lanes(n)], N_META)
            _lru_gates(N_META, 0, zero_row, zero_row, lambda n: xr_ref[0, 0:1, lanes(n)], *args)
            state = _lru_scan(N_META, [0], False, [zero_row(n) for n in range(LRU_BLOCKS)], a_scr, u_scr, None)
            for n in range(LRU_BLOCKS):
                carry_ref[0:1, lanes(n)] = state[n]

    def before(row):
        return lambda n: jnp.where(t == 0, xm_ref[N_META - _HALO + row:N_META - _HALO + row + 1, lanes(n)],
                                   xp_ref[0, row:row + 1, lanes(n)])

    after = lambda n: jnp.where(t == n_steps - 1, 0.0, xn_ref[0, 0:1, lanes(n)])
    inside = lambda row: (lambda n: xr_ref[0, row:row + 1, lanes(n)])
    tiles = list(range(0, _TC, _TS))
    for row0 in tiles:
        _interleave_in(x_scr, lambda r0, nr, n: xr_ref[0, r0:r0 + nr, lanes(n)], _TS, row0)
        prev2, prev1 = (before(_HALO - 2), before(_HALO - 1)) if row0 == 0 else (inside(row0 - 2), inside(row0 - 1))
        next0 = after if row0 + _TS == _TC else inside(row0 + _TS)
        _lru_gates(_TS, row0, prev2, prev1, next0, *args)
    state = _lru_scan(_TS, tiles, reverse, [carry_ref[0:1, lanes(n)] for n in range(LRU_BLOCKS)],
                      a_scr, u_scr, h_scr)
    for n in range(LRU_BLOCKS):
        carry_ref[0:1, lanes(n)] = state[n]

    def write(r0, nr, n, rows):
        h_ref[0, r0:r0 + nr, lanes(n)] = rows.astype(h_ref.dtype)

    for row0 in tiles:
        _interleave_out(write, h_scr, _TS, row0)


def _lru(xr, xr_meta, conv_w, conv_b, wg_bf, bg, lam, reverse):
    batch, seq, _ = xr.shape
    n_steps = seq // _TC
    n_halo = seq // _HALO
    per_tile = _TC // _HALO
    tile = (lambda s: n_steps - 1 - s) if reverse else (lambda s: s)
    main = pl.BlockSpec((1, _TC, LRU_WIDTH), lambda b, s: (b, tile(s), 0))
    before = pl.BlockSpec((1, _HALO, LRU_WIDTH), lambda b, s: (b, jnp.maximum(tile(s) * per_tile - 1, 0), 0))
    after = pl.BlockSpec((1, _HALO, LRU_WIDTH),
                         lambda b, s: (b, jnp.minimum((tile(s) + 1) * per_tile, n_halo - 1), 0))
    full = lambda a: pl.BlockSpec(a.shape, lambda b, s: (0,) * a.ndim)
    return pl.pallas_call(
        functools.partial(_lru_kernel, reverse),
        grid=(batch, n_steps),
        in_specs=[main, before, after, full(xr_meta), full(conv_w), full(conv_b), full(wg_bf), full(bg),
                  full(lam)],
        out_specs=main,
        out_shape=jax.ShapeDtypeStruct((batch, seq, LRU_WIDTH), BF16),
        scratch_shapes=[pltpu.VMEM((SUBLANES, LRU_WIDTH), F32)]
        + [pltpu.VMEM((LRU_BLOCKS, _TC, LRU_BLOCK_DIM), F32)] * 4,
        compiler_params=_params(2),
        name="lru_bwd" if reverse else "lru_fwd",
    )(xr, xr, xr, xr_meta, conv_w, conv_b, wg_bf, bg, lam)


_ROUTER_LANES = LANES
_MERGE_ROWS = 256


def _split_dot(a, b_hi, b_lo):
    a_hi = a.astype(BF16)
    a_lo = (a - a_hi.astype(F32)).astype(BF16)
    both = jnp.dot(a_hi, jnp.concatenate([b_hi, b_lo], axis=1), preferred_element_type=F32)
    return (both[:, :_ROUTER_LANES]
            + (jnp.dot(a_lo, b_hi, preferred_element_type=F32) + both[:, _ROUTER_LANES:]))


def _merge_kernel(x_ref, hf_ref, hb_ref, gy_ref, ga_ref, gr_ref, wrec_ref, wout_ref, g_ref,
                  wr_hi_ref, wr_lo_ref, br_ref, h1_ref, n2_ref, gates_ref, count_ref):
    subs = [slice(r0, r0 + _MERGE_ROWS) for r0 in range(0, x_ref.shape[0], _MERGE_ROWS)]
    rec, n2 = {}, {}
    w_rec, w_out = wrec_ref[...].astype(BF16), wout_ref[...].astype(BF16)
    for k, rows in enumerate(subs):
        rec_in = (hf_ref[rows, :] + hb_ref[rows, :]) * gy_ref[rows, :]
        rec[k] = jnp.dot(rec_in, w_rec, preferred_element_type=F32)
    for k, rows in enumerate(subs):
        mix = (ga_ref[rows, :].astype(F32) + gr_ref[rows, :].astype(F32) * rec[k]).astype(BF16)
        h1 = x_ref[rows, :] + jnp.dot(mix, w_out, preferred_element_type=F32)
        h1_ref[rows, :] = h1
        n2[k] = _rms_norm(h1, g_ref[...])
        n2_ref[rows, :] = n2[k].astype(BF16)
    count_ref[0] = sum(_route(rows, n2[k], wr_hi_ref, wr_lo_ref, br_ref, gates_ref) for k, rows in enumerate(subs))


def _route(rows, n2, wr_hi_ref, wr_lo_ref, br_ref, gates_ref):
    logits = _split_dot(n2, wr_hi_ref[...], wr_lo_ref[...]) + br_ref[...]
    lane_i = lax.broadcasted_iota(jnp.int32, logits.shape, 1)
    lane = lane_i.astype(F32)
    first = lambda mask: jnp.min(jnp.where(mask, lane, float(_ROUTER_LANES)), axis=-1, keepdims=True)
    lg = jnp.where(lane < N_GROUPS, logits, -jnp.inf)
    g_max = jnp.max(lg, axis=-1, keepdims=True)
    g_top_p = 1.0 / jnp.sum(jnp.exp(lg - g_max), axis=-1, keepdims=True)
    g_idx = first(lg == g_max)
    e = lane_i - N_GROUPS
    e_group = jnp.right_shift(e, int(math.log2(EXPERTS_PER_GROUP))).astype(F32)
    in_group = (e >= 0) & (e < N_EXPERTS) & (e_group == g_idx)
    le = jnp.where(in_group, logits, -jnp.inf)
    m1 = jnp.max(le, axis=-1, keepdims=True)
    i1 = first(le == m1)
    le2 = jnp.where(lane == i1, -jnp.inf, le)
    m2 = jnp.max(le2, axis=-1, keepdims=True)
    i2 = first(le2 == m2)
    e2 = jnp.exp(m2 - m1)
    w1 = g_top_p / (1.0 + e2)
    w2 = g_top_p * e2 / (1.0 + e2)
    gates = jnp.where(lane == i1, w1, 0.0) + jnp.where(lane == i2, w2, 0.0)
    gates_ref[rows, :] = gates
    return jnp.sum((gates > 0.0).astype(F32), axis=0, keepdims=True)


def _merge(x2, hf, hb, gy, ga, gr, w_rec, w_out, g, wr_hi, wr_lo, br, tm):
    n_rows = x2.shape[0]
    row = lambda w: pl.BlockSpec((tm, w), lambda i: (i, 0))
    full = lambda a: pl.BlockSpec(a.shape, lambda i: (0,) * a.ndim)
    return pl.pallas_call(
        _merge_kernel,
        grid=(n_rows // tm,),
        in_specs=[row(D_MODEL)] * 6 + [full(w_rec), full(w_out), full(g), full(wr_hi), full(wr_lo), full(br)],
        out_specs=(row(D_MODEL), row(D_MODEL), row(_ROUTER_LANES),
                   pl.BlockSpec((1, 1, _ROUTER_LANES), lambda i: (i, 0, 0))),
        out_shape=(jax.ShapeDtypeStruct((n_rows, D_MODEL), F32),
                   jax.ShapeDtypeStruct((n_rows, D_MODEL), BF16),
                   jax.ShapeDtypeStruct((n_rows, _ROUTER_LANES), F32),
                   jax.ShapeDtypeStruct((n_rows // tm, 1, _ROUTER_LANES), F32)),
        compiler_params=_params(1),
        name="merge",
    )(x2, hf, hb, gy, ga, gr, w_rec, w_out, g, wr_hi, wr_lo, br)


_TT = 512
_CHUNK = 16
_TM = 512
_SLOTS = 1280
_SLOT_CHUNK = 256
_BIG = 1.0e6


def _moe_plan(counts, n_tiles):
    i32 = jnp.int32
    cnt = counts[:, 0, N_GROUPS:N_GROUPS + N_EXPERTS].astype(i32)
    padc = (cnt + _CHUNK - 1) // _CHUNK * _CHUNK
    lstart = jnp.cumsum(padc, axis=1) - padc
    tot = jnp.sum(padc, axis=0)
    ntile = (tot + _TM - 1) // _TM
    tile_end = jnp.cumsum(ntile)
    base = (tile_end - ntile) * _TM
    roff = base[None, :] + jnp.cumsum(padc, axis=0) - padc
    n_active = tile_end[-1]
    max_tiles = (2 * n_tiles * _TT + n_tiles * N_EXPERTS * (_CHUNK - 1)) // _TM + N_EXPERTS
    g = jnp.minimum(jnp.arange(max_tiles, dtype=i32), n_active - 1)
    tile_expert = jnp.sum(g[:, None] >= tile_end[None, :], axis=1, dtype=i32)
    lstart_vec = jnp.pad(lstart.astype(F32), ((0, 0), (N_GROUPS, _ROUTER_LANES - N_GROUPS - N_EXPERTS)))[:, None, :]
    has_tiles = ntile > 0
    ids = jnp.arange(N_EXPERTS, dtype=i32)
    later = (ids[None, :] > ids[:, None]) & has_tiles[None, :]
    expert_next = jnp.min(jnp.where(later, ids[None, :], N_EXPERTS), axis=1)
    plan = dict(
        nchunk=(padc // _CHUNK).reshape(-1), lstart=lstart.reshape(-1), roff=roff.reshape(-1),
        tile_chunks=jnp.sum(padc // _CHUNK, axis=1, dtype=i32),
        tail_start=base + tot, tail_chunks=(ntile * _TM - tot) // _CHUNK,
        tile_expert=tile_expert, tile_block=g, n_active=n_active.reshape(1), lstart_vec=lstart_vec,
        expert_slot=(jnp.cumsum(has_tiles.astype(i32)) - 1) % 2,
        expert_next=jnp.where(expert_next == N_EXPERTS, -1, expert_next).astype(i32))
    return plan, max_tiles


def _slot_positions(gates, lstart_vec):
    sel = gates > 0.0
    r = lax.broadcasted_iota(jnp.int32, (_TT, _TT), 0)
    c = lax.broadcasted_iota(jnp.int32, (_TT, _TT), 1)
    before = (c < r).astype(BF16)
    rank = jnp.dot(before, sel.astype(BF16), preferred_element_type=F32)
    return sel, rank + lstart_vec


_TILE_CHUNKS = (2 * _TT + N_EXPERTS * (_CHUNK - 1)) // _CHUNK
_RUN_CHUNKS = (_TT + _CHUNK - 1) // _CHUNK


def _for_each_piece(n_chunks, most, fn):
    for bit in reversed(range(most.bit_length())):
        @pl.when((n_chunks >> bit) & 1 == 1)
        def _piece(bit=bit):
            fn((n_chunks >> (bit + 1)) << (bit + 1), 1 << bit)


def _dispatch_kernel(nchunk_ref, lstart_ref, roff_ref, tchunks_ref, tail_start_ref, tail_chunks_ref, n_active_ref,
                     n2_ref, gates_ref, lvec_ref, xs_ref, xloc_ref, zero_ref, sem, zsem):
    i = pl.program_id(0)
    last = pl.num_programs(0) - 1
    buf = i % 2

    def run_copy(b, src0, dst0, first, chunks):
        src = pl.multiple_of(src0 + first * _CHUNK, _CHUNK)
        dst = pl.multiple_of(dst0 + first * _CHUNK, _CHUNK)
        rows = chunks * _CHUNK
        return pltpu.make_async_copy(xloc_ref.at[b, pl.ds(src, rows), :], xs_ref.at[pl.ds(dst, rows), :], sem.at[b])

    def wait_step(b, step):
        _for_each_piece(tchunks_ref[step], _TILE_CHUNKS, lambda first, chunks: run_copy(b, 0, 0, 0, chunks).wait())

    @pl.when(i >= 2)
    def _buffer_free():
        wait_step(buf, i - 2)

    sel, pos = _slot_positions(gates_ref[...], lvec_ref[0])
    lo = jnp.min(jnp.where(sel, pos, _BIG).T, axis=0, keepdims=True)
    hi = jnp.max(jnp.where(sel, pos, -1.0).T, axis=0, keepdims=True)
    slot = lax.broadcasted_iota(jnp.int32, (_SLOTS, _TT), 0).astype(F32)
    onehot = ((slot == lo) | (slot == hi)).astype(BF16)
    xloc_ref[buf] = jnp.dot(onehot, n2_ref[...], preferred_element_type=F32).astype(BF16)

    for e in range(N_EXPERTS):
        idx = i * N_EXPERTS + e
        src0, dst0 = lstart_ref[idx], roff_ref[idx]
        _for_each_piece(nchunk_ref[idx], _RUN_CHUNKS,
                        lambda first, chunks, src0=src0, dst0=dst0: run_copy(buf, src0, dst0, first, chunks).start())

    @pl.when(i == last)
    def _drain():
        @pl.when(i >= 1)
        def _previous():
            wait_step(1 - buf, i - 1)

        wait_step(buf, i)

    @pl.when(i == last)
    def _zero_tails():
        zero_ref[...] = jnp.zeros_like(zero_ref)

        def tail_copy(dst0, c):
            dst = pl.multiple_of(dst0 + c * _CHUNK, _CHUNK)
            return pltpu.make_async_copy(zero_ref.at[pl.ds(0, _CHUNK), :], xs_ref.at[pl.ds(dst, _CHUNK), :], zsem)

        def tile_copy(t):
            dst = pl.multiple_of(t * _TM, _TM)
            return pltpu.make_async_copy(zero_ref, xs_ref.at[pl.ds(dst, _TM), :], zsem)

        n_tiles_total = xs_ref.shape[0] // _TM

        def tstart(t, carry):
            tile_copy(t).start()
            return carry

        def twait(t, carry):
            tile_copy(0).wait()
            return carry

        lax.fori_loop(n_active_ref[0], n_tiles_total, tstart, 0)
        lax.fori_loop(n_active_ref[0], n_tiles_total, twait, 0)

        for e in range(N_EXPERTS):
            dst0 = tail_start_ref[e]

            def zstart(c, carry, dst0=dst0):
                tail_copy(dst0, c).start()
                return carry

            def zwait(c, carry):
                tail_copy(0, 0).wait()
                return carry

            lax.fori_loop(0, tail_chunks_ref[e], zstart, 0)
            lax.fori_loop(0, tail_chunks_ref[e], zwait, 0)


def _dispatch(plan, n2, gates, n_tiles, n_sorted):
    row = lambda w: pl.BlockSpec((_TT, w), lambda i, *_: (i, 0))
    return pl.pallas_call(
        _dispatch_kernel,
        grid_spec=pltpu.PrefetchScalarGridSpec(
            num_scalar_prefetch=7,
            grid=(n_tiles,),
            in_specs=[row(D_MODEL), row(_ROUTER_LANES),
                      pl.BlockSpec((1, 1, _ROUTER_LANES), lambda i, *_: (i, 0, 0))],
            out_specs=pl.BlockSpec(memory_space=pl.ANY),
            scratch_shapes=[pltpu.VMEM((2, _SLOTS, D_MODEL), BF16), pltpu.VMEM((_TM, D_MODEL), BF16),
                            pltpu.SemaphoreType.DMA((2,)), pltpu.SemaphoreType.DMA],
        ),
        out_shape=jax.ShapeDtypeStruct((n_sorted, D_MODEL), BF16),
        compiler_params=_params(1),
        name="moe_dispatch",
    )(plan["nchunk"], plan["lstart"], plan["roff"], plan["tile_chunks"], plan["tail_start"], plan["tail_chunks"],
      plan["n_active"], n2, gates, plan["lstart_vec"])


def _experts_kernel(tile_expert_ref, tile_block_ref, n_active_ref, slot_ref, next_ref,
                    xs_ref, wg_hbm, wu_hbm, wd_hbm, ys_ref, wg_f32, wu_f32, wd_f32, wg_bf, wu_bf, wd_bf, sem):
    g = pl.program_id(0)
    active = g < n_active_ref[0]
    expert = tile_expert_ref[g]
    new_expert = (g == 0) | (expert != tile_expert_ref[jnp.maximum(g - 1, 0)])

    def weight_copies(e, slot):
        return [pltpu.make_async_copy(hbm.at[e], buf.at[slot], sem.at[slot])
                for hbm, buf in ((wg_hbm, wg_f32), (wu_hbm, wu_f32), (wd_hbm, wd_f32))]

    @pl.when(active & new_expert)
    def _switch_expert():
        slot = slot_ref[expert]

        @pl.when(g == 0)
        def _first():
            for copy in weight_copies(expert, slot):
                copy.start()

        for copy in weight_copies(expert, slot):
            copy.wait()
        wg_bf[...] = wg_f32[slot].astype(BF16)
        wu_bf[...] = wu_f32[slot].astype(BF16)
        wd_bf[...] = wd_f32[slot].astype(BF16)

        @pl.when(next_ref[expert] >= 0)
        def _prefetch():
            for copy in weight_copies(next_ref[expert], 1 - slot):
                copy.start()

    @pl.when(active)
    def _ffn():
        xs = xs_ref[...]
        gate = jnp.dot(xs, wg_bf[...], preferred_element_type=F32)
        up = jnp.dot(xs, wu_bf[...], preferred_element_type=F32)
        hidden = (gate * _sigmoid(gate) * up).astype(BF16)
        ys_ref[...] = jnp.dot(hidden, wd_bf[...], preferred_element_type=F32).astype(BF16)

    @pl.when(jnp.logical_not(active))
    def _unused_tile():
        ys_ref[...] = jnp.zeros_like(ys_ref)


def _experts(plan, xs, w_gate, w_up, w_down, max_tiles):
    rows_in = pl.BlockSpec((_TM, D_MODEL), lambda g, te, tb, *_: (tb[g], 0))
    rows_out = pl.BlockSpec((_TM, D_MODEL), lambda g, *_: (g, 0))
    weights = (w_gate, w_up, w_down)
    return pl.pallas_call(
        _experts_kernel,
        grid_spec=pltpu.PrefetchScalarGridSpec(
            num_scalar_prefetch=5,
            grid=(max_tiles,),
            in_specs=[rows_in] + [pl.BlockSpec(memory_space=pl.ANY)] * len(weights),
            out_specs=rows_out,
            scratch_shapes=[pltpu.VMEM((2,) + w.shape[1:], F32) for w in weights]
            + [pltpu.VMEM(w.shape[1:], BF16) for w in weights] + [pltpu.SemaphoreType.DMA((2,))],
        ),
        out_shape=jax.ShapeDtypeStruct(xs.shape, BF16),
        compiler_params=_params(1),
        name="moe_experts",
    )(plan["tile_expert"], plan["tile_block"], plan["n_active"], plan["expert_slot"], plan["expert_next"],
      xs, w_gate, w_up, w_down)


def _combine_kernel(nchunk_ref, lstart_ref, roff_ref, tchunks_ref,
                    gates_ref, lvec_ref, h1_ref, g_ref, ys_ref, o_ref, yloc_ref, sem):
    i = pl.program_id(0)
    buf = i % 2

    def run_copy(b, src0, dst0, first, chunks):
        src = pl.multiple_of(src0 + first * _CHUNK, _CHUNK)
        dst = pl.multiple_of(dst0 + first * _CHUNK, _CHUNK)
        rows = chunks * _CHUNK
        return pltpu.make_async_copy(ys_ref.at[pl.ds(src, rows), :], yloc_ref.at[b, pl.ds(dst, rows), :], sem.at[b])

    def fetch(b, step):
        for e in range(N_EXPERTS):
            idx = step * N_EXPERTS + e
            src0, dst0 = roff_ref[idx], lstart_ref[idx]
            _for_each_piece(nchunk_ref[idx], _RUN_CHUNKS,
                            lambda first, chunks, src0=src0, dst0=dst0: run_copy(b, src0, dst0, first, chunks).start())

    @pl.when(i == 0)
    def _first():
        yloc_ref[...] = jnp.zeros_like(yloc_ref)
        fetch(buf, i)

    @pl.when(i + 1 < pl.num_programs(0))
    def _prefetch():
        fetch(1 - buf, i + 1)

    _for_each_piece(tchunks_ref[i], _TILE_CHUNKS, lambda first, chunks: run_copy(buf, 0, 0, 0, chunks).wait())

    gates = gates_ref[...]
    sel, pos = _slot_positions(gates, lvec_ref[0])
    pos_lo = jnp.where(sel, pos, _BIG)
    pos_hi = jnp.where(sel, pos, -1.0)
    lo = jnp.min(pos_lo, axis=-1, keepdims=True)
    hi = jnp.max(pos_hi, axis=-1, keepdims=True)
    w_lo = jnp.sum(jnp.where(pos_lo == lo, gates, 0.0), axis=-1, keepdims=True)
    w_hi = jnp.where(hi == lo, 0.0, jnp.sum(jnp.where(pos_hi == hi, gates, 0.0), axis=-1, keepdims=True))
    moe = None
    for s0 in range(0, _SLOTS, _SLOT_CHUNK):
        slot = (lax.broadcasted_iota(jnp.int32, (_TT, _SLOT_CHUNK), 1) + s0).astype(F32)
        weights = (jnp.where(slot == lo, w_lo, 0.0) + jnp.where(slot == hi, w_hi, 0.0)).astype(BF16)
        part = jnp.dot(weights, yloc_ref[buf, s0:s0 + _SLOT_CHUNK, :], preferred_element_type=F32)
        moe = part if moe is None else moe + part
    o_ref[...] = _rms_norm(h1_ref[...] + moe, g_ref[...])


def _combine(plan, gates, h1, g, ys, n_tiles):
    row = lambda w: pl.BlockSpec((_TT, w), lambda i, *_: (i, 0))
    return pl.pallas_call(
        _combine_kernel,
        grid_spec=pltpu.PrefetchScalarGridSpec(
            num_scalar_prefetch=4,
            grid=(n_tiles,),
            in_specs=[row(_ROUTER_LANES), pl.BlockSpec((1, 1, _ROUTER_LANES), lambda i, *_: (i, 0, 0)),
                      row(D_MODEL), pl.BlockSpec(g.shape, lambda i, *_: (0, 0)),
                      pl.BlockSpec(memory_space=pl.ANY)],
            out_specs=row(D_MODEL),
            scratch_shapes=[pltpu.VMEM((2, _SLOTS, D_MODEL), BF16), pltpu.SemaphoreType.DMA((2,))],
        ),
        out_shape=jax.ShapeDtypeStruct(h1.shape, F32),
        compiler_params=_params(1),
        name="moe_combine",
    )(plan["nchunk"], plan["lstart"], plan["roff"], plan["tile_chunks"], gates, plan["lstart_vec"], h1, g, ys)


def _moe(n2, gates, counts, h1, w_gate, w_up, w_down, g):
    n_rows = n2.shape[0]
    assert n_rows % _TT == 0 and _SLOTS >= 2 * _TT + N_EXPERTS * (_CHUNK - 1)
    n_tiles = n_rows // _TT
    assert counts.shape[0] == n_tiles
    plan, max_tiles = _moe_plan(counts, n_tiles)
    xs = _dispatch(plan, n2, gates, n_tiles, max_tiles * _TM)
    ys = _experts(plan, xs, w_gate, w_up, w_down, max_tiles)
    return _combine(plan, gates, h1, g, ys, n_tiles)


def kernel(x, meta_tokens, norm_mix_g, w_in, conv_w, conv_b, lru_w_a, lru_b_a, lru_w_x, lru_b_x, lru_lambda, attn_sink, w_attn_branch, w_rec_branch, w_out, norm_ffn_g, w_group, b_group, w_router, b_router, moe_w_gate, moe_w_up, moe_w_down, final_norm_g):
    batch, seq, _ = x.shape
    assert norm_mix_g.shape[0] == 1, "single-layer block"
    assert seq % _TQ == 0 and seq % _TC == 0
    n_rows = batch * seq
    x2 = x.reshape(n_rows, D_MODEL)
    row = lambda a: a.reshape(1, -1).astype(F32)

    g_mix = row(norm_mix_g[0])
    w_in_bf, z_meta = _meta_proj(meta_tokens.astype(F32), g_mix, w_in[0].astype(F32))
    k_meta = z_meta[:, ATTN_WIDTH:ATTN_WIDTH + KV_WIDTH].astype(BF16)
    v_meta = z_meta[:, ATTN_WIDTH + KV_WIDTH:ATTN_WIDTH + 2 * KV_WIDTH].astype(BF16)
    xr_meta = z_meta[:, ATTN_WIDTH + 2 * KV_WIDTH:ATTN_WIDTH + 2 * KV_WIDTH + LRU_WIDTH]
    q, k, v, xr, gy, ga, gr = _in_proj(x2, g_mix, w_in_bf, 512)

    sink_rows = jnp.broadcast_to((attn_sink[0].astype(F32) * HEAD_DIM ** 0.5)[:, None, None],
                                 (N_HEADS, BLOCK, BLOCK)).reshape(N_HEADS * BLOCK, BLOCK)
    shape3 = lambda a: a.reshape(batch, seq, a.shape[-1])
    pad_keys = lambda a: jnp.pad(a, ((0, BLOCK - N_META), (0, 0)))
    attn = _attention(shape3(q), shape3(k), shape3(v), pad_keys(k_meta), pad_keys(v_meta), sink_rows, shape3(ga),
                      w_attn_branch[0].astype(F32))

    h_dirs = []
    for d, reverse in enumerate((False, True)):
        wg = jnp.concatenate([lru_w_a[0, d], lru_w_x[0, d]], axis=-1).astype(BF16)
        bg_half = 0.5 * jnp.stack([lru_b_a[0, d], lru_b_x[0, d]]).astype(F32)
        h_dirs.append(_lru(shape3(xr), xr_meta, 0.5 * conv_w[0].astype(F32), 0.5 * row(conv_b[0]), wg, bg_half,
                           row(lru_lambda[0, d]), reverse))

    w_route = jnp.concatenate([w_group[0], w_router[0]], axis=1).astype(F32)
    w_route = jnp.pad(w_route, ((0, 0), (0, _ROUTER_LANES - w_route.shape[1])))
    wr_hi = w_route.astype(BF16)
    wr_lo = (w_route - wr_hi.astype(F32)).astype(BF16)
    b_route = jnp.pad(jnp.concatenate([b_group[0], b_router[0]]).astype(F32),
                      (0, _ROUTER_LANES - N_GROUPS - N_EXPERTS)).reshape(1, _ROUTER_LANES)
    h1, n2, gates, counts = _merge(x2, h_dirs[0].reshape(n_rows, LRU_WIDTH), h_dirs[1].reshape(n_rows, LRU_WIDTH),
                                   gy, attn.reshape(n_rows, D_MODEL), gr,
                                   w_rec_branch[0].astype(F32), w_out[0].astype(F32), row(norm_ffn_g[0]),
                                   wr_hi, wr_lo, b_route, _TT)

    out = _moe(n2, gates, counts, h1, moe_w_gate[0].astype(F32), moe_w_up[0].astype(F32),
               moe_w_down[0].astype(F32), row(final_norm_g))
    return out.reshape(batch, seq, D_MODEL)
```
